```python
import math
import jax
import jax.numpy as jnp
from jax import lax
import numpy as np

D_MODEL = 2048
BATCH = 4
SEQ = 2048
DEPTH = 4
DEC_BATCH = 8
DEC_SEQ = 1
PAST_LEN = 16384
PAGE_SIZE = 128

N_MIXERS = 4
HEAD_DIM = 128
PLE_DIM = 256
Q_BLOCK = 128
RMS_EPS = 1e-6
NEG_INF = -1e30
FORCE_SCORE = 1e9

A_HEADS = D_MODEL // HEAD_DIM
A_WIDTH = A_HEADS * HEAD_DIM
A_IN = 4 * A_WIDTH + A_HEADS

B_HEADS = D_MODEL // HEAD_DIM
B_KV = 2
B_REP = B_HEADS // B_KV
B_WIDTH = B_HEADS * HEAD_DIM
B_KV_WIDTH = B_KV * HEAD_DIM
N_BRANCH_KV = 6
B_IN = 2 * B_WIDTH + N_BRANCH_KV * B_KV_WIDTH + 3 * B_HEADS
CMP_BLOCK = 64
CMP_HIDDEN = 256
N_SEL = 16
WINDOW = 512
SEL_Q_BLOCK = 32

C_DK = 128
C_HEADS = D_MODEL // C_DK
C_DV = D_MODEL // C_HEADS
C_WK = C_HEADS * C_DK
C_WV = C_HEADS * C_DV
C_IN = 2 * C_WK + 2 * C_WV
C_CHUNK = 32

D_HEADS = D_MODEL // HEAD_DIM
D_WIDTH = D_HEADS * HEAD_DIM
D_IN = 4 * D_WIDTH

N_BUCKETS = 32
MAX_DISTANCE = 128

kernel_name = "hybrid_fox_nsa_hgrn2_stickbreak_step"


def _n_layers_of(m):
    return len(range(m, DEPTH, N_MIXERS))


def rms_norm(x, g):
    xf = x.astype(jnp.float32)
    y = xf * lax.rsqrt(jnp.mean(xf * xf, axis=-1, keepdims=True) + RMS_EPS)
    return (y * g.astype(jnp.float32)).astype(x.dtype)


def gather_pages(pool, layer, page_table):
    g = pool[layer, page_table]
    return g.reshape((g.shape[0], g.shape[1] * g.shape[2]) + g.shape[3:])


def t5_bucket(dist):
    exact = N_BUCKETS // 2
    d = jnp.maximum(dist, 0)
    logd = jnp.log(jnp.maximum(d, 1).astype(jnp.float32) / exact)
    large = exact + (logd / math.log(MAX_DISTANCE / exact) * (N_BUCKETS - exact)).astype(jnp.int32)
    return jnp.where(d < exact, d, jnp.minimum(large, N_BUCKETS - 1))


def sweep(fn, block, *qargs):
    T = qargs[0].shape[1]
    if T <= block or T % block:
        return fn(*qargs)
    nb = T // block
    split = lambda a: jnp.moveaxis(a.reshape((a.shape[0], nb, block) + a.shape[2:]), 1, 0)
    out = lax.map(lambda a: fn(*a), tuple(split(a) for a in qargs))
    out = jnp.moveaxis(out, 0, 1)
    return out.reshape((out.shape[0], T) + out.shape[3:])


def fox_mixer(u, k_past, v_past, logf_past, w_in, b_f, w_out):
    B, T, _ = u.shape
    L = k_past.shape[1]
    proj = u @ w_in
    q, k, v = (proj[..., n * A_WIDTH:(n + 1) * A_WIDTH].reshape(B, T, A_HEADS, HEAD_DIM) for n in range(3))
    z = proj[..., 3 * A_WIDTH:4 * A_WIDTH]
    logf = jax.nn.log_sigmoid(proj[..., 4 * A_WIDTH:].astype(jnp.float32) + b_f.astype(jnp.float32))
    k_past, v_past = k_past.astype(u.dtype), v_past.astype(u.dtype)
    cum = jnp.cumsum(jnp.concatenate([logf_past.astype(jnp.float32), logf], axis=1), axis=1)
    cum_k = jnp.moveaxis(cum, 2, 1)
    k_pos = jnp.arange(L + T)
    q_pos = jnp.broadcast_to(L + jnp.arange(T), (B, T))
    scale = HEAD_DIM ** -0.5

    def block(qb, cqb, pqb):
        s = jnp.concatenate([jnp.einsum('bqhd,bkhd->bhqk', qb, k_past),
                             jnp.einsum('bqhd,bkhd->bhqk', qb, k)], axis=-1).astype(jnp.float32) * scale
        s = s + jnp.moveaxis(cqb, 2, 1)[..., None] - cum_k[:, :, None, :]
        s = jnp.where(k_pos[None, None, None, :] <= pqb[:, None, :, None], s, NEG_INF)
        pr = jax.nn.softmax(s, axis=-1).astype(u.dtype)
        return (jnp.einsum('bhqk,bkhd->bqhd', pr[..., :L], v_past)
                + jnp.einsum('bhqk,bkhd->bqhd', pr[..., L:], v))

    o = sweep(block, Q_BLOCK, q, cum[:, L:], q_pos)
    o = o.reshape(B, T, A_WIDTH) * jax.nn.silu(z)
    return o.astype(u.dtype) @ w_out, (k, v, logf.astype(u.dtype))


def compress_blocks(x, pos, w1, w2):
    B, Lk, G, dh = x.shape
    n = Lk // CMP_BLOCK
    blocks = x[:, :n * CMP_BLOCK].reshape(B, n, CMP_BLOCK, G, dh) + pos[:, None, :].astype(x.dtype)
    flat = jnp.transpose(blocks, (0, 1, 3, 2, 4)).reshape(B, n, G, CMP_BLOCK * dh)
    return jax.nn.silu(flat @ w1) @ w2


def select_branch(q, k_all, v_all, sel_idx, q_pos, t5_table):
    B, Lk, G, dh = k_all.shape
    R = q.shape[3]
    n_blk = -(-Lk // CMP_BLOCK)
    pad = n_blk * CMP_BLOCK - Lk

    def to_blocks(a):
        a = jnp.pad(a, ((0, 0), (0, pad), (0, 0), (0, 0)))
        return jnp.transpose(a.reshape(B, n_blk, CMP_BLOCK, G, dh), (0, 3, 1, 2, 4))

    kb, vb = to_blocks(k_all), to_blocks(v_all)
    table_g = jnp.transpose(t5_table.reshape(N_BUCKETS, G, R), (1, 0, 2))
    b_ix = jnp.arange(B)[:, None, None, None]
    g_ix = jnp.arange(G)[None, None, :, None]
    offs = jnp.arange(CMP_BLOCK)
    scale = HEAD_DIM ** -0.5

    def one(q_blk, idx_blk, pos_blk):
        kg = kb[b_ix, g_ix, idx_blk]
        vg = vb[b_ix, g_ix, idx_blk]
        dist = pos_blk[:, :, None, None, None] - (idx_blk[..., None] * CMP_BLOCK + offs)
        bias = jnp.moveaxis(table_g[g_ix[..., None], t5_bucket(dist)], -1, 3)
        s = jnp.einsum('bqgrd,bqgkcd->bqgrkc', q_blk, kg).astype(jnp.float32) * scale + bias
        s = jnp.where((dist >= 0)[:, :, :, None], s, NEG_INF)
        pr = jax.nn.softmax(s.reshape(s.shape[:4] + (-1,)), axis=-1).reshape(s.shape).astype(vg.dtype)
        return jnp.einsum('bqgrkc,bqgkcd->bqgrd', pr, vg)

    return sweep(one, SEL_Q_BLOCK, q, sel_idx, q_pos)


def window_branch(q, k_ext, v_ext, q_start, t5_table):
    B, T, G, R, dh = q.shape
    qb = Q_BLOCK if (T > Q_BLOCK and T % Q_BLOCK == 0) else T
    span = WINDOW + qb
    scale = HEAD_DIM ** -0.5

    def one(i):
        q_blk = lax.dynamic_slice_in_dim(q, i * qb, qb, axis=1)
        k_blk = lax.dynamic_slice_in_dim(k_ext, i * qb, span, axis=1)
        v_blk = lax.dynamic_slice_in_dim(v_ext, i * qb, span, axis=1)
        t_pos = q_start + i * qb + jnp.arange(qb)
        s_pos = q_start - WINDOW + i * qb + jnp.arange(span)
        dist = t_pos[:, None] - s_pos[None, :]
        bias = jnp.transpose(t5_table[t5_bucket(dist)].reshape(qb, span, G, R), (2, 3, 0, 1))
        s = jnp.einsum('bqgrd,bkgd->bgrqk', q_blk, k_blk).astype(jnp.float32) * scale + bias
        mask = (dist >= 0) & (dist < WINDOW) & (s_pos[None, :] >= 0)
        pr = jax.nn.softmax(jnp.where(mask, s, NEG_INF), axis=-1).astype(v_blk.dtype)
        return jnp.einsum('bgrqk,bkgd->bqgrd', pr, v_blk)

    out = jnp.moveaxis(lax.map(one, jnp.arange(T // qb)), 0, 1)
    return out.reshape((B, T) + out.shape[3:])


def nsa_mixer(u, kv_past, win_past, w_in, cmp_pos, cmp_w1, cmp_w2, t5_table, w_out):
    B, T, _ = u.shape
    L, Lw = kv_past.shape[1], win_past.shape[1]
    proj = u @ w_in
    off_kv = B_WIDTH
    off_gate = off_kv + N_BRANCH_KV * B_KV_WIDTH
    off_z = off_gate + 3 * B_HEADS
    q = proj[..., :off_kv].reshape(B, T, B_KV, B_REP, HEAD_DIM)
    kvs = proj[..., off_kv:off_gate].reshape(B, T, N_BRANCH_KV, B_KV, HEAD_DIM)
    gates = jax.nn.sigmoid(proj[..., off_gate:off_z].astype(jnp.float32)).reshape(B, T, B_KV, B_REP, 3)
    z = proj[..., off_z:]
    kv_new, win_new = kvs[:, :, :4], kvs[:, :, 4:]
    kv_all = jnp.concatenate([kv_past.astype(u.dtype), kv_new], axis=1)
    Lk = L + T
    q_pos = jnp.broadcast_to(L + jnp.arange(T), (B, T))
    scale = HEAD_DIM ** -0.5
    k_cmp = compress_blocks(kv_all[:, :, 0], cmp_pos[0], cmp_w1[0], cmp_w2[0])
    v_cmp = compress_blocks(kv_all[:, :, 1], cmp_pos[1], cmp_w1[1], cmp_w2[1])
    n_cmp = k_cmp.shape[1]
    blk_end = (jnp.arange(n_cmp) + 1) * CMP_BLOCK - 1
    visible = (blk_end[None, None, :] <= q_pos[:, :, None])[:, :, None, None, :]
    s_cmp = jnp.einsum('btgrd,bngd->btgrn', q, k_cmp).astype(jnp.float32) * scale
    p_cmp = jax.nn.softmax(jnp.where(visible, s_cmp, NEG_INF), axis=-1) * visible
    o_cmp = jnp.einsum('btgrn,bngd->btgrd', p_cmp.astype(u.dtype), v_cmp)
    n_blk = -(-Lk // CMP_BLOCK)
    imp = jnp.pad(p_cmp.sum(axis=3), ((0, 0), (0, 0), (0, 0), (0, n_blk - n_cmp)))
    blk = jnp.arange(n_blk)
    cur = (q_pos // CMP_BLOCK)[:, :, None, None]
    forced = (blk == 0) | (blk == cur) | (blk == cur - 1)
    score = jnp.where(blk > cur, NEG_INF, jnp.where(forced, FORCE_SCORE, imp))
    _, sel_idx = lax.top_k(score, min(N_SEL, n_blk))
    o_sel = select_branch(q, kv_all[:, :, 2], kv_all[:, :, 3], sel_idx, q_pos, t5_table)
    win_ext = jnp.concatenate([jnp.zeros((B, WINDOW - Lw, 2, B_KV, HEAD_DIM), u.dtype),
                               win_past.astype(u.dtype), win_new], axis=1)
    o_win = window_branch(q, win_ext[:, :, 0], win_ext[:, :, 1], L, t5_table)
    o = gates[..., 0:1] * o_cmp + gates[..., 1:2] * o_sel + gates[..., 2:3] * o_win
    o = o.reshape(B, T, B_WIDTH).astype(u.dtype) * jax.nn.silu(z)
    keep = min(WINDOW, Lw + T)
    win_state = jnp.concatenate([win_past.astype(u.dtype), win_new], axis=1)[:, -keep:]
    return o @ w_out, (kv_new, win_state)


def chunked_gla(q, k, v, log_g, s0):
    B, T, H, dk = q.shape
    dv = v.shape[-1]
    C = math.gcd(T, C_CHUNK)
    n = T // C

    def chunks(a):
        a = a.astype(jnp.float32).reshape(B, n, C, H, a.shape[-1])
        return jnp.transpose(a, (1, 0, 3, 2, 4))

    qc, kc, vc = chunks(q), chunks(k), chunks(v)
    bc = jnp.cumsum(chunks(log_g), axis=3)
    causal = jnp.tril(jnp.ones((C, C), bool))

    def step(S, inp):
        q_, k_, v_, b_ = inp
        qe = q_ * jnp.exp(b_)
        att = jnp.where(causal, jnp.einsum('bhtd,bhsd->bhts', qe, k_ * jnp.exp(-b_)), 0.0)
        o = jnp.einsum('bhts,bhsv->bhtv', att, v_) + jnp.einsum('bhtd,bhdv->bhtv', qe, S)
        b_last = b_[:, :, -1:, :]
        S = (jnp.exp(b_last[:, :, 0, :])[..., None] * S
             + jnp.einsum('bhsd,bhsv->bhdv', k_ * jnp.exp(b_last - b_), v_))
        return S, o

    s_final, o = lax.scan(step, s0.astype(jnp.float32), (qc, kc, vc, bc))
    o = jnp.transpose(o, (1, 0, 3, 2, 4)).reshape(B, T, H, dv)
    return o, s_final


def hgrn2_mixer(u, s_past, lb, w_in, norm_g, w_out):
    B, T, _ = u.shape
    proj = u @ w_in
    q = jax.nn.silu(proj[..., :C_WK]).reshape(B, T, C_HEADS, C_DK)
    f = proj[..., C_WK:2 * C_WK].astype(jnp.float32)
    i_in = proj[..., 2 * C_WK:2 * C_WK + C_WV].reshape(B, T, C_HEADS, C_DV)
    z = proj[..., 2 * C_WK + C_WV:]
    g = lb + (1.0 - lb) * jax.nn.sigmoid(f)
    log_g = jnp.log(g).reshape(B, T, C_HEADS, C_DK)
    k = (1.0 - g).reshape(B, T, C_HEADS, C_DK)
    o, s_new = chunked_gla(q, k, i_in, log_g, s_past)
    o = rms_norm(o, norm_g).reshape(B, T, C_WV).astype(u.dtype) * jax.nn.silu(z)
    return o @ w_out, (s_new,)


def sb_mixer(u, k_past, v_past, w_in, w_out):
    B, T, _ = u.shape
    L = k_past.shape[1]
    proj = u @ w_in
    q, k, v = (proj[..., n * D_WIDTH:(n + 1) * D_WIDTH].reshape(B, T, D_HEADS, HEAD_DIM) for n in range(3))
    z = proj[..., 3 * D_WIDTH:]
    k_past, v_past = k_past.astype(u.dtype), v_past.astype(u.dtype)
    k_pos = jnp.arange(L + T)
    q_pos = jnp.broadcast_to(L + jnp.arange(T), (B, T))
    scale = HEAD_DIM ** -0.5

    def block(qb, pqb):
        logit = jnp.concatenate([jnp.einsum('bqhd,bkhd->bhqk', qb, k_past),
                                 jnp.einsum('bqhd,bkhd->bhqk', qb, k)], axis=-1).astype(jnp.float32) * scale
        before = k_pos[None, None, None, :] < pqb[:, None, :, None]
        log_stay = jnp.where(before, jax.nn.log_sigmoid(-logit), 0.0)
        log_w = jax.nn.log_sigmoid(logit) + lax.cumsum(log_stay, axis=3, reverse=True) - log_stay
        w = jnp.where(before, jnp.exp(log_w), 0.0).astype(u.dtype)
        return (jnp.einsum('bhqk,bkhd->bqhd', w[..., :L], v_past)
                + jnp.einsum('bhqk,bkhd->bqhd', w[..., L:], v))

    o = sweep(block, Q_BLOCK, q, q_pos)
    o = o.reshape(B, T, D_WIDTH) * jax.nn.silu(z)
    return o.astype(u.dtype) @ w_out, (k, v)


def setup_inputs(seed: int = 0) -> dict:
    key = jax.random.key(seed)
    keys = iter(jax.random.split(key, 48))
    f32 = jnp.float32

    def nrm(shape, scale=1.0):
        return scale * jax.random.normal(next(keys), shape, f32)

    def unif(shape, lo, hi):
        return jax.random.uniform(next(keys), shape, f32, lo, hi)

    n_a, n_b, n_c, n_d = (_n_layers_of(m) for m in range(N_MIXERS))
    n_pages = PAST_LEN // PAGE_SIZE
    n_used = DEC_BATCH * n_pages
    n_pool = n_used + max(1, n_used // 4)
    perm = jax.random.permutation(next(keys), n_pool)
    page_table = perm[:n_used].reshape(DEC_BATCH, n_pages).astype(jnp.int32)
    w_buf = min(WINDOW, PAST_LEN)
    d_in = D_MODEL ** -0.5
    return {
        "x_prompt": nrm((BATCH, SEQ, D_MODEL)),
        "x_sample": nrm((DEC_BATCH, DEC_SEQ, D_MODEL)),
        "cache_fox_k": nrm((n_a, n_pool, PAGE_SIZE, A_HEADS, HEAD_DIM)),
        "cache_fox_v": nrm((n_a, n_pool, PAGE_SIZE, A_HEADS, HEAD_DIM)),
        "cache_fox_logf": jax.nn.log_sigmoid(nrm((n_a, n_pool, PAGE_SIZE, A_HEADS)) + 2.5),
        "cache_nsa_kv": nrm((n_b, n_pool, PAGE_SIZE, 4, B_KV, HEAD_DIM)),
        "state_nsa_win": nrm((n_b, DEC_BATCH, w_buf, 2, B_KV, HEAD_DIM)),
        "state_hgrn": nrm((n_c, DEC_BATCH, C_HEADS, C_DK, C_DV), 0.5),
        "cache_sb_k": nrm((n_d, n_pool, PAGE_SIZE, D_HEADS, HEAD_DIM)),
        "cache_sb_v": nrm((n_d, n_pool, PAGE_SIZE, D_HEADS, HEAD_DIM)),
        "page_table": page_table,
        "p_prompt": nrm((DEPTH, BATCH, SEQ, PLE_DIM)),
        "p_sample": nrm((DEPTH, DEC_BATCH, DEC_SEQ, PLE_DIM)),
        "norm_g": 1.0 + nrm((DEPTH, D_MODEL), 0.02),
        "final_norm_g": 1.0 + nrm((D_MODEL,), 0.02),
        "ple_gate_w": nrm((DEPTH, D_MODEL, D_MODEL), d_in),
        "ple_proj_w": nrm((DEPTH, PLE_DIM, D_MODEL), PLE_DIM ** -0.5),
        "fox_w_in": nrm((n_a, D_MODEL, A_IN), d_in),
        "fox_b_f": unif((n_a, A_HEADS), 1.0, 4.0),
        "fox_w_out": nrm((n_a, A_WIDTH, D_MODEL), A_WIDTH ** -0.5),
        "nsa_w_in": nrm((n_b, D_MODEL, B_IN), d_in),
        "nsa_cmp_pos": nrm((n_b, 2, CMP_BLOCK, HEAD_DIM), 0.2),
        "nsa_cmp_w1": nrm((n_b, 2, CMP_BLOCK * HEAD_DIM, CMP_HIDDEN), (CMP_BLOCK * HEAD_DIM) ** -0.5),
        "nsa_cmp_w2": nrm((n_b, 2, CMP_HIDDEN, HEAD_DIM), CMP_HIDDEN ** -0.5),
        "t5_bias": nrm((N_BUCKETS, B_HEADS), 0.5),
        "nsa_w_out": nrm((n_b, B_WIDTH, D_MODEL), B_WIDTH ** -0.5),
        "hgrn_w_in": nrm((n_c, D_MODEL, C_IN), d_in),
        "hgrn_lower_bounds": 1.0 + nrm((DEPTH, C_WK), 0.1),
        "hgrn_norm_g": 1.0 + nrm((n_c, C_DV), 0.02),
        "hgrn_w_out": nrm((n_c, C_WV, D_MODEL), C_WV ** -0.5),
        "sb_w_in": nrm((n_d, D_MODEL, D_IN), d_in),
        "sb_w_out": nrm((n_d, D_WIDTH, D_MODEL), D_WIDTH ** -0.5),
    }


def reference(x_prompt, x_sample, cache_fox_k, cache_fox_v, cache_fox_logf, cache_nsa_kv, state_nsa_win,
              state_hgrn, cache_sb_k, cache_sb_v, page_table, p_prompt, p_sample, norm_g, final_norm_g,
              ple_gate_w, ple_proj_w, fox_w_in, fox_b_f, fox_w_out, nsa_w_in, nsa_cmp_pos, nsa_cmp_w1,
              nsa_cmp_w2, t5_bias, nsa_w_out, hgrn_w_in, hgrn_lower_bounds, hgrn_norm_g, hgrn_w_out,
              sb_w_in, sb_w_out):
    lb = jax.nn.softmax(hgrn_lower_bounds.astype(jnp.float32), axis=0)
    lower_bounds = jnp.cumsum(lb, axis=0) - lb[0]

    def prompt_past(m, j):
        B, dt = x_prompt.shape[0], x_prompt.dtype
        if m == 0:
            return (jnp.zeros((B, 0, A_HEADS, HEAD_DIM), dt), jnp.zeros((B, 0, A_HEADS, HEAD_DIM), dt),
                    jnp.zeros((B, 0, A_HEADS), jnp.float32))
        if m == 1:
            return (jnp.zeros((B, 0, 4, B_KV, HEAD_DIM), dt), jnp.zeros((B, 0, 2, B_KV, HEAD_DIM), dt))
        if m == 2:
            return (jnp.zeros((B, C_HEADS, C_DK, C_DV), jnp.float32),)
        return (jnp.zeros((B, 0, D_HEADS, HEAD_DIM), dt), jnp.zeros((B, 0, D_HEADS, HEAD_DIM), dt))

    def sample_past(m, j):
        if m == 0:
            return (gather_pages(cache_fox_k, j, page_table), gather_pages(cache_fox_v, j, page_table),
                    gather_pages(cache_fox_logf, j, page_table))
        if m == 1:
            return (gather_pages(cache_nsa_kv, j, page_table), state_nsa_win[j])
        if m == 2:
            return (state_hgrn[j],)
        return (gather_pages(cache_sb_k, j, page_table), gather_pages(cache_sb_v, j, page_table))

    def trunk(x, p, past):
        states = ([], [], [], [])
        for i in range(DEPTH):
            m, j = i % N_MIXERS, i // N_MIXERS
            u = rms_norm(x, norm_g[i])
            pst = past(m, j)
            if m == 0:
                out, new = fox_mixer(u, *pst, fox_w_in[j], fox_b_f[j], fox_w_out[j])
            elif m == 1:
                out, new = nsa_mixer(u, *pst, nsa_w_in[j], nsa_cmp_pos[j], nsa_cmp_w1[j], nsa_cmp_w2[j],
                                     t5_bias, nsa_w_out[j])
            elif m == 2:
                out, new = hgrn2_mixer(u, *pst, lower_bounds[i], hgrn_w_in[j], hgrn_norm_g[j], hgrn_w_out[j])
            else:
                out, new = sb_mixer(u, *pst, sb_w_in[j], sb_w_out[j])
            states[m].append(new)
            h = x + out
            x = h + jax.nn.sigmoid(h @ ple_gate_w[i]) * (p[i] @ ple_proj_w[i])
        stacked = [[jnp.stack(c) for c in zip(*states[m])] for m in range(N_MIXERS)]
        return rms_norm(x, final_norm_g), stacked

    y_prompt, sp = trunk(x_prompt, p_prompt, prompt_past)
    y_sample, ss = trunk(x_sample, p_sample, sample_past)
    return (y_prompt, y_sample,
            sp[0][0], ss[0][0], sp[0][1], ss[0][1], sp[0][2], ss[0][2],
            sp[1][0], ss[1][0], sp[1][1], ss[1][1],
            sp[2][0], ss[2][0],
            sp[3][0], ss[3][0], sp[3][1], ss[3][1])
```

```python
import functools
import math

import jax
import jax.numpy as jnp
from jax import lax
from jax.experimental import pallas as pl
from jax.experimental.pallas import tpu as pltpu

F32 = jnp.float32
BF16 = jnp.bfloat16
I32 = jnp.int32

HEAD_DIM = 128
RMS_EPS = 1e-6
NEG_INF = -1e30
FORCE_SCORE = 1e9
CMP_BLOCK = 64
N_SEL = 16
WINDOW = 512
N_BUCKETS = 32
MAX_DISTANCE = 128
GLA_CHUNK = 32
NSA_GROUPS = 2
NSA_REP = 8
NSA_KINDS = 4
SUBLANES = 8
SCALE = HEAD_DIM ** -0.5
MIB = 1 << 20


def _cparams(sem, vmem_mib=None):
    kw = dict(dimension_semantics=sem)
    if vmem_mib is not None:
        kw["vmem_limit_bytes"] = vmem_mib * MIB
    return pltpu.CompilerParams(**kw)


def _iota(shape, axis):
    return lax.broadcasted_iota(I32, shape, axis)


def _dot(a, b):
    return jnp.dot(a, b, preferred_element_type=F32)


def _dot_nt(a, b):
    return lax.dot_general(a, b, (((1,), (1,)), ((), ())), preferred_element_type=F32)


def _dot_tn(a, b):
    return lax.dot_general(a, b, (((0,), (0,)), ((), ())), preferred_element_type=F32)


def _split3(x):
    hi = x.astype(BF16)
    r1 = x - hi.astype(F32)
    mid = r1.astype(BF16)
    lo = (r1 - mid.astype(F32)).astype(BF16)
    return hi, mid, lo


def _dot3(x, m, dot=_dot):
    hi, mid, lo = _split3(x)
    return dot(hi, m) + dot(mid, m) + dot(lo, m)


def _dot3_rhs(m, x, dot=_dot):
    hi, mid, lo = _split3(x)
    return dot(m, hi) + dot(m, mid) + dot(m, lo)


def _log_sigmoid_pair(x):
    sp = jnp.log1p(jnp.exp(-jnp.abs(x)))
    return jnp.minimum(x, 0.0) - sp, jnp.minimum(-x, 0.0) - sp


def _sigmoid(x):
    return jax.nn.sigmoid(x)


def _t5_bucket(dist):
    exact = N_BUCKETS // 2
    d = jnp.maximum(dist, 0)
    logd = jnp.log(jnp.maximum(d, 1).astype(F32) * (1.0 / exact))
    large = exact + (logd * ((N_BUCKETS - exact) / math.log(MAX_DISTANCE / exact))).astype(I32)
    return jnp.where(d < exact, d, jnp.minimum(large, N_BUCKETS - 1))


def _div_pow2(x, n):
    assert n & (n - 1) == 0
    return lax.shift_right_arithmetic(x, n.bit_length() - 1)


def _row_tile(m, cap):
    t = min(m, cap)
    assert m % t == 0, (m, t)
    return t


def _norm_proj_kernel(x_ref, g_ref, w_ref, *rest, slab_blocks, transposed):
    outs, u_ref = rest[:-1], rest[-1]
    j = pl.program_id(1)

    @pl.when(j == 0)
    def _():
        x = x_ref[...]
        inv = lax.rsqrt(jnp.mean(x * x, axis=-1, keepdims=True) + RMS_EPS)
        u_ref[...] = (x * inv * g_ref[...]).astype(BF16)

    w = w_ref[...].astype(BF16)
    acc = _dot_nt(u_ref[...], w) if transposed else _dot(u_ref[...], w)
    for (lo, n), o_ref in zip(slab_blocks, outs):
        @pl.when((j >= lo) & (j < lo + n))
        def _(o_ref=o_ref):
            o_ref[...] = acc


def _norm_proj(x, g, w, slabs, tn, transposed=False):
    m, d = x.shape
    tm = _row_tile(m, 1024)
    base = slabs[0][0]
    slab_blocks, pos = [], base
    for off, width in slabs:
        assert off == pos and width % tn == 0 and off % tn == 0, (off, width, tn)
        slab_blocks.append(((off - base) // tn, width // tn))
        pos += width
    nj = (pos - base) // tn
    b0 = base // tn

    def out_map(lo, n):
        return lambda i, j: (i, jnp.minimum(jnp.maximum(j - lo, 0), n - 1))

    if transposed:
        w_spec = pl.BlockSpec((tn, d), lambda i, j: (b0 + j, 0))
    else:
        w_spec = pl.BlockSpec((d, tn), lambda i, j: (0, b0 + j))
    return pl.pallas_call(
        functools.partial(_norm_proj_kernel, slab_blocks=tuple(slab_blocks), transposed=transposed),
        grid=(m // tm, nj),
        in_specs=[pl.BlockSpec((tm, d), lambda i, j: (i, 0)), pl.BlockSpec((1, d), lambda i, j: (0, 0)), w_spec],
        out_specs=[pl.BlockSpec((tm, tn), out_map(lo, n)) for lo, n in slab_blocks],
        out_shape=[jax.ShapeDtypeStruct((m, width), F32) for _, width in slabs],
        scratch_shapes=[pltpu.VMEM((tm, d), BF16)],
        compiler_params=_cparams(("parallel", "arbitrary"), 56),
        name="norm_proj",
    )(x, g.reshape(1, d), w)


def _gated_out_kernel(o_ref, z_ref, w_ref, x_ref, out_ref, a_ref):
    @pl.when(pl.program_id(1) == 0)
    def _():
        z = z_ref[...]
        a_ref[...] = (o_ref[...] * (z * _sigmoid(z))).astype(BF16)

    out_ref[...] = x_ref[...] + _dot(a_ref[...], w_ref[...].astype(BF16))


def _gated_out(o, z, w, x, tn=512):
    m, d = o.shape
    n = w.shape[1]
    tm = _row_tile(m, 512)
    return pl.pallas_call(
        _gated_out_kernel,
        grid=(m // tm, n // tn),
        in_specs=[
            pl.BlockSpec((tm, d), lambda i, j: (i, 0)),
            pl.BlockSpec((tm, d), lambda i, j: (i, 0)),
            pl.BlockSpec((d, tn), lambda i, j: (0, j)),
            pl.BlockSpec((tm, tn), lambda i, j: (i, j)),
        ],
        out_specs=pl.BlockSpec((tm, tn), lambda i, j: (i, j)),
        out_shape=jax.ShapeDtypeStruct((m, n), F32),
        scratch_shapes=[pltpu.VMEM((tm, d), BF16)],
        compiler_params=_cparams(("parallel", "arbitrary"), 48),
        name="gated_out",
    )(o, z, w, x)


def _ple_kernel(h_ref, hs_ref, p_ref, wg_ref, wp_ref, o_ref, hb_ref, pb_ref):
    @pl.when(pl.program_id(1) == 0)
    def _():
        hb_ref[...] = h_ref[...].astype(BF16)
        pb_ref[...] = p_ref[...].astype(BF16)

    gate = _sigmoid(_dot(hb_ref[...], wg_ref[...].astype(BF16)))
    o_ref[...] = hs_ref[...] + gate * _dot(pb_ref[...], wp_ref[...].astype(BF16))


def _ple(h, wg, p, wp, tn=512):
    m, d = h.shape
    pd = p.shape[1]
    tm = _row_tile(m, 1024)
    return pl.pallas_call(
        _ple_kernel,
        grid=(m // tm, d // tn),
        in_specs=[
            pl.BlockSpec((tm, d), lambda i, j: (i, 0)),
            pl.BlockSpec((tm, tn), lambda i, j: (i, j)),
            pl.BlockSpec((tm, pd), lambda i, j: (i, 0)),
            pl.BlockSpec((d, tn), lambda i, j: (0, j)),
            pl.BlockSpec((pd, tn), lambda i, j: (0, j)),
        ],
        out_specs=pl.BlockSpec((tm, tn), lambda i, j: (i, j)),
        out_shape=jax.ShapeDtypeStruct((m, d), F32),
        scratch_shapes=[pltpu.VMEM((tm, d), BF16), pltpu.VMEM((tm, pd), BF16)],
        compiler_params=_cparams(("parallel", "arbitrary"), 48),
        name="ple",
    )(h, h, p, wg, wp)


def _rmsnorm_kernel(x_ref, g_ref, o_ref):
    x = x_ref[...]
    inv = lax.rsqrt(jnp.mean(x * x, axis=-1, keepdims=True) + RMS_EPS)
    o_ref[...] = x * inv * g_ref[...]


def _rmsnorm(x, g):
    m, d = x.shape
    tm = _row_tile(m, 512)
    return pl.pallas_call(
        _rmsnorm_kernel,
        grid=(m // tm,),
        in_specs=[pl.BlockSpec((tm, d), lambda i: (i, 0)), pl.BlockSpec((1, d), lambda i: (0, 0))],
        out_specs=pl.BlockSpec((tm, d), lambda i: (i, 0)),
        out_shape=jax.ShapeDtypeStruct((m, d), F32),
        compiler_params=_cparams(("parallel",)),
        name="final_rmsnorm",
    )(x, g.reshape(1, d))


def _fox_prep_kernel(x_ref, bf_ref, logf_ref, cum_ref, *, n_chunks):
    tri = (_iota((128, 128), 0) <= _iota((128, 128), 1)).astype(BF16)
    carry = jnp.zeros((x_ref.shape[0], 1), F32)
    for c in range(n_chunks):
        sl = slice(c * 128, (c + 1) * 128)
        lf, _ = _log_sigmoid_pair(x_ref[:, sl] + bf_ref[...])
        logf_ref[:, sl] = lf
        cum = _dot3(lf, tri) + carry
        cum_ref[:, sl] = cum
        carry = cum[:, 127:128]


def _fox_prep(logits_t, b_f):
    b, h, t = logits_t.shape
    assert t % 128 == 0
    return pl.pallas_call(
        functools.partial(_fox_prep_kernel, n_chunks=t // 128),
        grid=(b,),
        in_specs=[pl.BlockSpec((None, h, t), lambda i: (i, 0, 0)), pl.BlockSpec((h, 1), lambda i: (0, 0))],
        out_specs=[pl.BlockSpec((None, h, t), lambda i: (i, 0, 0))] * 2,
        out_shape=[jax.ShapeDtypeStruct((b, h, t), F32)] * 2,
        compiler_params=_cparams(("parallel",)),
        name="fox_prep",
    )(logits_t, b_f.reshape(h, 1))


def _row_to_col(row, n):
    eye = _iota((n, n), 0) == _iota((n, n), 1)
    return jnp.sum(jnp.where(eye, jnp.broadcast_to(row, (n, n)), 0.0), axis=1, keepdims=True)


def _fox_attn_kernel(q_ref, k_ref, v_ref, cum_ref, o_ref, *, tq):
    qi = pl.program_id(2)
    q = q_ref[...].astype(BF16)
    q0 = pl.multiple_of(qi * tq, tq)
    cq = _row_to_col(cum_ref[:, pl.ds(q0, tq)], tq)
    row = q0 + _iota((tq, tq), 0)

    def body(kc, carry):
        m, l, acc = carry
        ks = pl.multiple_of(kc * tq, tq)
        k = k_ref[pl.ds(ks, tq), :].astype(BF16)
        v = v_ref[pl.ds(ks, tq), :].astype(BF16)
        s = _dot_nt(q, k) * SCALE + cq - cum_ref[:, pl.ds(ks, tq)]
        ok = (ks + _iota((tq, tq), 1)) <= row
        s = jnp.where(ok, s, NEG_INF)
        m_new = jnp.maximum(m, jnp.max(s, axis=-1, keepdims=True))
        p = jnp.where(ok, jnp.exp(s - m_new), 0.0)
        alpha = jnp.exp(m - m_new)
        l = alpha * l + jnp.sum(p, axis=-1, keepdims=True)
        acc = alpha * acc + _dot(p.astype(BF16), v)
        return m_new, l, acc

    init = (jnp.full((tq, 1), NEG_INF, F32), jnp.zeros((tq, 1), F32), jnp.zeros((tq, HEAD_DIM), F32))
    _, l, acc = lax.fori_loop(0, qi + 1, body, init)
    o_ref[...] = acc / l


def _fox_attn(q, k, v, cum_t, tq=128):
    b, t, w = q.shape
    h = w // HEAD_DIM
    tq = min(tq, t)
    assert t % tq == 0
    return pl.pallas_call(
        functools.partial(_fox_attn_kernel, tq=tq),
        grid=(b, h, t // tq),
        in_specs=[
            pl.BlockSpec((None, tq, HEAD_DIM), lambda bi, hi, qi: (bi, qi, hi)),
            pl.BlockSpec((None, t, HEAD_DIM), lambda bi, hi, qi: (bi, 0, hi)),
            pl.BlockSpec((None, t, HEAD_DIM), lambda bi, hi, qi: (bi, 0, hi)),
            pl.BlockSpec((None, 1, t), lambda bi, hi, qi: (bi * h + hi, 0, 0)),
        ],
        out_specs=pl.BlockSpec((None, tq, HEAD_DIM), lambda bi, hi, qi: (bi, qi, hi)),
        out_shape=jax.ShapeDtypeStruct((b, t, w), F32),
        compiler_params=_cparams(("parallel", "parallel", "arbitrary")),
        name="fox_attn",
    )(q, k, v, cum_t)


def _sb_attn_kernel(q_ref, k_ref, v_ref, o_ref, *, tq):
    qi = pl.program_id(2)
    q = q_ref[...].astype(BF16)
    q0 = pl.multiple_of(qi * tq, tq)
    row = q0 + _iota((tq, tq), 0)
    later = (_iota((tq, tq), 0) > _iota((tq, tq), 1)).astype(BF16)

    def body(i, carry):
        tail, acc = carry
        ks = pl.multiple_of((qi - i) * tq, tq)
        k = k_ref[pl.ds(ks, tq), :].astype(BF16)
        v = v_ref[pl.ds(ks, tq), :].astype(BF16)
        x = _dot_nt(q, k) * SCALE
        before = (ks + _iota((tq, tq), 1)) < row
        ls_pos, ls_neg = _log_sigmoid_pair(x)
        log_stay = jnp.where(before, ls_neg, 0.0)
        w = jnp.where(before, jnp.exp(ls_pos + tail + _dot3(log_stay, later)), 0.0)
        acc = acc + _dot(w.astype(BF16), v)
        return tail + jnp.sum(log_stay, axis=-1, keepdims=True), acc

    init = (jnp.zeros((tq, 1), F32), jnp.zeros((tq, HEAD_DIM), F32))
    _, acc = lax.fori_loop(0, qi + 1, body, init)
    o_ref[...] = acc


def _sb_attn(q, k, v, tq=128):
    b, t, w = q.shape
    h = w // HEAD_DIM
    tq = min(tq, t)
    assert t % tq == 0
    return pl.pallas_call(
        functools.partial(_sb_attn_kernel, tq=tq),
        grid=(b, h, t // tq),
        in_specs=[
            pl.BlockSpec((None, tq, HEAD_DIM), lambda bi, hi, qi: (bi, qi, hi)),
            pl.BlockSpec((None, t, HEAD_DIM), lambda bi, hi, qi: (bi, 0, hi)),
            pl.BlockSpec((None, t, HEAD_DIM), lambda bi, hi, qi: (bi, 0, hi)),
        ],
        out_specs=pl.BlockSpec((None, tq, HEAD_DIM), lambda bi, hi, qi: (bi, qi, hi)),
        out_shape=jax.ShapeDtypeStruct((b, t, w), F32),
        compiler_params=_cparams(("parallel", "parallel", "arbitrary")),
        name="sb_attn",
    )(q, k, v)


def _head_expand(nh):
    return (_div_pow2(_iota((nh, nh * HEAD_DIM), 1), HEAD_DIM) == _iota((nh, nh * HEAD_DIM), 0)).astype(BF16)


def _page_scores(k_ref, qt, nh, page):
    lane = _iota((page, nh), 1)
    s = jnp.zeros((page, nh), F32)
    for h in range(nh):
        kh = k_ref[pl.ds(h, page, stride=nh), :].astype(BF16)
        s = jnp.where(lane == h, _dot(kh, qt), s)
    return s * SCALE


def _page_accumulate(v_ref, acc_ref, wgt, decay, nh, page):
    expand = _head_expand(nh)
    we = _dot(wgt.astype(BF16), expand)
    de = None if decay is None else _dot3(jnp.broadcast_to(decay, (SUBLANES, nh)), expand)
    for h in range(nh):
        sl = slice(h * HEAD_DIM, (h + 1) * HEAD_DIM)
        vh = v_ref[pl.ds(h, page, stride=nh), :]
        part = jnp.sum((we[:, sl] * vh).reshape(page // SUBLANES, SUBLANES, HEAD_DIM), axis=0)
        acc_ref[h] = part + (acc_ref[h] if de is None else de[:, sl] * acc_ref[h])


def _page_finish(o_ref, acc_ref, denom, nh):
    de = None if denom is None else _dot3(jnp.broadcast_to(denom, (SUBLANES, nh)), _head_expand(nh))
    for h in range(nh):
        sl = slice(h * HEAD_DIM, (h + 1) * HEAD_DIM)
        o = jnp.sum(acc_ref[h], axis=0, keepdims=True)
        o_ref[:, sl] = o if de is None else o / de[0:1, sl]


def _fox_dec_kernel(pt_ref, qt_ref, q_ref, kn_ref, vn_ref, fl_ref, bf_ref, k_ref, v_ref, lft_ref,
                    o_ref, lfo_ref, m_ref, l_ref, c_ref, acc_ref):
    p = pl.program_id(1)
    nh = qt_ref.shape[1]
    page = lft_ref.shape[1]

    @pl.when(p == 0)
    def _():
        kq = _dot_nt(kn_ref[...].astype(BF16), q_ref[...].astype(BF16))
        eye = _iota((nh, nh), 0) == _iota((nh, nh), 1)
        m_ref[...] = jnp.sum(jnp.where(eye, kq, 0.0), axis=0, keepdims=True) * SCALE
        l_ref[...] = jnp.ones_like(l_ref)
        lf_new, _ = _log_sigmoid_pair(fl_ref[...] + bf_ref[...])
        lfo_ref[...] = lf_new
        c_ref[...] = lf_new
        first = _iota((SUBLANES, HEAD_DIM), 0) == 0
        for h in range(nh):
            acc_ref[h] = jnp.where(first, jnp.broadcast_to(vn_ref[h:h + 1, :], (SUBLANES, HEAD_DIM)), 0.0)

    lft = lft_ref[...]
    later = (_iota((page, page), 1) > _iota((page, page), 0)).astype(BF16)
    s = _page_scores(k_ref, qt_ref[...].astype(BF16), nh, page)
    s = s + c_ref[...] + _dot3_rhs(later, lft, _dot_nt)
    m_old = m_ref[...]
    m_new = jnp.maximum(m_old, jnp.max(s, axis=0, keepdims=True))
    pr = jnp.exp(s - m_new)
    alpha = jnp.exp(m_old - m_new)
    l_ref[...] = alpha * l_ref[...] + jnp.sum(pr, axis=0, keepdims=True)
    m_ref[...] = m_new
    _page_accumulate(v_ref, acc_ref, pr, alpha, nh, page)
    total = _dot3_rhs(jnp.ones((SUBLANES, page), BF16), lft, _dot_nt)
    c_ref[...] = c_ref[...] + total[0:1, :]

    @pl.when(p == pl.num_programs(1) - 1)
    def _():
        _page_finish(o_ref, acc_ref, l_ref[...], nh)


def _fox_decode(q, k_new, v_new, fl, b_f, k_pool, v_pool, lf_pool_t, page_table):
    b, nh, _ = q.shape
    w = nh * HEAD_DIM
    n_pages = page_table.shape[1]
    page = lf_pool_t.shape[2]
    qt = jnp.swapaxes(q, 1, 2)

    def row(bi, p, pt):
        return (bi, 0, 0)

    def pg(bi, p, pt):
        return (pt[bi, n_pages - 1 - p], 0, 0)

    grid_spec = pltpu.PrefetchScalarGridSpec(
        num_scalar_prefetch=1,
        grid=(b, n_pages),
        in_specs=[
            pl.BlockSpec((None, HEAD_DIM, nh), row),
            pl.BlockSpec((None, nh, HEAD_DIM), row), pl.BlockSpec((None, nh, HEAD_DIM), row),
            pl.BlockSpec((None, nh, HEAD_DIM), row),
            pl.BlockSpec((None, 1, nh), row), pl.BlockSpec((1, nh), lambda bi, p, pt: (0, 0)),
            pl.BlockSpec((None, page * nh, HEAD_DIM), pg), pl.BlockSpec((None, page * nh, HEAD_DIM), pg),
            pl.BlockSpec((None, nh, page), pg),
        ],
        out_specs=[pl.BlockSpec((None, 1, w), row), pl.BlockSpec((None, 1, nh), row)],
        scratch_shapes=[pltpu.VMEM((1, nh), F32), pltpu.VMEM((1, nh), F32), pltpu.VMEM((1, nh), F32),
                        pltpu.VMEM((nh, SUBLANES, HEAD_DIM), F32)],
    )
    return pl.pallas_call(
        _fox_dec_kernel,
        grid_spec=grid_spec,
        out_shape=[jax.ShapeDtypeStruct((b, 1, w), F32), jax.ShapeDtypeStruct((b, 1, nh), F32)],
        compiler_params=_cparams(("parallel", "arbitrary")),
        name="fox_decode",
    )(page_table, qt, q, k_new, v_new, fl, b_f.reshape(1, nh), k_pool, v_pool, lf_pool_t)


def _sb_dec_kernel(pt_ref, qt_ref, k_ref, v_ref, o_ref, c_ref, acc_ref):
    p = pl.program_id(1)
    nh = qt_ref.shape[1]
    page = k_ref.shape[0] // nh

    @pl.when(p == 0)
    def _():
        c_ref[...] = jnp.zeros_like(c_ref)
        acc_ref[...] = jnp.zeros_like(acc_ref)

    x = _page_scores(k_ref, qt_ref[...].astype(BF16), nh, page)
    ls_pos, ls_neg = _log_sigmoid_pair(x)
    later = (_iota((page, page), 1) > _iota((page, page), 0)).astype(BF16)
    wgt = jnp.exp(ls_pos + c_ref[...] + _dot3_rhs(later, ls_neg))
    _page_accumulate(v_ref, acc_ref, wgt, None, nh, page)
    c_ref[...] = c_ref[...] + jnp.sum(ls_neg, axis=0, keepdims=True)

    @pl.when(p == pl.num_programs(1) - 1)
    def _():
        _page_finish(o_ref, acc_ref, None, nh)


def _sb_decode(q, k_pool, v_pool, page_table, page):
    b, nh, _ = q.shape
    w = nh * HEAD_DIM
    n_pages = page_table.shape[1]

    def row(bi, p, pt):
        return (bi, 0, 0)

    def pg(bi, p, pt):
        return (pt[bi, n_pages - 1 - p], 0, 0)

    grid_spec = pltpu.PrefetchScalarGridSpec(
        num_scalar_prefetch=1,
        grid=(b, n_pages),
        in_specs=[pl.BlockSpec((None, HEAD_DIM, nh), row), pl.BlockSpec((None, page * nh, HEAD_DIM), pg),
                  pl.BlockSpec((None, page * nh, HEAD_DIM), pg)],
        out_specs=pl.BlockSpec((None, 1, w), row),
        scratch_shapes=[pltpu.VMEM((1, nh), F32), pltpu.VMEM((nh, SUBLANES, HEAD_DIM), F32)],
    )
    return pl.pallas_call(
        _sb_dec_kernel,
        grid_spec=grid_spec,
        out_shape=jax.ShapeDtypeStruct((b, 1, w), F32),
        compiler_params=_cparams(("parallel", "arbitrary")),
        name="sb_decode",
    )(page_table, jnp.swapaxes(q, 1, 2), k_pool, v_pool)


def _cmp_gather_kernel(pt_ref, x_ref, o_ref, *, inner, page):
    for c in range(2 * NSA_GROUPS):
        o_ref[c] = x_ref[pl.ds(c, page, stride=inner), :]


def _cmp_gather(pool, page_table, page):
    b, n_pages = page_table.shape
    inner = pool.shape[1] // page
    grid_spec = pltpu.PrefetchScalarGridSpec(
        num_scalar_prefetch=1,
        grid=(b, n_pages),
        in_specs=[pl.BlockSpec((None, page * inner, HEAD_DIM), lambda bi, p, pt: (pt[bi, p], 0, 0))],
        out_specs=pl.BlockSpec((2 * NSA_GROUPS, page, HEAD_DIM), lambda bi, p, pt: (0, bi * n_pages + p, 0)),
    )
    return pl.pallas_call(
        functools.partial(_cmp_gather_kernel, inner=inner, page=page),
        grid_spec=grid_spec,
        out_shape=jax.ShapeDtypeStruct((2 * NSA_GROUPS, b * n_pages * page, HEAD_DIM), F32),
        compiler_params=_cparams(("parallel", "arbitrary")),
        name="nsa_cmp_gather",
    )(page_table, pool)


def _compress_kernel(x_ref, pos_ref, w1_ref, w2_ref, o_ref, *, rb):
    hid = jnp.zeros((rb, w1_ref.shape[2]), F32)
    for c in range(CMP_BLOCK):
        xc = x_ref[pl.ds(c, rb, stride=CMP_BLOCK), :] + pos_ref[c:c + 1, :]
        hid = hid + _dot(xc.astype(BF16), w1_ref[c].astype(BF16))
    hid = hid * _sigmoid(hid)
    o_ref[...] = _dot(hid.astype(BF16), w2_ref[...].astype(BF16))


def _compress(x, pos, w1, w2):
    r = x.shape[-2]
    nb = r // CMP_BLOCK
    rb = _row_tile(nb, 256)
    hidden = w1.shape[-1]
    w1v = w1.reshape(2, CMP_BLOCK, HEAD_DIM, hidden)
    if x.ndim == 2:
        x_spec = pl.BlockSpec((rb * CMP_BLOCK, HEAD_DIM), lambda c, i: (i, c))
    else:
        x_spec = pl.BlockSpec((None, rb * CMP_BLOCK, HEAD_DIM), lambda c, i: (c, i, 0))
    return pl.pallas_call(
        functools.partial(_compress_kernel, rb=rb),
        grid=(2 * NSA_GROUPS, nb // rb),
        in_specs=[
            x_spec,
            pl.BlockSpec((None, CMP_BLOCK, HEAD_DIM), lambda c, i: (c // NSA_GROUPS, 0, 0)),
            pl.BlockSpec((None, CMP_BLOCK, HEAD_DIM, hidden), lambda c, i: (c // NSA_GROUPS, 0, 0, 0)),
            pl.BlockSpec((None, hidden, HEAD_DIM), lambda c, i: (c // NSA_GROUPS, 0, 0)),
        ],
        out_specs=pl.BlockSpec((None, rb, HEAD_DIM), lambda c, i: (c, i, 0)),
        out_shape=jax.ShapeDtypeStruct((2 * NSA_GROUPS, nb, HEAD_DIM), F32),
        compiler_params=_cparams(("parallel", "arbitrary"), 48),
        name="nsa_compress",
    )(x, pos, w1v, w2)


def _gate_col(gl, col):
    lane = _iota(gl.shape, 1)
    return _sigmoid(jnp.sum(jnp.where(lane == col, gl, 0.0), axis=-1, keepdims=True))


def _nsa_cmp_kernel(q_ref, kc_ref, vc_ref, gate_ref, o_ref, sel_ref, *, past, tq, n_cmp, n_blk, t_keys, decode):
    g = pl.program_id(1)
    qi = pl.program_id(2)
    n_pad = kc_ref.shape[0]
    kc = kc_ref[...].astype(BF16)
    vc = vc_ref[...].astype(BF16)
    qpos = past + qi * tq + _iota((tq, 1), 0)
    blk = _iota((tq, n_pad), 1)
    visible = ((blk + 1) * CMP_BLOCK - 1 <= qpos) & (blk < n_cmp)
    gl = gate_ref[...]
    imp = jnp.zeros((tq, n_pad), F32)
    for r in range(NSA_REP):
        sl = slice(r * HEAD_DIM, (r + 1) * HEAD_DIM)
        s = jnp.where(visible, _dot_nt(q_ref[:, sl].astype(BF16), kc) * SCALE, NEG_INF)
        e = jnp.exp(s - jnp.max(s, axis=-1, keepdims=True))
        pr = jnp.where(visible, e / jnp.sum(e, axis=-1, keepdims=True), 0.0)
        imp = imp + pr
        o_ref[:, sl] = _gate_col(gl, (g * NSA_REP + r) * 3) * _dot(pr.astype(BF16), vc)

    cur = _div_pow2(qpos, CMP_BLOCK)
    forced = (blk == 0) | (blk == cur) | (blk == cur - 1)
    score = jnp.where(blk > cur, NEG_INF, jnp.where(forced, FORCE_SCORE, imp))

    def rank_body(i, rank):
        col = jnp.sum(jnp.where(blk == i, score, 0.0), axis=1, keepdims=True)
        first = (col > score) | ((col == score) & (i < blk))
        return rank + jnp.where(first, 1.0, 0.0)

    rank = lax.fori_loop(0, n_blk, rank_body, jnp.zeros((tq, n_pad), F32))
    if decode:
        blk_f = blk.astype(F32)
        lane = _iota((tq, N_SEL), 1)
        idx = jnp.zeros((tq, N_SEL), F32)
        for k in range(N_SEL):
            ik = jnp.sum(jnp.where(rank == k, blk_f, 0.0), axis=1, keepdims=True)
            idx = jnp.where(lane == k, ik, idx)
        sel_ref[...] = idx.astype(I32)
    else:
        sel = jnp.where(rank < N_SEL, 1.0, 0.0).astype(BF16)
        expand = (_div_pow2(_iota((n_pad, t_keys), 1), CMP_BLOCK) == _iota((n_pad, t_keys), 0)).astype(BF16)
        sel_ref[...] = _dot(sel, expand).astype(BF16)


def _nsa_cmp(q, kc, vc, gates, past, n_keys, decode):
    b, t, w = q.shape
    n_pad = kc.shape[2]
    n_cmp = n_keys // CMP_BLOCK
    n_blk = -(-n_keys // CMP_BLOCK)
    assert N_SEL <= n_blk <= n_pad
    tq = min(t, 128)
    gw = w // NSA_GROUPS
    if decode:
        sel_shape, sel_block, sel_dtype = (b, NSA_GROUPS, t, N_SEL), (None, None, tq, N_SEL), I32
    else:
        sel_shape, sel_block, sel_dtype = (b, NSA_GROUPS, t, t), (None, None, tq, t), BF16
    return pl.pallas_call(
        functools.partial(_nsa_cmp_kernel, past=past, tq=tq, n_cmp=n_cmp, n_blk=n_blk, t_keys=t, decode=decode),
        grid=(b, NSA_GROUPS, t // tq),
        in_specs=[
            pl.BlockSpec((None, tq, gw), lambda bi, g, qi: (bi, qi, g)),
            pl.BlockSpec((None, None, n_pad, HEAD_DIM), lambda bi, g, qi: (bi, g, 0, 0)),
            pl.BlockSpec((None, None, n_pad, HEAD_DIM), lambda bi, g, qi: (bi, g, 0, 0)),
            pl.BlockSpec((None, tq, gates.shape[-1]), lambda bi, g, qi: (bi, qi, 0)),
        ],
        out_specs=[
            pl.BlockSpec((None, tq, gw), lambda bi, g, qi: (bi, qi, g)),
            pl.BlockSpec(sel_block, lambda bi, g, qi: (bi, g, qi, 0)),
        ],
        out_shape=[jax.ShapeDtypeStruct((b, t, w), F32), jax.ShapeDtypeStruct(sel_shape, sel_dtype)],
        compiler_params=_cparams(("parallel", "parallel", "arbitrary")),
        name="nsa_cmp_select",
    )(q, kc, vc, gates)


def _t5_tiles_kernel(tab_ref, o_ref, *, tq):
    h = pl.program_id(0)
    d0 = _iota((tq, tq), 0) - _iota((tq, tq), 1)
    for t in range(3):
        bucket = _t5_bucket(d0 + t * tq)
        acc = jnp.zeros((tq, tq), F32)
        for n in range(N_BUCKETS):
            acc = jnp.where(bucket == n, tab_ref[n, h], acc)
        o_ref[t] = acc


def _t5_tiles(t5_bias, tq):
    nh = t5_bias.shape[1]
    assert tq >= MAX_DISTANCE
    return pl.pallas_call(
        functools.partial(_t5_tiles_kernel, tq=tq),
        grid=(nh,),
        in_specs=[pl.BlockSpec(memory_space=pltpu.SMEM)],
        out_specs=pl.BlockSpec((None, 3, tq, tq), lambda h: (h, 0, 0, 0)),
        out_shape=jax.ShapeDtypeStruct((nh, 3, tq, tq), F32),
        compiler_params=_cparams(("arbitrary",)),
        name="t5_tiles",
    )(t5_bias)


def _nsa_attn_kernel(*refs, tq, windowed, branch):
    if windowed:
        q_ref, k_ref, v_ref, tiles_ref, gate_ref, prev_ref, o_ref = refs
        mask_ref = None
    else:
        q_ref, k_ref, v_ref, tiles_ref, gate_ref, prev_ref, mask_ref, o_ref = refs
    h = pl.program_id(1)
    qi = pl.program_id(2)
    q = q_ref[...].astype(BF16)
    dist0 = _iota((tq, tq), 0) - _iota((tq, tq), 1)

    def body(kc, carry):
        m, l, acc = carry
        ks = pl.multiple_of(kc * tq, tq)
        k = k_ref[pl.ds(ks, tq), :].astype(BF16)
        v = v_ref[pl.ds(ks, tq), :].astype(BF16)
        s = _dot_nt(q, k) * SCALE + tiles_ref[jnp.minimum(qi - kc, 2)]
        dist = dist0 + (qi - kc) * tq
        ok = dist >= 0
        if windowed:
            ok = ok & (dist < WINDOW)
        else:
            ok = ok & (mask_ref[:, pl.ds(ks, tq)].astype(F32) > 0.5)
        s = jnp.where(ok, s, NEG_INF)
        m_new = jnp.maximum(m, jnp.max(s, axis=-1, keepdims=True))
        p = jnp.where(ok, jnp.exp(s - m_new), 0.0)
        alpha = jnp.exp(m - m_new)
        l = alpha * l + jnp.sum(p, axis=-1, keepdims=True)
        acc = alpha * acc + _dot(p.astype(BF16), v)
        return m_new, l, acc

    lo = jnp.maximum(qi - WINDOW // tq, 0) if windowed else 0
    init = (jnp.full((tq, 1), NEG_INF, F32), jnp.zeros((tq, 1), F32), jnp.zeros((tq, HEAD_DIM), F32))
    _, l, acc = lax.fori_loop(lo, qi + 1, body, init)
    o_ref[...] = prev_ref[...] + _gate_col(gate_ref[...], h * 3 + branch) * (acc / l)


def _nsa_attn(q, kv, k_blk, v_blk, tiles, gates, prev, mask, branch):
    b, t, w = q.shape
    nh = w // HEAD_DIM
    tq = tiles.shape[-1]
    assert t % tq == 0 and WINDOW % tq == 0
    windowed = mask is None
    in_specs = [
        pl.BlockSpec((None, tq, HEAD_DIM), lambda bi, h, qi: (bi, qi, h)),
        pl.BlockSpec((None, t, HEAD_DIM), lambda bi, h, qi: (bi, 0, k_blk + h // NSA_REP)),
        pl.BlockSpec((None, t, HEAD_DIM), lambda bi, h, qi: (bi, 0, v_blk + h // NSA_REP)),
        pl.BlockSpec((None, 3, tq, tq), lambda bi, h, qi: (h, 0, 0, 0)),
        pl.BlockSpec((None, tq, gates.shape[-1]), lambda bi, h, qi: (bi, qi, 0)),
        pl.BlockSpec((None, tq, HEAD_DIM), lambda bi, h, qi: (bi, qi, h)),
    ]
    args = [q, kv, kv, tiles, gates, prev]
    if not windowed:
        in_specs.append(pl.BlockSpec((None, None, tq, t), lambda bi, h, qi: (bi, h // NSA_REP, qi, 0)))
        args.append(mask)
    return pl.pallas_call(
        functools.partial(_nsa_attn_kernel, tq=tq, windowed=windowed, branch=branch),
        grid=(b, nh, t // tq),
        in_specs=in_specs,
        out_specs=pl.BlockSpec((None, tq, HEAD_DIM), lambda bi, h, qi: (bi, qi, h)),
        out_shape=jax.ShapeDtypeStruct((b, t, w), F32),
        compiler_params=_cparams(("parallel", "parallel", "arbitrary")),
        name="nsa_window" if windowed else "nsa_select",
    )(*args)


def _t5_rows(tab_t, dist):
    n = dist.shape[1]
    onehot = (_iota((N_BUCKETS, n), 0) == _t5_bucket(dist)).astype(BF16)
    return _dot3(tab_t, onehot)


def _pick_row(x, r):
    return jnp.sum(jnp.where(_iota(x.shape, 0) == r, x, 0.0), axis=0, keepdims=True)


def _nsa_sel_dec_kernel(pt_ref, idx_ref, q_ref, kv_ref, new_ref, tab_ref, gate_ref, prev_ref,
                        o_ref, m_ref, l_ref, acc_ref, *, past, n_past_blk, inner):
    bi, g, j = pl.program_id(0), pl.program_id(1), pl.program_id(2)
    blk = idx_ref[(bi * NSA_GROUPS + g) * N_SEL + j]

    @pl.when(j == 0)
    def _():
        m_ref[...] = jnp.full_like(m_ref, NEG_INF)
        l_ref[...] = jnp.zeros_like(l_ref)
        acc_ref[...] = jnp.zeros_like(acc_ref)

    k_row = 2 * NSA_GROUPS + g
    v_row = 3 * NSA_GROUPS + g
    is_new = blk >= n_past_blk
    new = new_ref[...]
    k_old = kv_ref[pl.ds(k_row, CMP_BLOCK, stride=inner), :]
    v_old = kv_ref[pl.ds(v_row, CMP_BLOCK, stride=inner), :]
    k = jnp.where(is_new, jnp.broadcast_to(_pick_row(new, k_row), k_old.shape), k_old).astype(BF16)
    v = jnp.where(is_new, jnp.broadcast_to(_pick_row(new, v_row), v_old.shape), v_old).astype(BF16)
    dist = past - (blk * CMP_BLOCK + _iota((1, CMP_BLOCK), 1))
    ok = dist >= 0
    s = _dot_nt(q_ref[...].astype(BF16), k) * SCALE + _t5_rows(tab_ref[...], dist)
    s = jnp.where(ok, s, NEG_INF)
    m_old = m_ref[...]
    m_new = jnp.maximum(m_old, jnp.max(s, axis=-1, keepdims=True))
    p = jnp.where(ok, jnp.exp(s - m_new), 0.0)
    alpha = jnp.exp(m_old - m_new)
    l_ref[...] = alpha * l_ref[...] + jnp.sum(p, axis=-1, keepdims=True)
    acc_ref[...] = alpha * acc_ref[...] + _dot(p.astype(BF16), v)
    m_ref[...] = m_new

    @pl.when(j == N_SEL - 1)
    def _():
        o_ref[...] = prev_ref[...] + _sigmoid(gate_ref[:, 1:2]) * (acc_ref[...] / l_ref[...])


def _nsa_sel_decode(q4, pool_blocks, kv_new, tab_t, gates4, prev4, sel_idx, page_table, past, page):
    b = q4.shape[0]
    assert past % page == 0 and page % CMP_BLOCK == 0
    n_past_blk = past // CMP_BLOCK
    per_page = page // CMP_BLOCK
    inner = pool_blocks.shape[1] // CMP_BLOCK

    def blk_row(bi, g, j, pt, idx):
        n = jnp.minimum(idx[(bi * NSA_GROUPS + g) * N_SEL + j], n_past_blk - 1)
        return pt[bi, n // per_page] * per_page + n % per_page

    def grp(bi, g, j, pt, idx):
        return (bi, g, 0, 0)

    grid_spec = pltpu.PrefetchScalarGridSpec(
        num_scalar_prefetch=2,
        grid=(b, NSA_GROUPS, N_SEL),
        in_specs=[
            pl.BlockSpec((None, None, NSA_REP, HEAD_DIM), grp),
            pl.BlockSpec((None, CMP_BLOCK * inner, HEAD_DIM), lambda bi, g, j, pt, idx: (blk_row(bi, g, j, pt, idx), 0, 0)),
            pl.BlockSpec((None, inner, HEAD_DIM), lambda bi, g, j, pt, idx: (bi, 0, 0)),
            pl.BlockSpec((NSA_REP, N_BUCKETS), lambda bi, g, j, pt, idx: (g, 0)),
            pl.BlockSpec((None, None, NSA_REP, 3), grp),
            pl.BlockSpec((None, None, NSA_REP, HEAD_DIM), grp),
        ],
        out_specs=pl.BlockSpec((None, None, NSA_REP, HEAD_DIM), grp),
        scratch_shapes=[pltpu.VMEM((NSA_REP, 1), F32), pltpu.VMEM((NSA_REP, 1), F32),
                        pltpu.VMEM((NSA_REP, HEAD_DIM), F32)],
    )
    return pl.pallas_call(
        functools.partial(_nsa_sel_dec_kernel, past=past, n_past_blk=n_past_blk, inner=inner),
        grid_spec=grid_spec,
        out_shape=jax.ShapeDtypeStruct(q4.shape, F32),
        compiler_params=_cparams(("parallel", "parallel", "arbitrary")),
        name="nsa_select_decode",
    )(page_table, sel_idx, q4, pool_blocks, kv_new, tab_t, gates4, prev4)


def _nsa_win_dec_kernel(q_ref, win_ref, new_ref, tab_ref, gate_ref, prev_ref, o_ref, *, lw):
    g = pl.program_id(1)
    inner = 2 * NSA_GROUPS
    q = q_ref[...].astype(BF16)
    k = win_ref[pl.ds(g, lw, stride=inner), :].astype(BF16)
    v = win_ref[pl.ds(NSA_GROUPS + g, lw, stride=inner), :].astype(BF16)
    new = new_ref[...]
    dist = lw - _iota((1, lw), 1)
    ok = dist < WINDOW
    tab = tab_ref[...]
    s = jnp.where(ok, _dot_nt(q, k) * SCALE + _t5_rows(tab, dist), NEG_INF)
    kn = _pick_row(new, g).astype(BF16).astype(F32)
    s_new = jnp.sum(q.astype(F32) * kn, axis=-1, keepdims=True) * SCALE + tab[:, 0:1]
    m = jnp.maximum(jnp.max(s, axis=-1, keepdims=True), s_new)
    p = jnp.where(ok, jnp.exp(s - m), 0.0)
    p_new = jnp.exp(s_new - m)
    l = jnp.sum(p, axis=-1, keepdims=True) + p_new
    o = (_dot(p.astype(BF16), v) + p_new * _pick_row(new, NSA_GROUPS + g)) / l
    o_ref[...] = prev_ref[...] + _sigmoid(gate_ref[:, 2:3]) * o


def _nsa_win_decode(q4, win_past, win_new, tab_t, gates4, prev4):
    b = q4.shape[0]
    inner = 2 * NSA_GROUPS
    lw = win_past.shape[1] // inner
    assert lw == WINDOW

    def grp(bi, g):
        return (bi, g, 0, 0)

    return pl.pallas_call(
        functools.partial(_nsa_win_dec_kernel, lw=lw),
        grid=(b, NSA_GROUPS),
        in_specs=[
            pl.BlockSpec((None, None, NSA_REP, HEAD_DIM), grp),
            pl.BlockSpec((None, lw * inner, HEAD_DIM), lambda bi, g: (bi, 0, 0)),
            pl.BlockSpec((None, inner, HEAD_DIM), lambda bi, g: (bi, 0, 0)),
            pl.BlockSpec((NSA_REP, N_BUCKETS), lambda bi, g: (g, 0)),
            pl.BlockSpec((None, None, NSA_REP, 3), grp),
            pl.BlockSpec((None, None, NSA_REP, HEAD_DIM), grp),
        ],
        out_specs=pl.BlockSpec((None, None, NSA_REP, HEAD_DIM), grp),
        out_shape=jax.ShapeDtypeStruct(q4.shape, F32),
        compiler_params=_cparams(("parallel", "parallel")),
        name="nsa_window_decode",
    )(q4, win_past, win_new, tab_t, gates4, prev4)


def _lower_bound(lbp, layer, axis):
    e = jnp.exp(lbp - jnp.max(lbp, axis=axis, keepdims=True))
    sm = e / jnp.sum(e, axis=axis, keepdims=True)
    idx = _iota(lbp.shape, axis)
    return jnp.sum(jnp.where((idx >= 1) & (idx <= layer), sm, 0.0), axis=axis, keepdims=True)


def _gla_kernel(q_ref, f_ref, i_ref, lbp_ref, s0_ref, ng_ref, o_ref, s_ref, *, layer, chunk, n_chunks):
    lb = _lower_bound(lbp_ref[...], layer, 0)
    tril = _iota((chunk, chunk), 0) >= _iota((chunk, chunk), 1)
    tril_b = tril.astype(BF16)
    ng = ng_ref[...]

    def body(c, st):
        r0 = pl.multiple_of(c * chunk, chunk)
        qv = q_ref[pl.ds(r0, chunk), :]
        q = qv * _sigmoid(qv)
        g = lb + (1.0 - lb) * _sigmoid(f_ref[pl.ds(r0, chunk), :])
        k = 1.0 - g
        v = i_ref[pl.ds(r0, chunk), :].astype(BF16)
        b = _dot3_rhs(tril_b, jnp.log(g))
        qe = (q * jnp.exp(b)).astype(BF16)
        ke = (k * jnp.exp(-b)).astype(BF16)
        att = jnp.where(tril, _dot_nt(qe, ke), 0.0)
        o = _dot(att.astype(BF16), v) + _dot_nt(qe, st.astype(BF16))
        b_last = b[chunk - 1:chunk, :]
        kt = (k * jnp.exp(b_last - b)).astype(BF16)
        st = jnp.exp(b_last) * st + _dot_tn(v, kt)
        inv = lax.rsqrt(jnp.mean(o * o, axis=-1, keepdims=True) + RMS_EPS)
        o_ref[pl.ds(r0, chunk), :] = o * inv * ng
        return st

    st = lax.fori_loop(0, n_chunks, body, s0_ref[...].T)
    s_ref[...] = st.T


def _gla(q, f, i_in, lbp, s0, norm_g, layer):
    b, t, w = q.shape
    nh = w // HEAD_DIM
    chunk = math.gcd(t, GLA_CHUNK)
    depth = lbp.shape[0]
    tok = pl.BlockSpec((None, t, HEAD_DIM), lambda bi, h: (bi, 0, h))
    st = pl.BlockSpec((None, None, HEAD_DIM, HEAD_DIM), lambda bi, h: (bi, h, 0, 0))
    return pl.pallas_call(
        functools.partial(_gla_kernel, layer=layer, chunk=chunk, n_chunks=t // chunk),
        grid=(b, nh),
        in_specs=[tok, tok, tok, pl.BlockSpec((depth, HEAD_DIM), lambda bi, h: (0, h)), st,
                  pl.BlockSpec((1, HEAD_DIM), lambda bi, h: (0, 0))],
        out_specs=[tok, st],
        out_shape=[jax.ShapeDtypeStruct((b, t, w), F32), jax.ShapeDtypeStruct(s0.shape, F32)],
        compiler_params=_cparams(("parallel", "parallel")),
        name="hgrn_gla",
    )(q, f, i_in, lbp, s0, norm_g.reshape(1, HEAD_DIM))


def _gla_dec_kernel(q_ref, f_ref, v_ref, lbp_ref, s0_ref, ng_ref, o_ref, s_ref, *, layer):
    lb = _lower_bound(lbp_ref[...], layer, 0)[0]
    qv = q_ref[...]
    q = qv * _sigmoid(qv)
    g = lb + (1.0 - lb) * _sigmoid(f_ref[...])
    s = g * s0_ref[...] + (1.0 - g) * v_ref[...]
    s_ref[...] = s
    o = jnp.sum(q * s, axis=0, keepdims=True)
    inv = lax.rsqrt(jnp.mean(o * o, axis=-1, keepdims=True) + RMS_EPS)
    o_ref[...] = o * inv * ng_ref[...]


def _gla_decode(q, f, i_in, lbp, s0, norm_g, layer):
    b, nh = s0.shape[:2]
    depth = lbp.shape[0]
    col = pl.BlockSpec((None, None, HEAD_DIM, 1), lambda bi, h: (bi, h, 0, 0))
    row = pl.BlockSpec((None, None, 1, HEAD_DIM), lambda bi, h: (bi, h, 0, 0))
    st = pl.BlockSpec((None, None, HEAD_DIM, HEAD_DIM), lambda bi, h: (bi, h, 0, 0))
    o, s = pl.pallas_call(
        functools.partial(_gla_dec_kernel, layer=layer),
        grid=(b, nh),
        in_specs=[col, col, row, pl.BlockSpec((depth, None, HEAD_DIM, 1), lambda bi, h: (0, h, 0, 0)), st,
                  pl.BlockSpec((1, HEAD_DIM), lambda bi, h: (0, 0))],
        out_specs=[row, st],
        out_shape=[jax.ShapeDtypeStruct((b, nh, 1, HEAD_DIM), F32), jax.ShapeDtypeStruct(s0.shape, F32)],
        compiler_params=_cparams(("parallel", "parallel")),
        name="hgrn_step",
    )(q.reshape(b, nh, HEAD_DIM, 1), f.reshape(b, nh, HEAD_DIM, 1), i_in.reshape(b, nh, 1, HEAD_DIM),
      lbp.reshape(depth, nh, HEAD_DIM, 1), s0, norm_g.reshape(1, HEAD_DIM))
    return o.reshape(b, 1, nh * HEAD_DIM), s


def _layer_tail(o, z, w_out, x, ple_gate_w, p, ple_proj_w):
    h = _gated_out(o, z, w_out, x)
    return _ple(h, ple_gate_w, p, ple_proj_w)


def _as_rows(w):
    if w.shape[1] % 128:
        return jnp.swapaxes(w, 0, 1), True
    return w, False


def _fox_layer(x, bt, norm_g, w_in, b_f, past):
    b, t = bt
    d = x.shape[1]
    nh = d // HEAD_DIM
    wt, tr = _as_rows(w_in)
    assert tr
    q, k, v, z = _norm_proj(x, norm_g, wt, [(n * d, d) for n in range(4)], 512, True)
    (fl,) = _norm_proj(x, norm_g, wt, [(4 * d, nh)], nh, True)
    if past is None:
        q3, k3, v3 = (a.reshape(b, t, d) for a in (q, k, v))
        logf_t, cum_t = _fox_prep(jnp.swapaxes(fl.reshape(b, t, nh), 1, 2), b_f)
        o = _fox_attn(q3, k3, v3, cum_t.reshape(b * nh, 1, t))
        logf = jnp.swapaxes(logf_t, 1, 2)
    else:
        k_pool, v_pool, lf_pool, page_table = past
        npool, page = k_pool.shape[:2]
        o, logf = _fox_decode(q.reshape(b, nh, HEAD_DIM), k.reshape(b, nh, HEAD_DIM), v.reshape(b, nh, HEAD_DIM),
                              fl.reshape(b, 1, nh), b_f, k_pool.reshape(npool, page * nh, HEAD_DIM),
                              v_pool.reshape(npool, page * nh, HEAD_DIM), jnp.swapaxes(lf_pool, 1, 2), page_table)
    state = (k.reshape(1, b, t, nh, HEAD_DIM), v.reshape(1, b, t, nh, HEAD_DIM), logf.reshape(1, b, t, nh))
    return o.reshape(b * t, d), z, state


def _pad_summaries(summaries, b, n_cmp, n_pad):
    s = summaries.reshape(2, NSA_GROUPS, b, n_cmp, HEAD_DIM)
    return jnp.pad(jnp.swapaxes(s, 1, 2), ((0, 0), (0, 0), (0, 0), (0, n_pad - n_cmp), (0, 0)))


def _nsa_layer(x, bt, norm_g, w_in, cmp_pos, cmp_w1, cmp_w2, t5_bias, past):
    b, t = bt
    d = x.shape[1]
    kvw = NSA_GROUPS * HEAD_DIM
    off_gate = d + 6 * kvw
    n_gate = 3 * (d // HEAD_DIM)
    wt, tr = _as_rows(w_in)
    assert tr
    q, kv4, win = _norm_proj(x, norm_g, wt, [(0, d), (d, 4 * kvw), (d + 4 * kvw, 2 * kvw)], 512, True)
    (gates,) = _norm_proj(x, norm_g, wt[off_gate:off_gate + n_gate], [(0, n_gate)], n_gate, True)
    (z,) = _norm_proj(x, norm_g, wt[off_gate + n_gate:], [(0, d)], 512, True)
    q3, gates3 = q.reshape(b, t, d), gates.reshape(b, t, n_gate)
    inner = NSA_KINDS * NSA_GROUPS
    if past is None:
        kv3, win3 = kv4.reshape(b, t, 4 * kvw), win.reshape(b, t, 2 * kvw)
        n_cmp = t // CMP_BLOCK
        n_pad = -(-n_cmp // 128) * 128
        summaries = _pad_summaries(_compress(kv4, cmp_pos, cmp_w1, cmp_w2), b, n_cmp, n_pad)
        o, mask = _nsa_cmp(q3, summaries[0], summaries[1], gates3, 0, t, False)
        tiles = _t5_tiles(t5_bias, 128)
        o = _nsa_attn(q3, kv3, 4, 6, tiles, gates3, o, mask, 1)
        o = _nsa_attn(q3, win3, 0, 2, tiles, gates3, o, None, 2)
        keep = min(WINDOW, t)
        win_state = win3[:, t - keep:]
    else:
        kv_pool, win_past, page_table = past
        assert t == 1
        npool, page = kv_pool.shape[:2]
        n_pages = page_table.shape[1]
        plen = n_pages * page
        pool = kv_pool.reshape(npool, page * inner, HEAD_DIM)
        n_cmp = plen // CMP_BLOCK
        n_pad = -(-(n_cmp + 1) // 128) * 128
        summaries = _pad_summaries(_compress(_cmp_gather(pool, page_table, page), cmp_pos, cmp_w1, cmp_w2),
                                   b, n_cmp, n_pad)
        rows = SUBLANES
        q_pad = jnp.pad(q3, ((0, 0), (0, rows - t), (0, 0)))
        g_pad = jnp.pad(gates3, ((0, 0), (0, rows - t), (0, 0)))
        o, sel_idx = _nsa_cmp(q_pad, summaries[0], summaries[1], g_pad, plen, plen + t, True)
        o4 = o[:, :1].reshape(b, NSA_GROUPS, NSA_REP, HEAD_DIM)
        sel_idx = sel_idx[:, :, 0, :].reshape(-1)
        q4 = q3.reshape(b, NSA_GROUPS, NSA_REP, HEAD_DIM)
        gates4 = gates3.reshape(b, NSA_GROUPS, NSA_REP, 3)
        tab_t = t5_bias.T
        blocks = pool.reshape(npool * page // CMP_BLOCK, CMP_BLOCK * inner, HEAD_DIM)
        o4 = _nsa_sel_decode(q4, blocks, kv4.reshape(b, inner, HEAD_DIM), tab_t, gates4, o4, sel_idx,
                             page_table, plen, page)
        lw = win_past.shape[1]
        wp = win_past.reshape(b, lw * 2 * NSA_GROUPS, HEAD_DIM)
        wn = win.reshape(b, 2 * NSA_GROUPS, HEAD_DIM)
        o4 = _nsa_win_decode(q4, wp, wn, tab_t, gates4, o4)
        o = o4.reshape(b, t, d)
        keep = min(WINDOW, lw + t)
        win_state = jnp.concatenate([wp, wn], axis=1)[:, (lw + t - keep) * 2 * NSA_GROUPS:]
    state = (kv4.reshape(1, b, t, NSA_KINDS, NSA_GROUPS, HEAD_DIM),
             win_state.reshape(1, b, -1, 2, NSA_GROUPS, HEAD_DIM))
    return o.reshape(b * t, d), z, state


def _hgrn_layer(x, bt, norm_g, w_in, lbp, hgrn_norm_g, layer, s0):
    b, t = bt
    d = x.shape[1]
    q, f, i_in, z = _norm_proj(x, norm_g, w_in, [(n * d, d) for n in range(4)], 512)
    if t == 1:
        o, s_new = _gla_decode(q, f, i_in, lbp, s0, hgrn_norm_g, layer)
    else:
        o, s_new = _gla(q.reshape(b, t, d), f.reshape(b, t, d), i_in.reshape(b, t, d), lbp, s0, hgrn_norm_g, layer)
    return o.reshape(b * t, d), z, (s_new[None],)


def _sb_layer(x, bt, norm_g, w_in, past):
    b, t = bt
    d = x.shape[1]
    nh = d // HEAD_DIM
    q, k, v, z = _norm_proj(x, norm_g, w_in, [(n * d, d) for n in range(4)], 512)
    if past is None:
        o = _sb_attn(q.reshape(b, t, d), k.reshape(b, t, d), v.reshape(b, t, d))
    else:
        k_pool, v_pool, page_table = past
        npool, page = k_pool.shape[:2]
        o = _sb_decode(q.reshape(b, nh, HEAD_DIM), k_pool.reshape(npool, page * nh, HEAD_DIM),
                       v_pool.reshape(npool, page * nh, HEAD_DIM), page_table, page)
    state = (k.reshape(1, b, t, nh, HEAD_DIM), v.reshape(1, b, t, nh, HEAD_DIM))
    return o.reshape(b * t, d), z, state


def kernel(x_prompt, x_sample, cache_fox_k, cache_fox_v, cache_fox_logf, cache_nsa_kv, state_nsa_win, state_hgrn, cache_sb_k, cache_sb_v, page_table, p_prompt, p_sample, norm_g, final_norm_g, ple_gate_w, ple_proj_w, fox_w_in, fox_b_f, fox_w_out, nsa_w_in, nsa_cmp_pos, nsa_cmp_w1, nsa_cmp_w2, t5_bias, nsa_w_out, hgrn_w_in, hgrn_lower_bounds, hgrn_norm_g, hgrn_w_out, sb_w_in, sb_w_out):
    depth = norm_g.shape[0]
    assert depth == 4 and all(a.shape[0] == 1 for a in (fox_w_in, nsa_w_in, hgrn_w_in, sb_w_in))

    def trunk(x3, p4, decode):
        b, t, d = x3.shape
        bt = (b, t)
        x = x3.reshape(b * t, d)
        p = p4.reshape(depth, b * t, -1)
        if decode:
            past = ((cache_fox_k[0], cache_fox_v[0], cache_fox_logf[0], page_table),
                    (cache_nsa_kv[0], state_nsa_win[0], page_table),
                    state_hgrn[0],
                    (cache_sb_k[0], cache_sb_v[0], page_table))
        else:
            past = (None, None, jnp.zeros((b,) + state_hgrn.shape[2:], F32), None)
        o, z, st_a = _fox_layer(x, bt, norm_g[0], fox_w_in[0], fox_b_f[0], past[0])
        x = _layer_tail(o, z, fox_w_out[0], x, ple_gate_w[0], p[0], ple_proj_w[0])
        o, z, st_b = _nsa_layer(x, bt, norm_g[1], nsa_w_in[0], nsa_cmp_pos[0], nsa_cmp_w1[0], nsa_cmp_w2[0],
                                t5_bias, past[1])
        x = _layer_tail(o, z, nsa_w_out[0], x, ple_gate_w[1], p[1], ple_proj_w[1])
        o, z, st_c = _hgrn_layer(x, bt, norm_g[2], hgrn_w_in[0], hgrn_lower_bounds, hgrn_norm_g[0], 2, past[2])
        x = _layer_tail(o, z, hgrn_w_out[0], x, ple_gate_w[2], p[2], ple_proj_w[2])
        o, z, st_d = _sb_layer(x, bt, norm_g[3], sb_w_in[0], past[3])
        x = _layer_tail(o, z, sb_w_out[0], x, ple_gate_w[3], p[3], ple_proj_w[3])
        y = _rmsnorm(x, final_norm_g).reshape(b, t, d)
        return y, st_a, st_b, st_c, st_d

    yp, pa, pb, pc, pd = trunk(x_prompt, p_prompt, False)
    ys, sa, sb, sc, sd = trunk(x_sample, p_sample, True)
    return (yp, ys, pa[0], sa[0], pa[1], sa[1], pa[2], sa[2], pb[0], sb[0], pb[1], sb[1],
            pc[0], sc[0], pd[0], sd[0], pd[1], sd[1])
```

```python
import functools
import math

import jax
import jax.numpy as jnp
from jax import lax
from jax.experimental import pallas as pl
from jax.experimental.pallas import tpu as pltpu

F32 = jnp.float32
BF16 = jnp.bfloat16
I32 = jnp.int32

HEAD_DIM = 128
RMS_EPS = 1e-6
NEG_INF = -1e30
FORCE_SCORE = 1e9
CMP_BLOCK = 64
N_SEL = 16
WINDOW = 512
N_BUCKETS = 32
MAX_DISTANCE = 128
GLA_CHUNK = 32
NSA_GROUPS = 2
NSA_REP = 8
NSA_KINDS = 4
SUBLANES = 8
SCALE = HEAD_DIM ** -0.5
MIB = 1 << 20


def _cparams(sem, vmem_mib=None):
    kw = dict(dimension_semantics=sem)
    if vmem_mib is not None:
        kw["vmem_limit_bytes"] = vmem_mib * MIB
    return pltpu.CompilerParams(**kw)


def _iota(shape, axis):
    return lax.broadcasted_iota(I32, shape, axis)


def _dot(a, b):
    return jnp.dot(a, b, preferred_element_type=F32)


def _dot_nt(a, b):
    return lax.dot_general(a, b, (((1,), (1,)), ((), ())), preferred_element_type=F32)


def _dot_tn(a, b):
    return lax.dot_general(a, b, (((0,), (0,)), ((), ())), preferred_element_type=F32)


def _split3(x):
    hi = x.astype(BF16)
    r1 = x - hi.astype(F32)
    mid = r1.astype(BF16)
    lo = (r1 - mid.astype(F32)).astype(BF16)
    return hi, mid, lo


def _dot3(x, m, dot=_dot):
    hi, mid, lo = _split3(x)
    return dot(hi, m) + dot(mid, m) + dot(lo, m)


def _dot3_rhs(m, x, dot=_dot):
    hi, mid, lo = _split3(x)
    return dot(m, hi) + dot(m, mid) + dot(m, lo)


def _log_sigmoid_pair(x):
    sp = jnp.log1p(jnp.exp(-jnp.abs(x)))
    return jnp.minimum(x, 0.0) - sp, jnp.minimum(-x, 0.0) - sp


def _sigmoid(x):
    return jax.nn.sigmoid(x)


def _t5_bucket(dist):
    exact = N_BUCKETS // 2
    d = jnp.maximum(dist, 0)
    logd = jnp.log(jnp.maximum(d, 1).astype(F32) * (1.0 / exact))
    large = exact + (logd * ((N_BUCKETS - exact) / math.log(MAX_DISTANCE / exact))).astype(I32)
    return jnp.where(d < exact, d, jnp.minimum(large, N_BUCKETS - 1))


def _div_pow2(x, n):
    assert n & (n - 1) == 0
    return lax.shift_right_arithmetic(x, n.bit_length() - 1)


def _row_tile(m, cap):
    t = min(m, cap)
    assert m % t == 0, (m, t)
    return t


def _norm_proj_kernel(x_ref, g_ref, w_ref, *rest, slab_blocks, transposed):
    outs, u_ref = rest[:-1], rest[-1]
    j = pl.program_id(1)

    @pl.when(j == 0)
    def _():
        x = x_ref[...]
        inv = lax.rsqrt(jnp.mean(x * x, axis=-1, keepdims=True) + RMS_EPS)
        u_ref[...] = (x * inv * g_ref[...]).astype(BF16)

    w = w_ref[...].astype(BF16)
    acc = _dot_nt(u_ref[...], w) if transposed else _dot(u_ref[...], w)
    for (lo, n), o_ref in zip(slab_blocks, outs):
        @pl.when((j >= lo) & (j < lo + n))
        def _(o_ref=o_ref):
            o_ref[...] = acc


def _norm_proj(x, g, w, slabs, tn, transposed=False):
    m, d = x.shape
    tm = _row_tile(m, 1024)
    base = slabs[0][0]
    slab_blocks, pos = [], base
    for off, width in slabs:
        assert off == pos and width % tn == 0 and off % tn == 0, (off, width, tn)
        slab_blocks.append(((off - base) // tn, width // tn))
        pos += width
    nj = (pos - base) // tn
    b0 = base // tn

    def out_map(lo, n):
        return lambda i, j: (i, jnp.minimum(jnp.maximum(j - lo, 0), n - 1))

    if transposed:
        w_spec = pl.BlockSpec((tn, d), lambda i, j: (b0 + j, 0))
    else:
        w_spec = pl.BlockSpec((d, tn), lambda i, j: (0, b0 + j))
    return pl.pallas_call(
        functools.partial(_norm_proj_kernel, slab_blocks=tuple(slab_blocks), transposed=transposed),
        grid=(m // tm, nj),
        in_specs=[pl.BlockSpec((tm, d), lambda i, j: (i, 0)), pl.BlockSpec((1, d), lambda i, j: (0, 0)), w_spec],
        out_specs=[pl.BlockSpec((tm, tn), out_map(lo, n)) for lo, n in slab_blocks],
        out_shape=[jax.ShapeDtypeStruct((m, width), F32) for _, width in slabs],
        scratch_shapes=[pltpu.VMEM((tm, d), BF16)],
        compiler_params=_cparams(("parallel", "arbitrary"), 56),
        name="norm_proj",
    )(x, g.reshape(1, d), w)


def _layer_tail_kernel(o_ref, z_ref, x_ref, p_ref, wo_ref, wg_ref, wp_ref, out_ref):
    z = z_ref[...]
    a = (o_ref[...] * (z * _sigmoid(z))).astype(BF16)
    h = x_ref[...] + _dot(a, wo_ref[...])
    gate = _sigmoid(_dot(h.astype(BF16), wg_ref[...]))
    out_ref[...] = h + gate * _dot(p_ref[...].astype(BF16), wp_ref[...])


def _layer_tail(o, z, x, p, wo, wg, wp):
    m, d = o.shape
    pd = p.shape[1]
    tm = _row_tile(m, 256)
    row = pl.BlockSpec((tm, d), lambda i: (i, 0))

    def resident(shape):
        return pl.BlockSpec(shape, lambda i: (0, 0), pipeline_mode=pl.Buffered(1))

    return pl.pallas_call(
        _layer_tail_kernel,
        grid=(m // tm,),
        in_specs=[row, row, row, pl.BlockSpec((tm, pd), lambda i: (i, 0)),
                  resident((d, d)), resident((d, d)), resident((pd, d))],
        out_specs=row,
        out_shape=jax.ShapeDtypeStruct((m, d), F32),
        compiler_params=_cparams(("parallel",), 48),
        name="layer_tail",
    )(o, z, x, p, wo, wg, wp)


def _rmsnorm_kernel(x_ref, g_ref, o_ref):
    x = x_ref[...]
    inv = lax.rsqrt(jnp.mean(x * x, axis=-1, keepdims=True) + RMS_EPS)
    o_ref[...] = x * inv * g_ref[...]


def _rmsnorm(x, g):
    m, d = x.shape
    tm = _row_tile(m, 512)
    return pl.pallas_call(
        _rmsnorm_kernel,
        grid=(m // tm,),
        in_specs=[pl.BlockSpec((tm, d), lambda i: (i, 0)), pl.BlockSpec((1, d), lambda i: (0, 0))],
        out_specs=pl.BlockSpec((tm, d), lambda i: (i, 0)),
        out_shape=jax.ShapeDtypeStruct((m, d), F32),
        compiler_params=_cparams(("parallel",)),
        name="final_rmsnorm",
    )(x, g.reshape(1, d))


def _fox_prep_kernel(x_ref, bf_ref, logf_ref, cum_ref, *, n_chunks):
    tril = (_iota((128, 128), 0) >= _iota((128, 128), 1)).astype(BF16)
    carry = jnp.zeros((1, x_ref.shape[1]), F32)
    for c in range(n_chunks):
        sl = slice(c * 128, (c + 1) * 128)
        lf, _ = _log_sigmoid_pair(x_ref[sl, :] + bf_ref[...])
        logf_ref[sl, :] = lf
        cum = _dot3_rhs(tril, lf) + carry
        cum_ref[sl, :] = cum
        carry = cum[127:128, :]


def _fox_prep(logits, b_f):
    b, t, h = logits.shape
    assert t % 128 == 0
    spec = pl.BlockSpec((None, t, h), lambda i: (i, 0, 0))
    return pl.pallas_call(
        functools.partial(_fox_prep_kernel, n_chunks=t // 128),
        grid=(b,),
        in_specs=[spec, pl.BlockSpec((1, h), lambda i: (0, 0))],
        out_specs=[spec, spec],
        out_shape=[jax.ShapeDtypeStruct((b, t, h), F32)] * 2,
        compiler_params=_cparams(("parallel",)),
        name="fox_prep",
    )(logits, b_f.reshape(1, h))


def _pick_col(x, c):
    return jnp.sum(jnp.where(_iota(x.shape, 1) == c, x, 0.0), axis=1, keepdims=True)


def _bias_lanes(col, ones_first):
    rows = col.shape[0]
    hi, mid, lo = (p.astype(F32) for p in _split3(col))
    lane = _iota((rows, HEAD_DIM), 1)
    one, val = (0, 3) if ones_first else (3, 0)
    x = jnp.where(lane == val, hi, jnp.where(lane == val + 1, mid, jnp.where(lane == val + 2, lo, 0.0)))
    x = jnp.where((lane >= one) & (lane < one + 3), 1.0, x)
    return x.astype(BF16)


def _fox_attn_kernel(q_ref, k_ref, v_ref, cq_ref, ck_ref, o_ref, ka_ref, vb_ref, *, tq):
    h = pl.program_id(1)
    qi = pl.program_id(2)
    t = k_ref.shape[0]

    @pl.when(qi == 0)
    def _():
        for c in range(t // tq):
            sl = slice(c * tq, (c + 1) * tq)
            ka_ref[sl, :HEAD_DIM] = k_ref[sl, :].astype(BF16)
            ka_ref[sl, HEAD_DIM:] = _bias_lanes(-_pick_col(ck_ref[sl, :], h), True)
            vb_ref[sl, :] = v_ref[sl, :].astype(BF16)

    qa = jnp.concatenate([(q_ref[...] * SCALE).astype(BF16), _bias_lanes(_pick_col(cq_ref[...], h), False)], axis=1)
    tri = _iota((tq, tq), 1) <= _iota((tq, tq), 0)

    def step(kc, carry, diagonal):
        m, l, acc = carry
        ks = pl.multiple_of(kc * tq, tq)
        s = _dot_nt(qa, ka_ref[pl.ds(ks, tq), :])
        if diagonal:
            s = jnp.where(tri, s, NEG_INF)
        m_new = jnp.maximum(m, jnp.max(s, axis=-1, keepdims=True))
        p = jnp.exp(s - m_new)
        alpha = jnp.exp(m - m_new)
        l = alpha * l + jnp.sum(p, axis=-1, keepdims=True)
        acc = alpha * acc + _dot(p.astype(BF16), vb_ref[pl.ds(ks, tq), :])
        return m_new, l, acc

    init = (jnp.full((tq, 1), NEG_INF, F32), jnp.zeros((tq, 1), F32), jnp.zeros((tq, HEAD_DIM), F32))
    carry = lax.fori_loop(0, qi, lambda kc, c: step(kc, c, False), init)
    _, l, acc = step(qi, carry, True)
    o_ref[...] = acc / l


def _fox_attn(q, k, v, cum, tq=512):
    b, t, w = q.shape
    h = w // HEAD_DIM
    tq = min(tq, t)
    assert t % tq == 0
    return pl.pallas_call(
        functools.partial(_fox_attn_kernel, tq=tq),
        grid=(b, h, t // tq),
        in_specs=[
            pl.BlockSpec((None, tq, HEAD_DIM), lambda bi, hi, qi: (bi, qi, hi)),
            pl.BlockSpec((None, t, HEAD_DIM), lambda bi, hi, qi: (bi, 0, hi)),
            pl.BlockSpec((None, t, HEAD_DIM), lambda bi, hi, qi: (bi, 0, hi)),
            pl.BlockSpec((None, tq, h), lambda bi, hi, qi: (bi, qi, 0)),
            pl.BlockSpec((None, t, h), lambda bi, hi, qi: (bi, 0, 0)),
        ],
        out_specs=pl.BlockSpec((None, tq, HEAD_DIM), lambda bi, hi, qi: (bi, qi, hi)),
        out_shape=jax.ShapeDtypeStruct((b, t, w), F32),
        scratch_shapes=[pltpu.VMEM((t, 2 * HEAD_DIM), BF16), pltpu.VMEM((t, HEAD_DIM), BF16)],
        compiler_params=_cparams(("arbitrary", "arbitrary", "arbitrary")),
        name="fox_attn",
    )(q, k, v, cum, cum)


def _sb_attn_kernel(q_ref, k_ref, v_ref, o_ref, kb_ref, vb_ref, *, tq):
    qi = pl.program_id(2)
    t = k_ref.shape[0]
    sub = HEAD_DIM

    @pl.when(qi == 0)
    def _():
        for c in range(t // tq):
            sl = slice(c * tq, (c + 1) * tq)
            kb_ref[sl, :] = k_ref[sl, :].astype(BF16)
            vb_ref[sl, :] = v_ref[sl, :].astype(BF16)

    q = (q_ref[...] * SCALE).astype(BF16)
    later = (_iota((sub, sub), 0) > _iota((sub, sub), 1)).astype(BF16)
    scan = jnp.concatenate([later, jnp.ones((sub, sub), BF16)], axis=1)
    scan = jnp.concatenate([scan, scan], axis=0)
    before = _iota((tq, tq), 1) < _iota((tq, tq), 0)

    def step(kc, carry, diagonal):
        tail, acc = carry
        ks = pl.multiple_of(kc * tq, tq)
        x = _dot_nt(q, kb_ref[pl.ds(ks, tq), :])
        ls_pos = jnp.minimum(x, 0.0) - jnp.log(1.0 + jnp.exp(-jnp.abs(x)))
        ls_neg = ls_pos - x
        if diagonal:
            ls_neg = jnp.where(before, ls_neg, 0.0)
        ws = [None] * (tq // sub)
        for j in reversed(range(tq // sub)):
            sl = slice(j * sub, (j + 1) * sub)
            stay = ls_neg[:, sl]
            hi = stay.astype(BF16)
            lo = (stay - hi.astype(F32)).astype(BF16)
            sums = _dot(jnp.concatenate([hi, lo], axis=1), scan)
            w = jnp.exp(ls_pos[:, sl] + sums[:, :sub] + tail)
            if diagonal:
                w = jnp.where(before[:, sl], w, 0.0)
            ws[j] = w.astype(BF16)
            tail = tail + sums[:, sub:]
        acc = acc + _dot(jnp.concatenate(ws, axis=1), vb_ref[pl.ds(ks, tq), :])
        return tail, acc

    carry = step(qi, (jnp.zeros((tq, sub), F32), jnp.zeros((tq, HEAD_DIM), F32)), True)
    _, acc = lax.fori_loop(0, qi, lambda i, c: step(qi - 1 - i, c, False), carry)
    o_ref[...] = acc


def _sb_attn(q, k, v, tq=512):
    b, t, w = q.shape
    h = w // HEAD_DIM
    tq = min(tq, t)
    assert t % tq == 0
    return pl.pallas_call(
        functools.partial(_sb_attn_kernel, tq=tq),
        grid=(b, h, t // tq),
        in_specs=[
            pl.BlockSpec((None, tq, HEAD_DIM), lambda bi, hi, qi: (bi, qi, hi)),
            pl.BlockSpec((None, t, HEAD_DIM), lambda bi, hi, qi: (bi, 0, hi)),
            pl.BlockSpec((None, t, HEAD_DIM), lambda bi, hi, qi: (bi, 0, hi)),
        ],
        out_specs=pl.BlockSpec((None, tq, HEAD_DIM), lambda bi, hi, qi: (bi, qi, hi)),
        out_shape=jax.ShapeDtypeStruct((b, t, w), F32),
        scratch_shapes=[pltpu.VMEM((t, HEAD_DIM), BF16), pltpu.VMEM((t, HEAD_DIM), BF16)],
        compiler_params=_cparams(("arbitrary", "arbitrary", "arbitrary")),
        name="sb_attn",
    )(q, k, v)


DEC_PAGES = 2


def _page_scores(k_ref, q, nh, page):
    y = (k_ref[...].reshape(page, nh, HEAD_DIM) * q[None]).astype(BF16)
    s = _dot(y.reshape(page * nh, HEAD_DIM), jnp.ones((HEAD_DIM, HEAD_DIM), BF16))
    return s.reshape(page, nh, HEAD_DIM)


def _fox_dec_kernel(pt_ref, q_ref, kn_ref, vn_ref, fl_ref, bf_ref, *refs, pp):
    k_refs, v_refs, lft_refs = refs[:pp], refs[pp:2 * pp], refs[2 * pp:3 * pp]
    o_ref, lfo_ref, m_ref, l_ref, c_ref, acc_ref = refs[3 * pp:]
    p = pl.program_id(1)
    nh, page = lft_refs[0].shape
    q = q_ref[...] * SCALE

    @pl.when(p == 0)
    def _():
        s_new = jnp.sum(q * kn_ref[...], axis=-1, keepdims=True)
        m_ref[...] = jnp.broadcast_to(s_new, m_ref.shape)
        l_ref[...] = jnp.ones_like(l_ref)
        acc_ref[...] = vn_ref[...]
        lf_new, _ = _log_sigmoid_pair(fl_ref[...] + bf_ref[...])
        lfo_ref[...] = lf_new
        c_ref[...] = lf_new

    later = (_iota((page, page), 0) > _iota((page, page), 1)).astype(BF16)
    for i in range(pp):
        lft = lft_refs[i][...]
        decay = c_ref[...] + _dot3(lft, later)
        decay = jnp.stack([jnp.broadcast_to(decay[:, r:r + 1], (nh, HEAD_DIM)) for r in range(page)], axis=0)
        s = _page_scores(k_refs[i], q, nh, page) + decay
        m_old = m_ref[...]
        m_new = jnp.maximum(m_old, jnp.max(s, axis=0))
        pr = jnp.exp(s - m_new[None])
        alpha = jnp.exp(m_old - m_new)
        l_ref[...] = alpha * l_ref[...] + jnp.sum(pr, axis=0)
        acc_ref[...] = alpha * acc_ref[...] + jnp.sum(pr * v_refs[i][...].reshape(page, nh, HEAD_DIM), axis=0)
        m_ref[...] = m_new
        c_ref[...] = c_ref[...] + jnp.sum(lft, axis=1, keepdims=True)

    @pl.when(p == pl.num_programs(1) - 1)
    def _():
        o_ref[...] = acc_ref[...] / l_ref[...]


def _fox_decode(q, k_new, v_new, fl, b_f, k_pool, v_pool, lf_pool_t, page_table):
    b, nh, _ = q.shape
    n_pages = page_table.shape[1]
    page = lf_pool_t.shape[2]
    pp = math.gcd(DEC_PAGES, n_pages)

    def row(bi, p, pt):
        return (bi, 0, 0)

    def pg(i):
        return lambda bi, p, pt: (pt[bi, n_pages - 1 - (p * pp + i)], 0, 0)

    head = pl.BlockSpec((None, nh, HEAD_DIM), row)
    stat = pltpu.VMEM((nh, HEAD_DIM), F32)
    grid_spec = pltpu.PrefetchScalarGridSpec(
        num_scalar_prefetch=1,
        grid=(b, n_pages // pp),
        in_specs=[head, head, head, pl.BlockSpec((None, nh, 1), row), pl.BlockSpec((nh, 1), lambda bi, p, pt: (0, 0))]
        + [pl.BlockSpec((None, page * nh, HEAD_DIM), pg(i)) for i in range(pp)] * 2
        + [pl.BlockSpec((None, nh, page), pg(i)) for i in range(pp)],
        out_specs=[head, pl.BlockSpec((None, nh, 1), row)],
        scratch_shapes=[stat, stat, pltpu.VMEM((nh, 1), F32), stat],
    )
    return pl.pallas_call(
        functools.partial(_fox_dec_kernel, pp=pp),
        grid_spec=grid_spec,
        out_shape=[jax.ShapeDtypeStruct((b, nh, HEAD_DIM), F32), jax.ShapeDtypeStruct((b, nh, 1), F32)],
        compiler_params=_cparams(("arbitrary", "arbitrary")),
        name="fox_decode",
    )(page_table, q, k_new, v_new, fl, b_f.reshape(nh, 1), *([k_pool] * pp), *([v_pool] * pp), *([lf_pool_t] * pp))


def _sb_dec_kernel(pt_ref, q_ref, *refs, pp, page):
    k_refs, v_refs = refs[:pp], refs[pp:2 * pp]
    o_ref, c_ref, acc_ref = refs[2 * pp:]
    p = pl.program_id(1)
    nh = q_ref.shape[0]
    q = q_ref[...] * SCALE

    @pl.when(p == 0)
    def _():
        c_ref[...] = jnp.zeros_like(c_ref)
        acc_ref[...] = jnp.zeros_like(acc_ref)

    for i in range(pp):
        x = _page_scores(k_refs[i], q, nh, page)
        ls_pos = jnp.minimum(x, 0.0) - jnp.log(1.0 + jnp.exp(-jnp.abs(x)))
        ls_neg = ls_pos - x
        tail = c_ref[...]
        ws = [None] * page
        for r in reversed(range(page)):
            ws[r] = jnp.exp(ls_pos[r] + tail)
            tail = tail + ls_neg[r]
        acc_ref[...] = acc_ref[...] + jnp.sum(jnp.stack(ws, axis=0) * v_refs[i][...].reshape(page, nh, HEAD_DIM), axis=0)
        c_ref[...] = tail

    @pl.when(p == pl.num_programs(1) - 1)
    def _():
        o_ref[...] = acc_ref[...]


def _sb_decode(q, k_pool, v_pool, page_table, page):
    b, nh, _ = q.shape
    n_pages = page_table.shape[1]
    pp = math.gcd(DEC_PAGES, n_pages)

    def row(bi, p, pt):
        return (bi, 0, 0)

    def pg(i):
        return lambda bi, p, pt: (pt[bi, n_pages - 1 - (p * pp + i)], 0, 0)

    head = pl.BlockSpec((None, nh, HEAD_DIM), row)
    stat = pltpu.VMEM((nh, HEAD_DIM), F32)
    grid_spec = pltpu.PrefetchScalarGridSpec(
        num_scalar_prefetch=1,
        grid=(b, n_pages // pp),
        in_specs=[head] + [pl.BlockSpec((None, page * nh, HEAD_DIM), pg(i)) for i in range(pp)] * 2,
        out_specs=head,
        scratch_shapes=[stat, stat],
    )
    return pl.pallas_call(
        functools.partial(_sb_dec_kernel, pp=pp, page=page),
        grid_spec=grid_spec,
        out_shape=jax.ShapeDtypeStruct((b, nh, HEAD_DIM), F32),
        compiler_params=_cparams(("arbitrary", "arbitrary")),
        name="sb_decode",
    )(page_table, q, *([k_pool] * pp), *([v_pool] * pp))


def _cmp_gather_kernel(pt_ref, *refs, inner, page):
    x_refs, o_ref = refs[:-1], refs[-1]
    for i, x_ref in enumerate(x_refs):
        for c in range(2 * NSA_GROUPS):
            o_ref[c, i * page:(i + 1) * page, :] = x_ref[pl.ds(c, page, stride=inner), :]


def _cmp_gather(pool, page_table, page, pp=8):
    b, n_pages = page_table.shape
    inner = pool.shape[1] // page
    pp = math.gcd(pp, n_pages)
    steps = n_pages // pp

    def page_spec(i):
        return pl.BlockSpec((None, page * inner, HEAD_DIM), lambda bi, p, pt: (pt[bi, p * pp + i], 0, 0))

    grid_spec = pltpu.PrefetchScalarGridSpec(
        num_scalar_prefetch=1,
        grid=(b, steps),
        in_specs=[page_spec(i) for i in range(pp)],
        out_specs=pl.BlockSpec((2 * NSA_GROUPS, pp * page, HEAD_DIM), lambda bi, p, pt: (0, bi * steps + p, 0)),
    )
    return pl.pallas_call(
        functools.partial(_cmp_gather_kernel, inner=inner, page=page),
        grid_spec=grid_spec,
        out_shape=jax.ShapeDtypeStruct((2 * NSA_GROUPS, b * n_pages * page, HEAD_DIM), F32),
        compiler_params=_cparams(("parallel", "arbitrary")),
        name="nsa_cmp_gather",
    )(page_table, *([pool] * pp))


def _compress_kernel(x_ref, pos_ref, w1_ref, w2_ref, o_ref, *, rb):
    hid = jnp.zeros((rb, w1_ref.shape[2]), F32)
    for c in range(CMP_BLOCK):
        xc = x_ref[pl.ds(c, rb, stride=CMP_BLOCK), :] + pos_ref[c:c + 1, :]
        hid = hid + _dot(xc.astype(BF16), w1_ref[c].astype(BF16))
    hid = hid * _sigmoid(hid)
    o_ref[...] = _dot(hid.astype(BF16), w2_ref[...].astype(BF16))


def _compress(x, pos, w1, w2):
    r = x.shape[-2]
    nb = r // CMP_BLOCK
    rb = _row_tile(nb, 256)
    hidden = w1.shape[-1]
    w1v = w1.reshape(2, CMP_BLOCK, HEAD_DIM, hidden)
    if x.ndim == 2:
        x_spec = pl.BlockSpec((rb * CMP_BLOCK, HEAD_DIM), lambda c, i: (i, c))
    else:
        x_spec = pl.BlockSpec((None, rb * CMP_BLOCK, HEAD_DIM), lambda c, i: (c, i, 0))
    return pl.pallas_call(
        functools.partial(_compress_kernel, rb=rb),
        grid=(2 * NSA_GROUPS, nb // rb),
        in_specs=[
            x_spec,
            pl.BlockSpec((None, CMP_BLOCK, HEAD_DIM), lambda c, i: (c // NSA_GROUPS, 0, 0)),
            pl.BlockSpec((None, CMP_BLOCK, HEAD_DIM, hidden), lambda c, i: (c // NSA_GROUPS, 0, 0, 0)),
            pl.BlockSpec((None, hidden, HEAD_DIM), lambda c, i: (c // NSA_GROUPS, 0, 0)),
        ],
        out_specs=pl.BlockSpec((None, rb, HEAD_DIM), lambda c, i: (c, i, 0)),
        out_shape=jax.ShapeDtypeStruct((2 * NSA_GROUPS, nb, HEAD_DIM), F32),
        compiler_params=_cparams(("parallel", "arbitrary"), 48),
        name="nsa_compress",
    )(x, pos, w1v, w2)


def _gate_col(gl, col):
    lane = _iota(gl.shape, 1)
    return _sigmoid(jnp.sum(jnp.where(lane == col, gl, 0.0), axis=-1, keepdims=True))


def _nsa_cmp_kernel(q_ref, kc_ref, vc_ref, gate_ref, o_ref, sel_ref, *, past, tq, n_cmp, n_blk, t_keys, decode):
    g = pl.program_id(1)
    qi = pl.program_id(2)
    n_pad = kc_ref.shape[0]
    kc = kc_ref[...].astype(BF16)
    vc = vc_ref[...].astype(BF16)
    qpos = past + qi * tq + _iota((tq, 1), 0)
    blk = _iota((tq, n_pad), 1)
    visible = ((blk + 1) * CMP_BLOCK - 1 <= qpos) & (blk < n_cmp)
    gl = gate_ref[...]
    q_all = jnp.concatenate([q_ref[:, r * HEAD_DIM:(r + 1) * HEAD_DIM].astype(BF16) for r in range(NSA_REP)], axis=0)
    s = (_dot_nt(q_all, kc) * SCALE).reshape(NSA_REP, tq, n_pad)
    s = jnp.where(visible[None], s, NEG_INF)
    e = jnp.exp(s - jnp.max(s, axis=-1, keepdims=True))
    pr = jnp.where(visible[None], e / jnp.sum(e, axis=-1, keepdims=True), 0.0)
    imp = jnp.sum(pr, axis=0)
    gate = jnp.concatenate([_gate_col(gl, (g * NSA_REP + r) * 3) for r in range(NSA_REP)], axis=0)
    o = gate * _dot(pr.reshape(NSA_REP * tq, n_pad).astype(BF16), vc)
    o_ref[...] = jnp.concatenate([o[r * tq:(r + 1) * tq] for r in range(NSA_REP)], axis=1)

    cur = _div_pow2(qpos, CMP_BLOCK)
    forced = (blk == 0) | (blk == cur) | (blk == cur - 1)
    score = jnp.where(blk > cur, NEG_INF, jnp.where(forced, FORCE_SCORE, imp))
    if decode:
        row = score[0:1, :]
        col = jnp.sum(jnp.where(_iota((n_pad, n_pad), 0) == _iota((n_pad, n_pad), 1),
                                jnp.broadcast_to(row, (n_pad, n_pad)), 0.0), axis=1, keepdims=True)
        i_blk, j_blk = _iota((n_pad, n_pad), 0), _iota((n_pad, n_pad), 1)
        first = ((col > row) | ((col == row) & (i_blk < j_blk))) & (i_blk < n_blk)
        rank = jnp.sum(jnp.where(first, 1.0, 0.0), axis=0, keepdims=True)
        blk_f = blk[0:1, :].astype(F32)
        lane = _iota((1, N_SEL), 1)
        idx = jnp.zeros((1, N_SEL), F32)
        for k in range(N_SEL):
            ik = jnp.sum(jnp.where(rank == k, blk_f, 0.0), axis=1, keepdims=True)
            idx = jnp.where(lane == k, ik, idx)
        sel_ref[...] = jnp.broadcast_to(idx, sel_ref.shape).astype(I32)
    else:
        rank = jnp.zeros((tq, n_pad), F32)
        for i in range(n_blk):
            col = jnp.sum(jnp.where(blk == i, score, 0.0), axis=1, keepdims=True)
            first = (col > score) | ((col == score) & (i < blk))
            rank = rank + jnp.where(first, 1.0, 0.0)
        sel = jnp.where(rank < N_SEL, 1.0, 0.0).astype(BF16)
        expand = (_div_pow2(_iota((n_pad, t_keys), 1), CMP_BLOCK) == _iota((n_pad, t_keys), 0)).astype(BF16)
        sel_ref[...] = _dot(sel, expand).astype(BF16)


def _nsa_cmp(q, kc, vc, gates, past, n_keys, decode):
    b, t, w = q.shape
    n_pad = kc.shape[2]
    n_cmp = n_keys // CMP_BLOCK
    n_blk = -(-n_keys // CMP_BLOCK)
    assert N_SEL <= n_blk <= n_pad
    tq = min(t, 128)
    gw = w // NSA_GROUPS
    if decode:
        sel_shape, sel_block, sel_dtype = (b, NSA_GROUPS, t, N_SEL), (None, None, tq, N_SEL), I32
    else:
        sel_shape, sel_block, sel_dtype = (b, NSA_GROUPS, t, t), (None, None, tq, t), BF16
    return pl.pallas_call(
        functools.partial(_nsa_cmp_kernel, past=past, tq=tq, n_cmp=n_cmp, n_blk=n_blk, t_keys=t, decode=decode),
        grid=(b, NSA_GROUPS, t // tq),
        in_specs=[
            pl.BlockSpec((None, tq, gw), lambda bi, g, qi: (bi, qi, g)),
            pl.BlockSpec((None, None, n_pad, HEAD_DIM), lambda bi, g, qi: (bi, g, 0, 0)),
            pl.BlockSpec((None, None, n_pad, HEAD_DIM), lambda bi, g, qi: (bi, g, 0, 0)),
            pl.BlockSpec((None, tq, gates.shape[-1]), lambda bi, g, qi: (bi, qi, 0)),
        ],
        out_specs=[
            pl.BlockSpec((None, tq, gw), lambda bi, g, qi: (bi, qi, g)),
            pl.BlockSpec(sel_block, lambda bi, g, qi: (bi, g, qi, 0)),
        ],
        out_shape=[jax.ShapeDtypeStruct((b, t, w), F32), jax.ShapeDtypeStruct(sel_shape, sel_dtype)],
        compiler_params=_cparams(("parallel", "parallel", "arbitrary")),
        name="nsa_cmp_select",
    )(q, kc, vc, gates)


def _t5_tiles_kernel(tab_ref, o_ref, *, tq):
    h = pl.program_id(0)
    d0 = _iota((tq, tq), 0) - _iota((tq, tq), 1)
    for t in range(3):
        bucket = _t5_bucket(d0 + t * tq)
        acc = jnp.zeros((tq, tq), F32)
        for n in range(N_BUCKETS):
            acc = jnp.where(bucket == n, tab_ref[n, h], acc)
        o_ref[t] = acc


def _t5_tiles(t5_bias, tq):
    nh = t5_bias.shape[1]
    assert tq >= MAX_DISTANCE
    return pl.pallas_call(
        functools.partial(_t5_tiles_kernel, tq=tq),
        grid=(nh,),
        in_specs=[pl.BlockSpec(memory_space=pltpu.SMEM)],
        out_specs=pl.BlockSpec((None, 3, tq, tq), lambda h: (h, 0, 0, 0)),
        out_shape=jax.ShapeDtypeStruct((nh, 3, tq, tq), F32),
        compiler_params=_cparams(("arbitrary",)),
        name="t5_tiles",
    )(t5_bias)


def _nsa_attn_kernel(*refs, tq, windowed, branch):
    if windowed:
        q_ref, k_ref, v_ref, tiles_ref, gate_ref, prev_ref, o_ref, kb_ref, vb_ref = refs
        mask_ref = None
    else:
        q_ref, k_ref, v_ref, tiles_ref, gate_ref, prev_ref, mask_ref, o_ref, kb_ref, vb_ref = refs
    g = pl.program_id(1)
    qi = pl.program_id(2)
    t = k_ref.shape[0]

    @pl.when(qi == 0)
    def _():
        for c in range(t // tq):
            sl = slice(c * tq, (c + 1) * tq)
            kb_ref[sl, :] = k_ref[sl, :].astype(BF16)
            vb_ref[sl, :] = v_ref[sl, :].astype(BF16)

    rows = NSA_REP * tq
    q_all = jnp.concatenate([(q_ref[:, r * HEAD_DIM:(r + 1) * HEAD_DIM] * SCALE).astype(BF16)
                             for r in range(NSA_REP)], axis=0)
    dist0 = _iota((tq, tq), 0) - _iota((tq, tq), 1)

    def body(kc, carry):
        m, l, acc = carry
        ks = pl.multiple_of(kc * tq, tq)
        dist = dist0 + (qi - kc) * tq
        ok = dist >= 0
        if windowed:
            ok = ok & (dist < WINDOW)
        else:
            ok = ok & (mask_ref[:, pl.ds(ks, tq)].astype(F32) > 0.5)
        s = _dot_nt(q_all, kb_ref[pl.ds(ks, tq), :]).reshape(NSA_REP, tq, tq)
        s = jnp.where(ok[None], s + tiles_ref[:, jnp.minimum(qi - kc, 2)], NEG_INF).reshape(rows, tq)
        m_new = jnp.maximum(m, jnp.max(s, axis=-1, keepdims=True))
        p = jnp.exp(s - m_new)
        if windowed:
            p = jnp.where(s > 0.5 * NEG_INF, p, 0.0)
        alpha = jnp.exp(m - m_new)
        l = alpha * l + jnp.sum(p, axis=-1, keepdims=True)
        acc = alpha * acc + _dot(p.astype(BF16), vb_ref[pl.ds(ks, tq), :])
        return m_new, l, acc

    lo = jnp.maximum(qi - WINDOW // tq, 0) if windowed else 0
    init = (jnp.full((rows, 1), NEG_INF, F32), jnp.zeros((rows, 1), F32), jnp.zeros((rows, HEAD_DIM), F32))
    _, l, acc = lax.fori_loop(lo, qi + 1, body, init)
    gl = gate_ref[...]
    gate = jnp.concatenate([_gate_col(gl, (g * NSA_REP + r) * 3 + branch) for r in range(NSA_REP)], axis=0)
    o = gate * (acc / l)
    o_ref[...] = prev_ref[...] + jnp.concatenate([o[r * tq:(r + 1) * tq] for r in range(NSA_REP)], axis=1)


def _nsa_attn(q, kv, k_blk, v_blk, tiles, gates, prev, mask, branch):
    b, t, w = q.shape
    gw = w // NSA_GROUPS
    tq = tiles.shape[-1]
    assert t % tq == 0 and WINDOW % tq == 0
    windowed = mask is None
    in_specs = [
        pl.BlockSpec((None, tq, gw), lambda bi, g, qi: (bi, qi, g)),
        pl.BlockSpec((None, t, HEAD_DIM), lambda bi, g, qi: (bi, 0, k_blk + g)),
        pl.BlockSpec((None, t, HEAD_DIM), lambda bi, g, qi: (bi, 0, v_blk + g)),
        pl.BlockSpec((NSA_REP, 3, tq, tq), lambda bi, g, qi: (g, 0, 0, 0)),
        pl.BlockSpec((None, tq, gates.shape[-1]), lambda bi, g, qi: (bi, qi, 0)),
        pl.BlockSpec((None, tq, gw), lambda bi, g, qi: (bi, qi, g)),
    ]
    args = [q, kv, kv, tiles, gates, prev]
    if not windowed:
        in_specs.append(pl.BlockSpec((None, None, tq, t), lambda bi, g, qi: (bi, g, qi, 0)))
        args.append(mask)
    return pl.pallas_call(
        functools.partial(_nsa_attn_kernel, tq=tq, windowed=windowed, branch=branch),
        grid=(b, NSA_GROUPS, t // tq),
        in_specs=in_specs,
        out_specs=pl.BlockSpec((None, tq, gw), lambda bi, g, qi: (bi, qi, g)),
        out_shape=jax.ShapeDtypeStruct((b, t, w), F32),
        scratch_shapes=[pltpu.VMEM((t, HEAD_DIM), BF16), pltpu.VMEM((t, HEAD_DIM), BF16)],
        compiler_params=_cparams(("arbitrary", "arbitrary", "arbitrary")),
        name="nsa_window" if windowed else "nsa_select",
    )(*args)


def _t5_rows(tab_t, dist):
    n = dist.shape[1]
    onehot = (_iota((N_BUCKETS, n), 0) == _t5_bucket(dist)).astype(BF16)
    return _dot3(tab_t, onehot)


def _pick_row(x, r):
    return jnp.sum(jnp.where(_iota(x.shape, 0) == r, x, 0.0), axis=0, keepdims=True)


def _nsa_sel_dec_kernel(pt_ref, idx_ref, q_ref, kv_ref, new_ref, tab_ref, gate_ref, prev_ref,
                        o_ref, m_ref, l_ref, acc_ref, *, past, n_past_blk, inner):
    bi, g, j = pl.program_id(0), pl.program_id(1), pl.program_id(2)
    blk = idx_ref[(bi * NSA_GROUPS + g) * N_SEL + j]

    @pl.when(j == 0)
    def _():
        m_ref[...] = jnp.full_like(m_ref, NEG_INF)
        l_ref[...] = jnp.zeros_like(l_ref)
        acc_ref[...] = jnp.zeros_like(acc_ref)

    k_row = 2 * NSA_GROUPS + g
    v_row = 3 * NSA_GROUPS + g
    is_new = blk >= n_past_blk
    new = new_ref[...]
    k_old = kv_ref[pl.ds(k_row, CMP_BLOCK, stride=inner), :]
    v_old = kv_ref[pl.ds(v_row, CMP_BLOCK, stride=inner), :]
    k = jnp.where(is_new, jnp.broadcast_to(_pick_row(new, k_row), k_old.shape), k_old).astype(BF16)
    v = jnp.where(is_new, jnp.broadcast_to(_pick_row(new, v_row), v_old.shape), v_old).astype(BF16)
    dist = past - (blk * CMP_BLOCK + _iota((1, CMP_BLOCK), 1))
    ok = dist >= 0
    s = _dot_nt(q_ref[...].astype(BF16), k) * SCALE + _t5_rows(tab_ref[...], dist)
    s = jnp.where(ok, s, NEG_INF)
    m_old = m_ref[...]
    m_new = jnp.maximum(m_old, jnp.max(s, axis=-1, keepdims=True))
    p = jnp.where(ok, jnp.exp(s - m_new), 0.0)
    alpha = jnp.exp(m_old - m_new)
    l_ref[...] = alpha * l_ref[...] + jnp.sum(p, axis=-1, keepdims=True)
    acc_ref[...] = alpha * acc_ref[...] + _dot(p.astype(BF16), v)
    m_ref[...] = m_new

    @pl.when(j == N_SEL - 1)
    def _():
        o_ref[...] = prev_ref[...] + _sigmoid(gate_ref[:, 1:2]) * (acc_ref[...] / l_ref[...])


def _nsa_sel_decode(q4, pool_blocks, kv_new, tab_t, gates4, prev4, sel_idx, page_table, past, page):
    b = q4.shape[0]
    assert past % page == 0 and page % CMP_BLOCK == 0
    n_past_blk = past // CMP_BLOCK
    per_page = page // CMP_BLOCK
    inner = pool_blocks.shape[1] // CMP_BLOCK

    def blk_row(bi, g, j, pt, idx):
        n = jnp.minimum(idx[(bi * NSA_GROUPS + g) * N_SEL + j], n_past_blk - 1)
        return pt[bi, n // per_page] * per_page + n % per_page

    def grp(bi, g, j, pt, idx):
        return (bi, g, 0, 0)

    grid_spec = pltpu.PrefetchScalarGridSpec(
        num_scalar_prefetch=2,
        grid=(b, NSA_GROUPS, N_SEL),
        in_specs=[
            pl.BlockSpec((None, None, NSA_REP, HEAD_DIM), grp),
            pl.BlockSpec((None, CMP_BLOCK * inner, HEAD_DIM), lambda bi, g, j, pt, idx: (blk_row(bi, g, j, pt, idx), 0, 0)),
            pl.BlockSpec((None, inner, HEAD_DIM), lambda bi, g, j, pt, idx: (bi, 0, 0)),
            pl.BlockSpec((NSA_REP, N_BUCKETS), lambda bi, g, j, pt, idx: (g, 0)),
            pl.BlockSpec((None, None, NSA_REP, 3), grp),
            pl.BlockSpec((None, None, NSA_REP, HEAD_DIM), grp),
        ],
        out_specs=pl.BlockSpec((None, None, NSA_REP, HEAD_DIM), grp),
        scratch_shapes=[pltpu.VMEM((NSA_REP, 1), F32), pltpu.VMEM((NSA_REP, 1), F32),
                        pltpu.VMEM((NSA_REP, HEAD_DIM), F32)],
    )
    return pl.pallas_call(
        functools.partial(_nsa_sel_dec_kernel, past=past, n_past_blk=n_past_blk, inner=inner),
        grid_spec=grid_spec,
        out_shape=jax.ShapeDtypeStruct(q4.shape, F32),
        compiler_params=_cparams(("parallel", "parallel", "arbitrary")),
        name="nsa_select_decode",
    )(page_table, sel_idx, q4, pool_blocks, kv_new, tab_t, gates4, prev4)


def _nsa_win_dec_kernel(q_ref, win_ref, new_ref, tab_ref, gate_ref, prev_ref, o_ref, *, lw):
    g = pl.program_id(1)
    inner = 2 * NSA_GROUPS
    q = q_ref[...].astype(BF16)
    k = win_ref[pl.ds(g, lw, stride=inner), :].astype(BF16)
    v = win_ref[pl.ds(NSA_GROUPS + g, lw, stride=inner), :].astype(BF16)
    new = new_ref[...]
    dist = lw - _iota((1, lw), 1)
    ok = dist < WINDOW
    tab = tab_ref[...]
    s = jnp.where(ok, _dot_nt(q, k) * SCALE + _t5_rows(tab, dist), NEG_INF)
    kn = _pick_row(new, g).astype(BF16).astype(F32)
    s_new = jnp.sum(q.astype(F32) * kn, axis=-1, keepdims=True) * SCALE + tab[:, 0:1]
    m = jnp.maximum(jnp.max(s, axis=-1, keepdims=True), s_new)
    p = jnp.where(ok, jnp.exp(s - m), 0.0)
    p_new = jnp.exp(s_new - m)
    l = jnp.sum(p, axis=-1, keepdims=True) + p_new
    o = (_dot(p.astype(BF16), v) + p_new * _pick_row(new, NSA_GROUPS + g)) / l
    o_ref[...] = prev_ref[...] + _sigmoid(gate_ref[:, 2:3]) * o


def _nsa_win_decode(q4, win_past, win_new, tab_t, gates4, prev4):
    b = q4.shape[0]
    inner = 2 * NSA_GROUPS
    lw = win_past.shape[1] // inner
    assert lw == WINDOW

    def grp(bi, g):
        return (bi, g, 0, 0)

    return pl.pallas_call(
        functools.partial(_nsa_win_dec_kernel, lw=lw),
        grid=(b, NSA_GROUPS),
        in_specs=[
            pl.BlockSpec((None, None, NSA_REP, HEAD_DIM), grp),
            pl.BlockSpec((None, lw * inner, HEAD_DIM), lambda bi, g: (bi, 0, 0)),
            pl.BlockSpec((None, inner, HEAD_DIM), lambda bi, g: (bi, 0, 0)),
            pl.BlockSpec((NSA_REP, N_BUCKETS), lambda bi, g: (g, 0)),
            pl.BlockSpec((None, None, NSA_REP, 3), grp),
            pl.BlockSpec((None, None, NSA_REP, HEAD_DIM), grp),
        ],
        out_specs=pl.BlockSpec((None, None, NSA_REP, HEAD_DIM), grp),
        out_shape=jax.ShapeDtypeStruct(q4.shape, F32),
        compiler_params=_cparams(("parallel", "parallel")),
        name="nsa_window_decode",
    )(q4, win_past, win_new, tab_t, gates4, prev4)


def _lower_bound(lbp, layer, axis):
    e = jnp.exp(lbp - jnp.max(lbp, axis=axis, keepdims=True))
    sm = e / jnp.sum(e, axis=axis, keepdims=True)
    idx = _iota(lbp.shape, axis)
    return jnp.sum(jnp.where((idx >= 1) & (idx <= layer), sm, 0.0), axis=axis, keepdims=True)


def _gla_kernel(q_ref, f_ref, i_ref, lbp_ref, s0_ref, ng_ref, o_ref, s_ref, st_ref, *, layer, chunk, n_chunks, hp):
    ti = pl.program_id(2)
    lb = _lower_bound(lbp_ref[...], layer, 0)
    tril = _iota((chunk, chunk), 0) >= _iota((chunk, chunk), 1)
    tril_b = tril.astype(BF16)
    ng = ng_ref[...]

    @pl.when(ti == 0)
    def _():
        for r in range(hp):
            st_ref[r] = s0_ref[r].T

    def heads(x):
        return jnp.stack([x[:, r * HEAD_DIM:(r + 1) * HEAD_DIM] for r in range(hp)], axis=0)

    def bdot(a, b, ca, cb):
        return lax.dot_general(a, b, (((ca,), (cb,)), ((0,), (0,))), preferred_element_type=F32)

    def body(c, _):
        r0 = pl.multiple_of(c * chunk, chunk)
        qv = q_ref[pl.ds(r0, chunk), :]
        q = qv * _sigmoid(qv)
        g = lb + (1.0 - lb) * _sigmoid(f_ref[pl.ds(r0, chunk), :])
        k = 1.0 - g
        b = _dot3_rhs(tril_b, jnp.log(g))
        b_last = b[chunk - 1:chunk, :]
        qe = heads((q * jnp.exp(b)).astype(BF16))
        ke = heads((k * jnp.exp(-b)).astype(BF16))
        kt = heads((k * jnp.exp(b_last - b)).astype(BF16))
        v = heads(i_ref[pl.ds(r0, chunk), :].astype(BF16))
        st = st_ref[...]
        att = jnp.where(tril[None], bdot(qe, ke, 2, 2), 0.0)
        o = bdot(att.astype(BF16), v, 2, 1) + bdot(qe, st.astype(BF16), 2, 2)
        st_ref[...] = heads(jnp.exp(b_last)) * st + bdot(v, kt, 1, 1)
        inv = lax.rsqrt(jnp.mean(o * o, axis=-1, keepdims=True) + RMS_EPS)
        o = o * inv * ng
        o_ref[pl.ds(r0, chunk), :] = jnp.concatenate([o[r] for r in range(hp)], axis=1)
        return 0

    lax.fori_loop(0, n_chunks, body, 0)

    @pl.when(ti == pl.num_programs(2) - 1)
    def _():
        for r in range(hp):
            s_ref[r] = st_ref[r].T


def _gla(q, f, i_in, lbp, s0, norm_g, layer, hp=8, tt=512):
    b, t, w = q.shape
    nh = w // HEAD_DIM
    chunk = math.gcd(t, GLA_CHUNK)
    depth = lbp.shape[0]
    tt = min(tt, t)
    assert nh % hp == 0 and t % tt == 0 and tt % chunk == 0
    tok = pl.BlockSpec((None, tt, hp * HEAD_DIM), lambda bi, h, ti: (bi, ti, h))
    st = pl.BlockSpec((None, hp, HEAD_DIM, HEAD_DIM), lambda bi, h, ti: (bi, h, 0, 0))
    return pl.pallas_call(
        functools.partial(_gla_kernel, layer=layer, chunk=chunk, n_chunks=tt // chunk, hp=hp),
        grid=(b, nh // hp, t // tt),
        in_specs=[tok, tok, tok, pl.BlockSpec((depth, hp * HEAD_DIM), lambda bi, h, ti: (0, h)), st,
                  pl.BlockSpec((1, HEAD_DIM), lambda bi, h, ti: (0, 0))],
        out_specs=[tok, st],
        out_shape=[jax.ShapeDtypeStruct((b, t, w), F32), jax.ShapeDtypeStruct(s0.shape, F32)],
        scratch_shapes=[pltpu.VMEM((hp, HEAD_DIM, HEAD_DIM), F32)],
        compiler_params=_cparams(("arbitrary", "arbitrary", "arbitrary")),
        name="hgrn_gla",
    )(q, f, i_in, lbp, s0, norm_g.reshape(1, HEAD_DIM))


def _gla_dec_kernel(q_ref, f_ref, v_ref, lbp_ref, s0_ref, ng_ref, o_ref, s_ref, *, layer):
    lb = _lower_bound(lbp_ref[...], layer, 0)[0]
    qv = q_ref[...]
    q = qv * _sigmoid(qv)
    g = lb + (1.0 - lb) * _sigmoid(f_ref[...])
    s = g * s0_ref[...] + (1.0 - g) * v_ref[...]
    s_ref[...] = s
    o = jnp.sum(q * s, axis=0, keepdims=True)
    inv = lax.rsqrt(jnp.mean(o * o, axis=-1, keepdims=True) + RMS_EPS)
    o_ref[...] = o * inv * ng_ref[...]


def _gla_decode(q, f, i_in, lbp, s0, norm_g, layer):
    b, nh = s0.shape[:2]
    depth = lbp.shape[0]
    col = pl.BlockSpec((None, None, HEAD_DIM, 1), lambda bi, h: (bi, h, 0, 0))
    row = pl.BlockSpec((None, None, 1, HEAD_DIM), lambda bi, h: (bi, h, 0, 0))
    st = pl.BlockSpec((None, None, HEAD_DIM, HEAD_DIM), lambda bi, h: (bi, h, 0, 0))
    o, s = pl.pallas_call(
        functools.partial(_gla_dec_kernel, layer=layer),
        grid=(b, nh),
        in_specs=[col, col, row, pl.BlockSpec((depth, None, HEAD_DIM, 1), lambda bi, h: (0, h, 0, 0)), st,
                  pl.BlockSpec((1, HEAD_DIM), lambda bi, h: (0, 0))],
        out_specs=[row, st],
        out_shape=[jax.ShapeDtypeStruct((b, nh, 1, HEAD_DIM), F32), jax.ShapeDtypeStruct(s0.shape, F32)],
        compiler_params=_cparams(("parallel", "parallel")),
        name="hgrn_step",
    )(q.reshape(b, nh, HEAD_DIM, 1), f.reshape(b, nh, HEAD_DIM, 1), i_in.reshape(b, nh, 1, HEAD_DIM),
      lbp.reshape(depth, nh, HEAD_DIM, 1), s0, norm_g.reshape(1, HEAD_DIM))
    return o.reshape(b, 1, nh * HEAD_DIM), s


def _as_rows(w):
    if w.shape[1] % 128:
        return jnp.swapaxes(w, 0, 1), True
    return w, False


def _fox_layer(x, bt, norm_g, w_in, b_f, past):
    b, t = bt
    d = x.shape[1]
    nh = d // HEAD_DIM
    wt, tr = _as_rows(w_in)
    assert tr
    q, k, v, z = _norm_proj(x, norm_g, wt, [(n * d, d) for n in range(4)], 512, True)
    (fl,) = _norm_proj(x, norm_g, wt, [(4 * d, nh)], nh, True)
    if past is None:
        q3, k3, v3 = (a.reshape(b, t, d) for a in (q, k, v))
        logf, cum = _fox_prep(fl.reshape(b, t, nh), b_f)
        o = _fox_attn(q3, k3, v3, cum)
    else:
        k_pool, v_pool, lf_pool, page_table = past
        npool, page = k_pool.shape[:2]
        o, logf = _fox_decode(q.reshape(b, nh, HEAD_DIM), k.reshape(b, nh, HEAD_DIM), v.reshape(b, nh, HEAD_DIM),
                              fl.reshape(b, nh, 1), b_f, k_pool.reshape(npool, page * nh, HEAD_DIM),
                              v_pool.reshape(npool, page * nh, HEAD_DIM), jnp.swapaxes(lf_pool, 1, 2), page_table)
    state = (k.reshape(1, b, t, nh, HEAD_DIM), v.reshape(1, b, t, nh, HEAD_DIM), logf.reshape(1, b, t, nh))
    return o.reshape(b * t, d), z, state


def _pad_summaries(summaries, b, n_cmp, n_pad):
    s = summaries.reshape(2, NSA_GROUPS, b, n_cmp, HEAD_DIM)
    return jnp.pad(jnp.swapaxes(s, 1, 2), ((0, 0), (0, 0), (0, 0), (0, n_pad - n_cmp), (0, 0)))


def _nsa_layer(x, bt, norm_g, w_in, cmp_pos, cmp_w1, cmp_w2, t5_bias, past):
    b, t = bt
    d = x.shape[1]
    kvw = NSA_GROUPS * HEAD_DIM
    off_gate = d + 6 * kvw
    n_gate = 3 * (d // HEAD_DIM)
    wt, tr = _as_rows(w_in)
    assert tr
    q, kv4, win = _norm_proj(x, norm_g, wt, [(0, d), (d, 4 * kvw), (d + 4 * kvw, 2 * kvw)], 512, True)
    (gates,) = _norm_proj(x, norm_g, wt[off_gate:off_gate + n_gate], [(0, n_gate)], n_gate, True)
    (z,) = _norm_proj(x, norm_g, wt[off_gate + n_gate:], [(0, d)], 512, True)
    q3, gates3 = q.reshape(b, t, d), gates.reshape(b, t, n_gate)
    inner = NSA_KINDS * NSA_GROUPS
    if past is None:
        kv3, win3 = kv4.reshape(b, t, 4 * kvw), win.reshape(b, t, 2 * kvw)
        n_cmp = t // CMP_BLOCK
        n_pad = -(-n_cmp // 128) * 128
        summaries = _pad_summaries(_compress(kv4, cmp_pos, cmp_w1, cmp_w2), b, n_cmp, n_pad)
        o, mask = _nsa_cmp(q3, summaries[0], summaries[1], gates3, 0, t, False)
        tiles = _t5_tiles(t5_bias, 128)
        o = _nsa_attn(q3, kv3, 4, 6, tiles, gates3, o, mask, 1)
        o = _nsa_attn(q3, win3, 0, 2, tiles, gates3, o, None, 2)
        keep = min(WINDOW, t)
        win_state = win3[:, t - keep:]
    else:
        kv_pool, win_past, page_table = past
        assert t == 1
        npool, page = kv_pool.shape[:2]
        n_pages = page_table.shape[1]
        plen = n_pages * page
        pool = kv_pool.reshape(npool, page * inner, HEAD_DIM)
        n_cmp = plen // CMP_BLOCK
        n_pad = -(-(n_cmp + 1) // 128) * 128
        summaries = _pad_summaries(_compress(_cmp_gather(pool, page_table, page), cmp_pos, cmp_w1, cmp_w2),
                                   b, n_cmp, n_pad)
        rows = SUBLANES
        q_pad = jnp.pad(q3, ((0, 0), (0, rows - t), (0, 0)))
        g_pad = jnp.pad(gates3, ((0, 0), (0, rows - t), (0, 0)))
        o, sel_idx = _nsa_cmp(q_pad, summaries[0], summaries[1], g_pad, plen, plen + t, True)
        o4 = o[:, :1].reshape(b, NSA_GROUPS, NSA_REP, HEAD_DIM)
        sel_idx = sel_idx[:, :, 0, :].reshape(-1)
        q4 = q3.reshape(b, NSA_GROUPS, NSA_REP, HEAD_DIM)
        gates4 = gates3.reshape(b, NSA_GROUPS, NSA_REP, 3)
        tab_t = t5_bias.T
        blocks = pool.reshape(npool * page // CMP_BLOCK, CMP_BLOCK * inner, HEAD_DIM)
        o4 = _nsa_sel_decode(q4, blocks, kv4.reshape(b, inner, HEAD_DIM), tab_t, gates4, o4, sel_idx,
                             page_table, plen, page)
        lw = win_past.shape[1]
        wp = win_past.reshape(b, lw * 2 * NSA_GROUPS, HEAD_DIM)
        wn = win.reshape(b, 2 * NSA_GROUPS, HEAD_DIM)
        o4 = _nsa_win_decode(q4, wp, wn, tab_t, gates4, o4)
        o = o4.reshape(b, t, d)
        keep = min(WINDOW, lw + t)
        win_state = jnp.concatenate([wp, wn], axis=1)[:, (lw + t - keep) * 2 * NSA_GROUPS:]
    state = (kv4.reshape(1, b, t, NSA_KINDS, NSA_GROUPS, HEAD_DIM),
             win_state.reshape(1, b, -1, 2, NSA_GROUPS, HEAD_DIM))
    return o.reshape(b * t, d), z, state


def _hgrn_layer(x, bt, norm_g, w_in, lbp, hgrn_norm_g, layer, s0):
    b, t = bt
    d = x.shape[1]
    q, f, i_in, z = _norm_proj(x, norm_g, w_in, [(n * d, d) for n in range(4)], 512)
    if t == 1:
        o, s_new = _gla_decode(q, f, i_in, lbp, s0, hgrn_norm_g, layer)
    else:
        o, s_new = _gla(q.reshape(b, t, d), f.reshape(b, t, d), i_in.reshape(b, t, d), lbp, s0, hgrn_norm_g, layer)
    return o.reshape(b * t, d), z, (s_new[None],)


def _sb_layer(x, bt, norm_g, w_in, past):
    b, t = bt
    d = x.shape[1]
    nh = d // HEAD_DIM
    q, k, v, z = _norm_proj(x, norm_g, w_in, [(n * d, d) for n in range(4)], 512)
    if past is None:
        o = _sb_attn(q.reshape(b, t, d), k.reshape(b, t, d), v.reshape(b, t, d))
    else:
        k_pool, v_pool, page_table = past
        npool, page = k_pool.shape[:2]
        o = _sb_decode(q.reshape(b, nh, HEAD_DIM), k_pool.reshape(npool, page * nh, HEAD_DIM),
                       v_pool.reshape(npool, page * nh, HEAD_DIM), page_table, page)
    state = (k.reshape(1, b, t, nh, HEAD_DIM), v.reshape(1, b, t, nh, HEAD_DIM))
    return o.reshape(b * t, d), z, state


def kernel(x_prompt, x_sample, cache_fox_k, cache_fox_v, cache_fox_logf, cache_nsa_kv, state_nsa_win, state_hgrn, cache_sb_k, cache_sb_v, page_table, p_prompt, p_sample, norm_g, final_norm_g, ple_gate_w, ple_proj_w, fox_w_in, fox_b_f, fox_w_out, nsa_w_in, nsa_cmp_pos, nsa_cmp_w1, nsa_cmp_w2, t5_bias, nsa_w_out, hgrn_w_in, hgrn_lower_bounds, hgrn_norm_g, hgrn_w_out, sb_w_in, sb_w_out):
    depth = norm_g.shape[0]
    assert depth == 4 and all(a.shape[0] == 1 for a in (fox_w_in, nsa_w_in, hgrn_w_in, sb_w_in))
    w_out = [w[0].astype(BF16) for w in (fox_w_out, nsa_w_out, hgrn_w_out, sb_w_out)]
    w_gate = [ple_gate_w[i].astype(BF16) for i in range(depth)]
    w_proj = [ple_proj_w[i].astype(BF16) for i in range(depth)]

    def trunk(x3, p4, decode):
        b, t, d = x3.shape
        bt = (b, t)
        x = x3.reshape(b * t, d)
        p = p4.reshape(depth, b * t, -1)
        if decode:
            past = ((cache_fox_k[0], cache_fox_v[0], cache_fox_logf[0], page_table),
                    (cache_nsa_kv[0], state_nsa_win[0], page_table),
                    state_hgrn[0],
                    (cache_sb_k[0], cache_sb_v[0], page_table))
        else:
            past = (None, None, jnp.zeros((b,) + state_hgrn.shape[2:], F32), None)
        o, z, st_a = _fox_layer(x, bt, norm_g[0], fox_w_in[0], fox_b_f[0], past[0])
        x = _layer_tail(o, z, x, p[0], w_out[0], w_gate[0], w_proj[0])
        o, z, st_b = _nsa_layer(x, bt, norm_g[1], nsa_w_in[0], nsa_cmp_pos[0], nsa_cmp_w1[0], nsa_cmp_w2[0],
                                t5_bias, past[1])
        x = _layer_tail(o, z, x, p[1], w_out[1], w_gate[1], w_proj[1])
        o, z, st_c = _hgrn_layer(x, bt, norm_g[2], hgrn_w_in[0], hgrn_lower_bounds, hgrn_norm_g[0], 2, past[2])
        x = _layer_tail(o, z, x, p[2], w_out[2], w_gate[2], w_proj[2])
        o, z, st_d = _sb_layer(x, bt, norm_g[3], sb_w_in[0], past[3])
        x = _layer_tail(o, z, x, p[3], w_out[3], w_gate[3], w_proj[3])
        y = _rmsnorm(x, final_norm_g).reshape(b, t, d)
        return y, st_a, st_b, st_c, st_d

    yp, pa, pb, pc, pd = trunk(x_prompt, p_prompt, False)
    ys, sa, sb, sc, sd = trunk(x_sample, p_sample, True)
    return (yp, ys, pa[0], sa[0], pa[1], sa[1], pa[2], sa[2], pb[0], sb[0], pb[1], sb[1],
            pc[0], sc[0], pd[0], sd[0], pd[1], sd[1])
```

```python
import functools
import math

import jax
import jax.numpy as jnp
from jax import lax
from jax.experimental import pallas as pl
from jax.experimental.pallas import tpu as pltpu

F32 = jnp.float32
BF16 = jnp.bfloat16
I32 = jnp.int32

HEAD_DIM = 128
RMS_EPS = 1e-6
NEG_INF = -1e30
FORCE_SCORE = 1e9
CMP_BLOCK = 64
N_SEL = 16
WINDOW = 512
N_BUCKETS = 32
MAX_DISTANCE = 128
GLA_CHUNK = 32
NSA_GROUPS = 2
NSA_REP = 8
NSA_KINDS = 4
SUBLANES = 8
SCALE = HEAD_DIM ** -0.5
MIB = 1 << 20


def _cparams(sem, vmem_mib=None):
    kw = dict(dimension_semantics=sem)
    if vmem_mib is not None:
        kw["vmem_limit_bytes"] = vmem_mib * MIB
    return pltpu.CompilerParams(**kw)


def _iota(shape, axis):
    return lax.broadcasted_iota(I32, shape, axis)


def _dot(a, b):
    return jnp.dot(a, b, preferred_element_type=F32)


def _dot_nt(a, b):
    return lax.dot_general(a, b, (((1,), (1,)), ((), ())), preferred_element_type=F32)


def _dot_tn(a, b):
    return lax.dot_general(a, b, (((0,), (0,)), ((), ())), preferred_element_type=F32)


def _split3(x):
    hi = x.astype(BF16)
    r1 = x - hi.astype(F32)
    mid = r1.astype(BF16)
    lo = (r1 - mid.astype(F32)).astype(BF16)
    return hi, mid, lo


def _dot3(x, m, dot=_dot):
    hi, mid, lo = _split3(x)
    return dot(hi, m) + dot(mid, m) + dot(lo, m)


def _dot3_rhs(m, x, dot=_dot):
    hi, mid, lo = _split3(x)
    return dot(m, hi) + dot(m, mid) + dot(m, lo)


def _log_sigmoid_pair(x):
    sp = jnp.log1p(jnp.exp(-jnp.abs(x)))
    return jnp.minimum(x, 0.0) - sp, jnp.minimum(-x, 0.0) - sp


def _sigmoid(x):
    return jax.nn.sigmoid(x)


def _t5_bucket(dist):
    exact = N_BUCKETS // 2
    d = jnp.maximum(dist, 0)
    logd = jnp.log(jnp.maximum(d, 1).astype(F32) * (1.0 / exact))
    large = exact + (logd * ((N_BUCKETS - exact) / math.log(MAX_DISTANCE / exact))).astype(I32)
    return jnp.where(d < exact, d, jnp.minimum(large, N_BUCKETS - 1))


def _div_pow2(x, n):
    assert n & (n - 1) == 0
    return lax.shift_right_arithmetic(x, n.bit_length() - 1)


def _row_tile(m, cap):
    t = min(m, cap)
    assert m % t == 0, (m, t)
    return t


def _norm_proj_kernel(x_ref, g_ref, w_ref, *rest, slab_blocks, transposed):
    outs, u_ref = rest[:-1], rest[-1]
    j = pl.program_id(1)

    @pl.when(j == 0)
    def _():
        x = x_ref[...]
        inv = lax.rsqrt(jnp.mean(x * x, axis=-1, keepdims=True) + RMS_EPS)
        u_ref[...] = (x * inv * g_ref[...]).astype(BF16)

    w = w_ref[...].astype(BF16)
    acc = _dot_nt(u_ref[...], w) if transposed else _dot(u_ref[...], w)
    for (lo, n), o_ref in zip(slab_blocks, outs):
        @pl.when((j >= lo) & (j < lo + n))
        def _(o_ref=o_ref):
            o_ref[...] = acc


def _norm_proj(x, g, w, slabs, tn, transposed=False):
    m, d = x.shape
    tm = _row_tile(m, 1024)
    base = slabs[0][0]
    slab_blocks, pos = [], base
    for off, width in slabs:
        assert off == pos and width % tn == 0 and off % tn == 0, (off, width, tn)
        slab_blocks.append(((off - base) // tn, width // tn))
        pos += width
    nj = (pos - base) // tn
    b0 = base // tn

    def out_map(lo, n):
        return lambda i, j: (i, jnp.minimum(jnp.maximum(j - lo, 0), n - 1))

    if transposed:
        w_spec = pl.BlockSpec((tn, d), lambda i, j: (b0 + j, 0))
    else:
        w_spec = pl.BlockSpec((d, tn), lambda i, j: (0, b0 + j))
    return pl.pallas_call(
        functools.partial(_norm_proj_kernel, slab_blocks=tuple(slab_blocks), transposed=transposed),
        grid=(m // tm, nj),
        in_specs=[pl.BlockSpec((tm, d), lambda i, j: (i, 0)), pl.BlockSpec((1, d), lambda i, j: (0, 0)), w_spec],
        out_specs=[pl.BlockSpec((tm, tn), out_map(lo, n)) for lo, n in slab_blocks],
        out_shape=[jax.ShapeDtypeStruct((m, width), F32) for _, width in slabs],
        scratch_shapes=[pltpu.VMEM((tm, d), BF16)],
        compiler_params=_cparams(("parallel", "arbitrary"), 56),
        name="norm_proj",
    )(x, g.reshape(1, d), w)


def _layer_tail_kernel(o_ref, z_ref, x_ref, p_ref, wo_ref, wg_ref, wp_ref, out_ref):
    z = z_ref[...]
    a = (o_ref[...] * (z * _sigmoid(z))).astype(BF16)
    h = x_ref[...] + _dot(a, wo_ref[...])
    gate = _sigmoid(_dot(h.astype(BF16), wg_ref[...]))
    out_ref[...] = h + gate * _dot(p_ref[...].astype(BF16), wp_ref[...])


def _layer_tail(o, z, x, p, wo, wg, wp):
    m, d = o.shape
    pd = p.shape[1]
    tm = _row_tile(m, 256)
    row = pl.BlockSpec((tm, d), lambda i: (i, 0))

    def resident(shape):
        return pl.BlockSpec(shape, lambda i: (0, 0), pipeline_mode=pl.Buffered(1))

    return pl.pallas_call(
        _layer_tail_kernel,
        grid=(m // tm,),
        in_specs=[row, row, row, pl.BlockSpec((tm, pd), lambda i: (i, 0)),
                  resident((d, d)), resident((d, d)), resident((pd, d))],
        out_specs=row,
        out_shape=jax.ShapeDtypeStruct((m, d), F32),
        compiler_params=_cparams(("parallel",), 48),
        name="layer_tail",
    )(o, z, x, p, wo, wg, wp)


def _rmsnorm_kernel(x_ref, g_ref, o_ref):
    x = x_ref[...]
    inv = lax.rsqrt(jnp.mean(x * x, axis=-1, keepdims=True) + RMS_EPS)
    o_ref[...] = x * inv * g_ref[...]


def _rmsnorm(x, g):
    m, d = x.shape
    tm = _row_tile(m, 512)
    return pl.pallas_call(
        _rmsnorm_kernel,
        grid=(m // tm,),
        in_specs=[pl.BlockSpec((tm, d), lambda i: (i, 0)), pl.BlockSpec((1, d), lambda i: (0, 0))],
        out_specs=pl.BlockSpec((tm, d), lambda i: (i, 0)),
        out_shape=jax.ShapeDtypeStruct((m, d), F32),
        compiler_params=_cparams(("parallel",)),
        name="final_rmsnorm",
    )(x, g.reshape(1, d))


def _fox_prep_kernel(x_ref, bf_ref, logf_ref, cum_ref, *, n_chunks):
    tril = (_iota((128, 128), 0) >= _iota((128, 128), 1)).astype(BF16)
    carry = jnp.zeros((1, x_ref.shape[1]), F32)
    for c in range(n_chunks):
        sl = slice(c * 128, (c + 1) * 128)
        lf, _ = _log_sigmoid_pair(x_ref[sl, :] + bf_ref[...])
        logf_ref[sl, :] = lf
        cum = _dot3_rhs(tril, lf) + carry
        cum_ref[sl, :] = cum
        carry = cum[127:128, :]


def _fox_prep(logits, b_f):
    b, t, h = logits.shape
    assert t % 128 == 0
    spec = pl.BlockSpec((None, t, h), lambda i: (i, 0, 0))
    return pl.pallas_call(
        functools.partial(_fox_prep_kernel, n_chunks=t // 128),
        grid=(b,),
        in_specs=[spec, pl.BlockSpec((1, h), lambda i: (0, 0))],
        out_specs=[spec, spec],
        out_shape=[jax.ShapeDtypeStruct((b, t, h), F32)] * 2,
        compiler_params=_cparams(("parallel",)),
        name="fox_prep",
    )(logits, b_f.reshape(1, h))


def _pick_col(x, c):
    return jnp.sum(jnp.where(_iota(x.shape, 1) == c, x, 0.0), axis=1, keepdims=True)


def _bias_lanes(col, ones_first):
    rows = col.shape[0]
    hi, mid, lo = (p.astype(F32) for p in _split3(col))
    lane = _iota((rows, HEAD_DIM), 1)
    one, val = (0, 3) if ones_first else (3, 0)
    x = jnp.where(lane == val, hi, jnp.where(lane == val + 1, mid, jnp.where(lane == val + 2, lo, 0.0)))
    x = jnp.where((lane >= one) & (lane < one + 3), 1.0, x)
    return x.astype(BF16)


def _fox_attn_kernel(q_ref, k_ref, v_ref, cq_ref, ck_ref, o_ref, ka_ref, vb_ref, *, tq):
    h = pl.program_id(1)
    qi = pl.program_id(2)
    t = k_ref.shape[0]

    @pl.when(qi == 0)
    def _():
        for c in range(t // tq):
            sl = slice(c * tq, (c + 1) * tq)
            ka_ref[sl, :HEAD_DIM] = k_ref[sl, :].astype(BF16)
            ka_ref[sl, HEAD_DIM:] = _bias_lanes(-_pick_col(ck_ref[sl, :], h), True)
            vb_ref[sl, :] = v_ref[sl, :].astype(BF16)

    qa = jnp.concatenate([(q_ref[...] * SCALE).astype(BF16), _bias_lanes(_pick_col(cq_ref[...], h), False)], axis=1)
    tri = _iota((tq, tq), 1) <= _iota((tq, tq), 0)

    def step(kc, carry, diagonal):
        m, l, acc = carry
        ks = pl.multiple_of(kc * tq, tq)
        s = _dot_nt(qa, ka_ref[pl.ds(ks, tq), :])
        if diagonal:
            s = jnp.where(tri, s, NEG_INF)
        m_new = jnp.maximum(m, jnp.max(s, axis=-1, keepdims=True))
        p = jnp.exp(s - m_new)
        alpha = jnp.exp(m - m_new)
        l = alpha * l + jnp.sum(p, axis=-1, keepdims=True)
        acc = alpha * acc + _dot(p.astype(BF16), vb_ref[pl.ds(ks, tq), :])
        return m_new, l, acc

    init = (jnp.full((tq, 1), NEG_INF, F32), jnp.zeros((tq, 1), F32), jnp.zeros((tq, HEAD_DIM), F32))
    carry = lax.fori_loop(0, qi, lambda kc, c: step(kc, c, False), init)
    _, l, acc = step(qi, carry, True)
    o_ref[...] = acc / l


def _fox_attn(q, k, v, cum, tq=512):
    b, t, w = q.shape
    h = w // HEAD_DIM
    tq = min(tq, t)
    assert t % tq == 0
    return pl.pallas_call(
        functools.partial(_fox_attn_kernel, tq=tq),
        grid=(b, h, t // tq),
        in_specs=[
            pl.BlockSpec((None, tq, HEAD_DIM), lambda bi, hi, qi: (bi, qi, hi)),
            pl.BlockSpec((None, t, HEAD_DIM), lambda bi, hi, qi: (bi, 0, hi)),
            pl.BlockSpec((None, t, HEAD_DIM), lambda bi, hi, qi: (bi, 0, hi)),
            pl.BlockSpec((None, tq, h), lambda bi, hi, qi: (bi, qi, 0)),
            pl.BlockSpec((None, t, h), lambda bi, hi, qi: (bi, 0, 0)),
        ],
        out_specs=pl.BlockSpec((None, tq, HEAD_DIM), lambda bi, hi, qi: (bi, qi, hi)),
        out_shape=jax.ShapeDtypeStruct((b, t, w), F32),
        scratch_shapes=[pltpu.VMEM((t, 2 * HEAD_DIM), BF16), pltpu.VMEM((t, HEAD_DIM), BF16)],
        compiler_params=_cparams(("arbitrary", "arbitrary", "arbitrary")),
        name="fox_attn",
    )(q, k, v, cum, cum)


def _sb_attn_kernel(q_ref, k_ref, v_ref, o_ref, kb_ref, vb_ref, *, tq):
    qi = pl.program_id(2)
    t = k_ref.shape[0]
    sub = HEAD_DIM

    @pl.when(qi == 0)
    def _():
        for c in range(t // tq):
            sl = slice(c * tq, (c + 1) * tq)
            kb_ref[sl, :] = k_ref[sl, :].astype(BF16)
            vb_ref[sl, :] = v_ref[sl, :].astype(BF16)

    q = (q_ref[...] * SCALE).astype(BF16)
    later = (_iota((sub, sub), 0) > _iota((sub, sub), 1)).astype(BF16)
    scan = jnp.concatenate([later, jnp.ones((sub, sub), BF16)], axis=1)
    scan = jnp.concatenate([scan, scan], axis=0)
    before = _iota((tq, tq), 1) < _iota((tq, tq), 0)

    def step(kc, carry, diagonal):
        tail, acc = carry
        ks = pl.multiple_of(kc * tq, tq)
        x = _dot_nt(q, kb_ref[pl.ds(ks, tq), :])
        ls_pos = jnp.minimum(x, 0.0) - jnp.log(1.0 + jnp.exp(-jnp.abs(x)))
        ls_neg = ls_pos - x
        if diagonal:
            ls_neg = jnp.where(before, ls_neg, 0.0)
        ws = [None] * (tq // sub)
        for j in reversed(range(tq // sub)):
            sl = slice(j * sub, (j + 1) * sub)
            stay = ls_neg[:, sl]
            hi = stay.astype(BF16)
            lo = (stay - hi.astype(F32)).astype(BF16)
            sums = _dot(jnp.concatenate([hi, lo], axis=1), scan)
            w = jnp.exp(ls_pos[:, sl] + sums[:, :sub] + tail)
            if diagonal:
                w = jnp.where(before[:, sl], w, 0.0)
            ws[j] = w.astype(BF16)
            tail = tail + sums[:, sub:]
        acc = acc + _dot(jnp.concatenate(ws, axis=1), vb_ref[pl.ds(ks, tq), :])
        return tail, acc

    carry = step(qi, (jnp.zeros((tq, sub), F32), jnp.zeros((tq, HEAD_DIM), F32)), True)
    _, acc = lax.fori_loop(0, qi, lambda i, c: step(qi - 1 - i, c, False), carry)
    o_ref[...] = acc


def _sb_attn(q, k, v, tq=512):
    b, t, w = q.shape
    h = w // HEAD_DIM
    tq = min(tq, t)
    assert t % tq == 0
    return pl.pallas_call(
        functools.partial(_sb_attn_kernel, tq=tq),
        grid=(b, h, t // tq),
        in_specs=[
            pl.BlockSpec((None, tq, HEAD_DIM), lambda bi, hi, qi: (bi, qi, hi)),
            pl.BlockSpec((None, t, HEAD_DIM), lambda bi, hi, qi: (bi, 0, hi)),
            pl.BlockSpec((None, t, HEAD_DIM), lambda bi, hi, qi: (bi, 0, hi)),
        ],
        out_specs=pl.BlockSpec((None, tq, HEAD_DIM), lambda bi, hi, qi: (bi, qi, hi)),
        out_shape=jax.ShapeDtypeStruct((b, t, w), F32),
        scratch_shapes=[pltpu.VMEM((t, HEAD_DIM), BF16), pltpu.VMEM((t, HEAD_DIM), BF16)],
        compiler_params=_cparams(("arbitrary", "arbitrary", "arbitrary")),
        name="sb_attn",
    )(q, k, v)


FOX_DEC_PAGES = 2
SB_DEC_PAGES = 4


def _page_scores(k_ref, q, nh, page):
    y = (k_ref[...].reshape(page, nh, HEAD_DIM) * q[None]).astype(BF16)
    s = _dot(y.reshape(page * nh, HEAD_DIM), jnp.ones((HEAD_DIM, HEAD_DIM), BF16))
    return s.reshape(page, nh, HEAD_DIM)


def _fox_dec_kernel(pt_ref, q_ref, kn_ref, vn_ref, fl_ref, bf_ref, *refs, pp):
    k_refs, v_refs, lft_refs = refs[:pp], refs[pp:2 * pp], refs[2 * pp:3 * pp]
    o_ref, lfo_ref, m_ref, l_ref, c_ref, acc_ref = refs[3 * pp:]
    p = pl.program_id(1)
    nh, page = lft_refs[0].shape
    q = q_ref[...] * SCALE

    @pl.when(p == 0)
    def _():
        s_new = jnp.sum(q * kn_ref[...], axis=-1, keepdims=True)
        m_ref[...] = jnp.broadcast_to(s_new, m_ref.shape)
        l_ref[...] = jnp.ones_like(l_ref)
        acc_ref[...] = vn_ref[...]
        lf_new, _ = _log_sigmoid_pair(fl_ref[...] + bf_ref[...])
        lfo_ref[...] = lf_new
        c_ref[...] = lf_new

    later = (_iota((page, page), 0) > _iota((page, page), 1)).astype(BF16)
    for i in range(pp):
        lft = lft_refs[i][...]
        decay = c_ref[...] + _dot3(lft, later)
        decay = jnp.stack([jnp.broadcast_to(decay[:, r:r + 1], (nh, HEAD_DIM)) for r in range(page)], axis=0)
        s = _page_scores(k_refs[i], q, nh, page) + decay
        m_old = m_ref[...]
        m_new = jnp.maximum(m_old, jnp.max(s, axis=0))
        pr = jnp.exp(s - m_new[None])
        alpha = jnp.exp(m_old - m_new)
        l_ref[...] = alpha * l_ref[...] + jnp.sum(pr, axis=0)
        acc_ref[...] = alpha * acc_ref[...] + jnp.sum(pr * v_refs[i][...].reshape(page, nh, HEAD_DIM), axis=0)
        m_ref[...] = m_new
        c_ref[...] = c_ref[...] + jnp.sum(lft, axis=1, keepdims=True)

    @pl.when(p == pl.num_programs(1) - 1)
    def _():
        o_ref[...] = acc_ref[...] / l_ref[...]


def _fox_decode(q, k_new, v_new, fl, b_f, k_pool, v_pool, lf_pool_t, page_table):
    b, nh, _ = q.shape
    n_pages = page_table.shape[1]
    page = lf_pool_t.shape[2]
    pp = math.gcd(FOX_DEC_PAGES, n_pages)

    def row(bi, p, pt):
        return (bi, 0, 0)

    def pg(i):
        return lambda bi, p, pt: (pt[bi, n_pages - 1 - (p * pp + i)], 0, 0)

    head = pl.BlockSpec((None, nh, HEAD_DIM), row)
    stat = pltpu.VMEM((nh, HEAD_DIM), F32)
    grid_spec = pltpu.PrefetchScalarGridSpec(
        num_scalar_prefetch=1,
        grid=(b, n_pages // pp),
        in_specs=[head, head, head, pl.BlockSpec((None, nh, 1), row), pl.BlockSpec((nh, 1), lambda bi, p, pt: (0, 0))]
        + [pl.BlockSpec((None, page * nh, HEAD_DIM), pg(i)) for i in range(pp)] * 2
        + [pl.BlockSpec((None, nh, page), pg(i)) for i in range(pp)],
        out_specs=[head, pl.BlockSpec((None, nh, 1), row)],
        scratch_shapes=[stat, stat, pltpu.VMEM((nh, 1), F32), stat],
    )
    return pl.pallas_call(
        functools.partial(_fox_dec_kernel, pp=pp),
        grid_spec=grid_spec,
        out_shape=[jax.ShapeDtypeStruct((b, nh, HEAD_DIM), F32), jax.ShapeDtypeStruct((b, nh, 1), F32)],
        compiler_params=_cparams(("arbitrary", "arbitrary")),
        name="fox_decode",
    )(page_table, q, k_new, v_new, fl, b_f.reshape(nh, 1), *([k_pool] * pp), *([v_pool] * pp), *([lf_pool_t] * pp))


def _sb_dec_kernel(pt_ref, q_ref, *refs, pp, page):
    k_refs, v_refs = refs[:pp], refs[pp:2 * pp]
    o_ref, c_ref, acc_ref = refs[2 * pp:]
    p = pl.program_id(1)
    nh = q_ref.shape[0]
    q = q_ref[...] * SCALE

    @pl.when(p == 0)
    def _():
        c_ref[...] = jnp.zeros_like(c_ref)
        acc_ref[...] = jnp.zeros_like(acc_ref)

    for i in range(pp):
        x = _page_scores(k_refs[i], q, nh, page)
        ls_pos = jnp.minimum(x, 0.0) - jnp.log(1.0 + jnp.exp(-jnp.abs(x)))
        ls_neg = ls_pos - x
        tail = c_ref[...]
        ws = [None] * page
        for r in reversed(range(page)):
            ws[r] = jnp.exp(ls_pos[r] + tail)
            tail = tail + ls_neg[r]
        acc_ref[...] = acc_ref[...] + jnp.sum(jnp.stack(ws, axis=0) * v_refs[i][...].reshape(page, nh, HEAD_DIM), axis=0)
        c_ref[...] = tail

    @pl.when(p == pl.num_programs(1) - 1)
    def _():
        o_ref[...] = acc_ref[...]


def _sb_decode(q, k_pool, v_pool, page_table, page):
    b, nh, _ = q.shape
    n_pages = page_table.shape[1]
    pp = math.gcd(SB_DEC_PAGES, n_pages)

    def row(bi, p, pt):
        return (bi, 0, 0)

    def pg(i):
        return lambda bi, p, pt: (pt[bi, n_pages - 1 - (p * pp + i)], 0, 0)

    head = pl.BlockSpec((None, nh, HEAD_DIM), row)
    stat = pltpu.VMEM((nh, HEAD_DIM), F32)
    grid_spec = pltpu.PrefetchScalarGridSpec(
        num_scalar_prefetch=1,
        grid=(b, n_pages // pp),
        in_specs=[head] + [pl.BlockSpec((None, page * nh, HEAD_DIM), pg(i)) for i in range(pp)] * 2,
        out_specs=head,
        scratch_shapes=[stat, stat],
    )
    return pl.pallas_call(
        functools.partial(_sb_dec_kernel, pp=pp, page=page),
        grid_spec=grid_spec,
        out_shape=jax.ShapeDtypeStruct((b, nh, HEAD_DIM), F32),
        compiler_params=_cparams(("arbitrary", "arbitrary")),
        name="sb_decode",
    )(page_table, q, *([k_pool] * pp), *([v_pool] * pp))


def _cmp_gather_kernel(pt_ref, *refs, inner, page):
    x_refs, o_ref = refs[:-1], refs[-1]
    for i, x_ref in enumerate(x_refs):
        for c in range(2 * NSA_GROUPS):
            o_ref[c, i * page:(i + 1) * page, :] = x_ref[pl.ds(c, page, stride=inner), :]


def _cmp_gather(pool, page_table, page, pp=8):
    b, n_pages = page_table.shape
    inner = pool.shape[1] // page
    pp = math.gcd(pp, n_pages)
    steps = n_pages // pp

    def page_spec(i):
        return pl.BlockSpec((None, page * inner, HEAD_DIM), lambda bi, p, pt: (pt[bi, p * pp + i], 0, 0))

    grid_spec = pltpu.PrefetchScalarGridSpec(
        num_scalar_prefetch=1,
        grid=(b, steps),
        in_specs=[page_spec(i) for i in range(pp)],
        out_specs=pl.BlockSpec((2 * NSA_GROUPS, pp * page, HEAD_DIM), lambda bi, p, pt: (0, bi * steps + p, 0)),
    )
    return pl.pallas_call(
        functools.partial(_cmp_gather_kernel, inner=inner, page=page),
        grid_spec=grid_spec,
        out_shape=jax.ShapeDtypeStruct((2 * NSA_GROUPS, b * n_pages * page, HEAD_DIM), F32),
        compiler_params=_cparams(("parallel", "arbitrary")),
        name="nsa_cmp_gather",
    )(page_table, *([pool] * pp))


def _compress_kernel(x_ref, pos_ref, w1_ref, w2_ref, o_ref, *, rb):
    hidden = w1_ref.shape[2]
    hid = jnp.zeros((rb, hidden), F32)
    for c in range(0, CMP_BLOCK, 2):
        xc = [(x_ref[pl.ds(c + i, rb, stride=CMP_BLOCK), :] + pos_ref[c + i:c + i + 1, :]).astype(BF16) for i in (0, 1)]
        w = w1_ref[c:c + 2].reshape(2 * HEAD_DIM, hidden).astype(BF16)
        hid = hid + _dot(jnp.concatenate(xc, axis=1), w)
    hid = hid * _sigmoid(hid)
    o_ref[...] = _dot(hid.astype(BF16), w2_ref[...].astype(BF16))


def _compress(x, pos, w1, w2):
    r = x.shape[-2]
    nb = r // CMP_BLOCK
    rb = _row_tile(nb, 256)
    hidden = w1.shape[-1]
    w1v = w1.reshape(2, CMP_BLOCK, HEAD_DIM, hidden)
    if x.ndim == 2:
        x_spec = pl.BlockSpec((rb * CMP_BLOCK, HEAD_DIM), lambda c, i: (i, c))
    else:
        x_spec = pl.BlockSpec((None, rb * CMP_BLOCK, HEAD_DIM), lambda c, i: (c, i, 0))
    return pl.pallas_call(
        functools.partial(_compress_kernel, rb=rb),
        grid=(2 * NSA_GROUPS, nb // rb),
        in_specs=[
            x_spec,
            pl.BlockSpec((None, CMP_BLOCK, HEAD_DIM), lambda c, i: (c // NSA_GROUPS, 0, 0)),
            pl.BlockSpec((None, CMP_BLOCK, HEAD_DIM, hidden), lambda c, i: (c // NSA_GROUPS, 0, 0, 0)),
            pl.BlockSpec((None, hidden, HEAD_DIM), lambda c, i: (c // NSA_GROUPS, 0, 0)),
        ],
        out_specs=pl.BlockSpec((None, rb, HEAD_DIM), lambda c, i: (c, i, 0)),
        out_shape=jax.ShapeDtypeStruct((2 * NSA_GROUPS, nb, HEAD_DIM), F32),
        compiler_params=_cparams(("parallel", "arbitrary"), 48),
        name="nsa_compress",
    )(x, pos, w1v, w2)


def _gate_col(gl, col):
    lane = _iota(gl.shape, 1)
    return _sigmoid(jnp.sum(jnp.where(lane == col, gl, 0.0), axis=-1, keepdims=True))


def _nsa_cmp_kernel(q_ref, kc_ref, vc_ref, gate_ref, o_ref, sel_ref, *, past, tq, n_cmp, n_blk, t_keys, decode):
    g = pl.program_id(1)
    qi = pl.program_id(2)
    n_pad = kc_ref.shape[0]
    kc = kc_ref[...].astype(BF16)
    vc = vc_ref[...].astype(BF16)
    qpos = past + qi * tq + _iota((tq, 1), 0)
    blk = _iota((tq, n_pad), 1)
    visible = ((blk + 1) * CMP_BLOCK - 1 <= qpos) & (blk < n_cmp)
    gl = gate_ref[...]
    q_all = jnp.concatenate([q_ref[:, r * HEAD_DIM:(r + 1) * HEAD_DIM].astype(BF16) for r in range(NSA_REP)], axis=0)
    s = (_dot_nt(q_all, kc) * SCALE).reshape(NSA_REP, tq, n_pad)
    s = jnp.where(visible[None], s, NEG_INF)
    e = jnp.exp(s - jnp.max(s, axis=-1, keepdims=True))
    pr = jnp.where(visible[None], e / jnp.sum(e, axis=-1, keepdims=True), 0.0)
    imp = jnp.sum(pr, axis=0)
    gate = jnp.concatenate([_gate_col(gl, (g * NSA_REP + r) * 3) for r in range(NSA_REP)], axis=0)
    o = gate * _dot(pr.reshape(NSA_REP * tq, n_pad).astype(BF16), vc)
    o_ref[...] = jnp.concatenate([o[r * tq:(r + 1) * tq] for r in range(NSA_REP)], axis=1)

    cur = _div_pow2(qpos, CMP_BLOCK)
    forced = (blk == 0) | (blk == cur) | (blk == cur - 1)
    score = jnp.where(blk > cur, NEG_INF, jnp.where(forced, FORCE_SCORE, imp))
    if decode:
        row = score[0:1, :]
        col = jnp.sum(jnp.where(_iota((n_pad, n_pad), 0) == _iota((n_pad, n_pad), 1),
                                jnp.broadcast_to(row, (n_pad, n_pad)), 0.0), axis=1, keepdims=True)
        i_blk, j_blk = _iota((n_pad, n_pad), 0), _iota((n_pad, n_pad), 1)
        first = ((col > row) | ((col == row) & (i_blk < j_blk))) & (i_blk < n_blk)
        rank = jnp.sum(jnp.where(first, 1.0, 0.0), axis=0, keepdims=True)
        blk_f = blk[0:1, :].astype(F32)
        lane = _iota((1, N_SEL), 1)
        idx = jnp.zeros((1, N_SEL), F32)
        for k in range(N_SEL):
            ik = jnp.sum(jnp.where(rank == k, blk_f, 0.0), axis=1, keepdims=True)
            idx = jnp.where(lane == k, ik, idx)
        sel_ref[...] = jnp.broadcast_to(idx, sel_ref.shape).astype(I32)
    else:
        rank = jnp.zeros((tq, n_pad), F32)
        for i in range(n_blk):
            col = jnp.sum(jnp.where(blk == i, score, 0.0), axis=1, keepdims=True)
            first = (col > score) | ((col == score) & (i < blk))
            rank = rank + jnp.where(first, 1.0, 0.0)
        sel = jnp.where(rank < N_SEL, 1.0, 0.0).astype(BF16)
        expand = (_div_pow2(_iota((n_pad, t_keys), 1), CMP_BLOCK) == _iota((n_pad, t_keys), 0)).astype(BF16)
        sel_ref[...] = _dot(sel, expand).astype(BF16)


def _nsa_cmp(q, kc, vc, gates, past, n_keys, decode):
    b, t, w = q.shape
    n_pad = kc.shape[2]
    n_cmp = n_keys // CMP_BLOCK
    n_blk = -(-n_keys // CMP_BLOCK)
    assert N_SEL <= n_blk <= n_pad
    tq = min(t, 128)
    gw = w // NSA_GROUPS
    if decode:
        sel_shape, sel_block, sel_dtype = (b, NSA_GROUPS, t, N_SEL), (None, None, tq, N_SEL), I32
    else:
        sel_shape, sel_block, sel_dtype = (b, NSA_GROUPS, t, t), (None, None, tq, t), BF16
    return pl.pallas_call(
        functools.partial(_nsa_cmp_kernel, past=past, tq=tq, n_cmp=n_cmp, n_blk=n_blk, t_keys=t, decode=decode),
        grid=(b, NSA_GROUPS, t // tq),
        in_specs=[
            pl.BlockSpec((None, tq, gw), lambda bi, g, qi: (bi, qi, g)),
            pl.BlockSpec((None, None, n_pad, HEAD_DIM), lambda bi, g, qi: (bi, g, 0, 0)),
            pl.BlockSpec((None, None, n_pad, HEAD_DIM), lambda bi, g, qi: (bi, g, 0, 0)),
            pl.BlockSpec((None, tq, gates.shape[-1]), lambda bi, g, qi: (bi, qi, 0)),
        ],
        out_specs=[
            pl.BlockSpec((None, tq, gw), lambda bi, g, qi: (bi, qi, g)),
            pl.BlockSpec(sel_block, lambda bi, g, qi: (bi, g, qi, 0)),
        ],
        out_shape=[jax.ShapeDtypeStruct((b, t, w), F32), jax.ShapeDtypeStruct(sel_shape, sel_dtype)],
        compiler_params=_cparams(("parallel", "parallel", "arbitrary")),
        name="nsa_cmp_select",
    )(q, kc, vc, gates)


def _t5_tiles_kernel(tab_ref, o_ref, *, tq, tk, n_tiles):
    h = pl.program_id(0)
    d0 = _iota((tq, tk), 0) - _iota((tq, tk), 1)
    for t in range(n_tiles):
        bucket = _t5_bucket(d0 + t * tq)
        acc = jnp.zeros((tq, tk), F32)
        for n in range(N_BUCKETS):
            acc = jnp.where(bucket == n, tab_ref[n, h], acc)
        o_ref[t] = acc


def _t5_tiles(t5_bias, tq, tk):
    nh = t5_bias.shape[1]
    n_tiles = -(-(MAX_DISTANCE + tk - 1) // tq) + 1
    return pl.pallas_call(
        functools.partial(_t5_tiles_kernel, tq=tq, tk=tk, n_tiles=n_tiles),
        grid=(nh,),
        in_specs=[pl.BlockSpec(memory_space=pltpu.SMEM)],
        out_specs=pl.BlockSpec((None, n_tiles, tq, tk), lambda h: (h, 0, 0, 0)),
        out_shape=jax.ShapeDtypeStruct((nh, n_tiles, tq, tk), F32),
        compiler_params=_cparams(("arbitrary",)),
        name="t5_tiles",
    )(t5_bias)


def _nsa_attn_kernel(*refs, tq, windowed, branch):
    if windowed:
        q_ref, k_ref, v_ref, tiles_ref, gate_ref, prev_ref, o_ref, kb_ref, vb_ref = refs
        mask_ref = None
    else:
        q_ref, k_ref, v_ref, tiles_ref, gate_ref, prev_ref, mask_ref, o_ref, kb_ref, vb_ref = refs
    g = pl.program_id(1)
    qi = pl.program_id(2)
    t = k_ref.shape[0]

    @pl.when(qi == 0)
    def _():
        for c in range(t // tq):
            sl = slice(c * tq, (c + 1) * tq)
            kb_ref[sl, :] = k_ref[sl, :].astype(BF16)
            vb_ref[sl, :] = v_ref[sl, :].astype(BF16)

    rows = NSA_REP * tq
    q_all = jnp.concatenate([(q_ref[:, r * HEAD_DIM:(r + 1) * HEAD_DIM] * SCALE).astype(BF16)
                             for r in range(NSA_REP)], axis=0)
    tk = tiles_ref.shape[-1]
    per = tk // tq
    far = tiles_ref.shape[1] - 1
    dist0 = _iota((tq, tk), 0) - _iota((tq, tk), 1)

    def body(kc, carry):
        m, l, acc = carry
        ks = pl.multiple_of(kc * tk, tk)
        off = qi - kc * per
        dist = dist0 + off * tq
        ok = dist >= 0
        if windowed:
            ok = ok & (dist < WINDOW)
        else:
            ok = ok & (mask_ref[:, pl.ds(ks, tk)].astype(F32) > 0.5)
        s = _dot_nt(q_all, kb_ref[pl.ds(ks, tk), :]).reshape(NSA_REP, tq, tk)
        s = jnp.where(ok[None], s + tiles_ref[:, jnp.minimum(off, far)], NEG_INF).reshape(rows, tk)
        m_new = jnp.maximum(m, jnp.max(s, axis=-1, keepdims=True))
        p = jnp.exp(s - m_new)
        if windowed:
            p = jnp.where(s > 0.5 * NEG_INF, p, 0.0)
        alpha = jnp.exp(m - m_new)
        l = alpha * l + jnp.sum(p, axis=-1, keepdims=True)
        acc = alpha * acc + _dot(p.astype(BF16), vb_ref[pl.ds(ks, tk), :])
        return m_new, l, acc

    lo = jnp.maximum(lax.shift_right_arithmetic(qi * tq - (WINDOW - 1), tk.bit_length() - 1), 0) if windowed else 0
    init = (jnp.full((rows, 1), NEG_INF, F32), jnp.zeros((rows, 1), F32), jnp.zeros((rows, HEAD_DIM), F32))
    _, l, acc = lax.fori_loop(lo, qi // per + 1, body, init)
    gl = gate_ref[...]
    gate = jnp.concatenate([_gate_col(gl, (g * NSA_REP + r) * 3 + branch) for r in range(NSA_REP)], axis=0)
    o = gate * (acc / l)
    o_ref[...] = prev_ref[...] + jnp.concatenate([o[r * tq:(r + 1) * tq] for r in range(NSA_REP)], axis=1)


def _nsa_attn(q, kv, k_blk, v_blk, tiles, gates, prev, mask, branch):
    b, t, w = q.shape
    gw = w // NSA_GROUPS
    n_tiles, tq, tk = tiles.shape[1:]
    assert t % tk == 0 and tk % tq == 0 and tk & (tk - 1) == 0
    windowed = mask is None
    in_specs = [
        pl.BlockSpec((None, tq, gw), lambda bi, g, qi: (bi, qi, g)),
        pl.BlockSpec((None, t, HEAD_DIM), lambda bi, g, qi: (bi, 0, k_blk + g)),
        pl.BlockSpec((None, t, HEAD_DIM), lambda bi, g, qi: (bi, 0, v_blk + g)),
        pl.BlockSpec((NSA_REP, n_tiles, tq, tk), lambda bi, g, qi: (g, 0, 0, 0)),
        pl.BlockSpec((None, tq, gates.shape[-1]), lambda bi, g, qi: (bi, qi, 0)),
        pl.BlockSpec((None, tq, gw), lambda bi, g, qi: (bi, qi, g)),
    ]
    args = [q, kv, kv, tiles, gates, prev]
    if not windowed:
        in_specs.append(pl.BlockSpec((None, None, tq, t), lambda bi, g, qi: (bi, g, qi, 0)))
        args.append(mask)
    return pl.pallas_call(
        functools.partial(_nsa_attn_kernel, tq=tq, windowed=windowed, branch=branch),
        grid=(b, NSA_GROUPS, t // tq),
        in_specs=in_specs,
        out_specs=pl.BlockSpec((None, tq, gw), lambda bi, g, qi: (bi, qi, g)),
        out_shape=jax.ShapeDtypeStruct((b, t, w), F32),
        scratch_shapes=[pltpu.VMEM((t, HEAD_DIM), BF16), pltpu.VMEM((t, HEAD_DIM), BF16)],
        compiler_params=_cparams(("arbitrary", "arbitrary", "arbitrary")),
        name="nsa_window" if windowed else "nsa_select",
    )(*args)


def _t5_rows(tab_t, dist):
    n = dist.shape[1]
    onehot = (_iota((N_BUCKETS, n), 0) == _t5_bucket(dist)).astype(BF16)
    return _dot3(tab_t, onehot)


def _pick_row(x, r):
    return jnp.sum(jnp.where(_iota(x.shape, 0) == r, x, 0.0), axis=0, keepdims=True)


def _nsa_sel_dec_kernel(pt_ref, idx_ref, q_ref, kv_ref, new_ref, tab_ref, gate_ref, prev_ref,
                        o_ref, m_ref, l_ref, acc_ref, *, past, n_past_blk, inner):
    bi, g, j = pl.program_id(0), pl.program_id(1), pl.program_id(2)
    blk = idx_ref[(bi * NSA_GROUPS + g) * N_SEL + j]

    @pl.when(j == 0)
    def _():
        m_ref[...] = jnp.full_like(m_ref, NEG_INF)
        l_ref[...] = jnp.zeros_like(l_ref)
        acc_ref[...] = jnp.zeros_like(acc_ref)

    k_row = 2 * NSA_GROUPS + g
    v_row = 3 * NSA_GROUPS + g
    is_new = blk >= n_past_blk
    new = new_ref[...]
    k_old = kv_ref[pl.ds(k_row, CMP_BLOCK, stride=inner), :]
    v_old = kv_ref[pl.ds(v_row, CMP_BLOCK, stride=inner), :]
    k = jnp.where(is_new, jnp.broadcast_to(_pick_row(new, k_row), k_old.shape), k_old).astype(BF16)
    v = jnp.where(is_new, jnp.broadcast_to(_pick_row(new, v_row), v_old.shape), v_old).astype(BF16)
    dist = past - (blk * CMP_BLOCK + _iota((1, CMP_BLOCK), 1))
    ok = dist >= 0
    s = _dot_nt(q_ref[...].astype(BF16), k) * SCALE + _t5_rows(tab_ref[...], dist)
    s = jnp.where(ok, s, NEG_INF)
    m_old = m_ref[...]
    m_new = jnp.maximum(m_old, jnp.max(s, axis=-1, keepdims=True))
    p = jnp.where(ok, jnp.exp(s - m_new), 0.0)
    alpha = jnp.exp(m_old - m_new)
    l_ref[...] = alpha * l_ref[...] + jnp.sum(p, axis=-1, keepdims=True)
    acc_ref[...] = alpha * acc_ref[...] + _dot(p.astype(BF16), v)
    m_ref[...] = m_new

    @pl.when(j == N_SEL - 1)
    def _():
        o_ref[...] = prev_ref[...] + _sigmoid(gate_ref[:, 1:2]) * (acc_ref[...] / l_ref[...])


def _nsa_sel_decode(q4, pool_blocks, kv_new, tab_t, gates4, prev4, sel_idx, page_table, past, page):
    b = q4.shape[0]
    assert past % page == 0 and page % CMP_BLOCK == 0
    n_past_blk = past // CMP_BLOCK
    per_page = page // CMP_BLOCK
    inner = pool_blocks.shape[1] // CMP_BLOCK

    def blk_row(bi, g, j, pt, idx):
        n = jnp.minimum(idx[(bi * NSA_GROUPS + g) * N_SEL + j], n_past_blk - 1)
        return pt[bi, n // per_page] * per_page + n % per_page

    def grp(bi, g, j, pt, idx):
        return (bi, g, 0, 0)

    grid_spec = pltpu.PrefetchScalarGridSpec(
        num_scalar_prefetch=2,
        grid=(b, NSA_GROUPS, N_SEL),
        in_specs=[
            pl.BlockSpec((None, None, NSA_REP, HEAD_DIM), grp),
            pl.BlockSpec((None, CMP_BLOCK * inner, HEAD_DIM), lambda bi, g, j, pt, idx: (blk_row(bi, g, j, pt, idx), 0, 0)),
            pl.BlockSpec((None, inner, HEAD_DIM), lambda bi, g, j, pt, idx: (bi, 0, 0)),
            pl.BlockSpec((NSA_REP, N_BUCKETS), lambda bi, g, j, pt, idx: (g, 0)),
            pl.BlockSpec((None, None, NSA_REP, 3), grp),
            pl.BlockSpec((None, None, NSA_REP, HEAD_DIM), grp),
        ],
        out_specs=pl.BlockSpec((None, None, NSA_REP, HEAD_DIM), grp),
        scratch_shapes=[pltpu.VMEM((NSA_REP, 1), F32), pltpu.VMEM((NSA_REP, 1), F32),
                        pltpu.VMEM((NSA_REP, HEAD_DIM), F32)],
    )
    return pl.pallas_call(
        functools.partial(_nsa_sel_dec_kernel, past=past, n_past_blk=n_past_blk, inner=inner),
        grid_spec=grid_spec,
        out_shape=jax.ShapeDtypeStruct(q4.shape, F32),
        compiler_params=_cparams(("parallel", "parallel", "arbitrary")),
        name="nsa_select_decode",
    )(page_table, sel_idx, q4, pool_blocks, kv_new, tab_t, gates4, prev4)


def _nsa_win_dec_kernel(q_ref, win_ref, new_ref, tab_ref, gate_ref, prev_ref, o_ref, *, lw):
    g = pl.program_id(1)
    inner = 2 * NSA_GROUPS
    q = q_ref[...].astype(BF16)
    k = win_ref[pl.ds(g, lw, stride=inner), :].astype(BF16)
    v = win_ref[pl.ds(NSA_GROUPS + g, lw, stride=inner), :].astype(BF16)
    new = new_ref[...]
    dist = lw - _iota((1, lw), 1)
    ok = dist < WINDOW
    tab = tab_ref[...]
    s = jnp.where(ok, _dot_nt(q, k) * SCALE + _t5_rows(tab, dist), NEG_INF)
    kn = _pick_row(new, g).astype(BF16).astype(F32)
    s_new = jnp.sum(q.astype(F32) * kn, axis=-1, keepdims=True) * SCALE + tab[:, 0:1]
    m = jnp.maximum(jnp.max(s, axis=-1, keepdims=True), s_new)
    p = jnp.where(ok, jnp.exp(s - m), 0.0)
    p_new = jnp.exp(s_new - m)
    l = jnp.sum(p, axis=-1, keepdims=True) + p_new
    o = (_dot(p.astype(BF16), v) + p_new * _pick_row(new, NSA_GROUPS + g)) / l
    o_ref[...] = prev_ref[...] + _sigmoid(gate_ref[:, 2:3]) * o


def _nsa_win_decode(q4, win_past, win_new, tab_t, gates4, prev4):
    b = q4.shape[0]
    inner = 2 * NSA_GROUPS
    lw = win_past.shape[1] // inner
    assert lw == WINDOW

    def grp(bi, g):
        return (bi, g, 0, 0)

    return pl.pallas_call(
        functools.partial(_nsa_win_dec_kernel, lw=lw),
        grid=(b, NSA_GROUPS),
        in_specs=[
            pl.BlockSpec((None, None, NSA_REP, HEAD_DIM), grp),
            pl.BlockSpec((None, lw * inner, HEAD_DIM), lambda bi, g: (bi, 0, 0)),
            pl.BlockSpec((None, inner, HEAD_DIM), lambda bi, g: (bi, 0, 0)),
            pl.BlockSpec((NSA_REP, N_BUCKETS), lambda bi, g: (g, 0)),
            pl.BlockSpec((None, None, NSA_REP, 3), grp),
            pl.BlockSpec((None, None, NSA_REP, HEAD_DIM), grp),
        ],
        out_specs=pl.BlockSpec((None, None, NSA_REP, HEAD_DIM), grp),
        out_shape=jax.ShapeDtypeStruct(q4.shape, F32),
        compiler_params=_cparams(("parallel", "parallel")),
        name="nsa_window_decode",
    )(q4, win_past, win_new, tab_t, gates4, prev4)


def _lower_bound(lbp, layer, axis):
    e = jnp.exp(lbp - jnp.max(lbp, axis=axis, keepdims=True))
    sm = e / jnp.sum(e, axis=axis, keepdims=True)
    idx = _iota(lbp.shape, axis)
    return jnp.sum(jnp.where((idx >= 1) & (idx <= layer), sm, 0.0), axis=axis, keepdims=True)


def _gla_kernel(q_ref, f_ref, i_ref, lbp_ref, s0_ref, ng_ref, o_ref, s_ref, st_ref, *, layer, chunk, n_chunks, hp):
    ti = pl.program_id(2)
    lb = _lower_bound(lbp_ref[...], layer, 0)
    tril = _iota((chunk, chunk), 0) >= _iota((chunk, chunk), 1)
    tril_b = tril.astype(BF16)
    ng = ng_ref[...]

    @pl.when(ti == 0)
    def _():
        for r in range(hp):
            st_ref[r] = s0_ref[r].T

    def heads(x):
        return jnp.stack([x[:, r * HEAD_DIM:(r + 1) * HEAD_DIM] for r in range(hp)], axis=0)

    def bdot(a, b, ca, cb):
        return lax.dot_general(a, b, (((ca,), (cb,)), ((0,), (0,))), preferred_element_type=F32)

    def body(c, _):
        r0 = pl.multiple_of(c * chunk, chunk)
        qv = q_ref[pl.ds(r0, chunk), :]
        q = qv * _sigmoid(qv)
        g = lb + (1.0 - lb) * _sigmoid(f_ref[pl.ds(r0, chunk), :])
        k = 1.0 - g
        b = _dot3_rhs(tril_b, jnp.log(g))
        b_last = b[chunk - 1:chunk, :]
        qe = heads((q * jnp.exp(b)).astype(BF16))
        ke = heads((k * jnp.exp(-b)).astype(BF16))
        kt = heads((k * jnp.exp(b_last - b)).astype(BF16))
        v = heads(i_ref[pl.ds(r0, chunk), :].astype(BF16))
        st = st_ref[...]
        att = jnp.where(tril[None], bdot(qe, ke, 2, 2), 0.0)
        o = bdot(att.astype(BF16), v, 2, 1) + bdot(qe, st.astype(BF16), 2, 2)
        st_ref[...] = heads(jnp.exp(b_last)) * st + bdot(v, kt, 1, 1)
        inv = lax.rsqrt(jnp.mean(o * o, axis=-1, keepdims=True) + RMS_EPS)
        o = o * inv * ng
        o_ref[pl.ds(r0, chunk), :] = jnp.concatenate([o[r] for r in range(hp)], axis=1)
        return 0

    lax.fori_loop(0, n_chunks, body, 0)

    @pl.when(ti == pl.num_programs(2) - 1)
    def _():
        for r in range(hp):
            s_ref[r] = st_ref[r].T


def _gla(q, f, i_in, lbp, s0, norm_g, layer, hp=16, tt=512):
    b, t, w = q.shape
    nh = w // HEAD_DIM
    chunk = math.gcd(t, GLA_CHUNK)
    depth = lbp.shape[0]
    tt = min(tt, t)
    assert nh % hp == 0 and t % tt == 0 and tt % chunk == 0
    tok = pl.BlockSpec((None, tt, hp * HEAD_DIM), lambda bi, h, ti: (bi, ti, h))
    st = pl.BlockSpec((None, hp, HEAD_DIM, HEAD_DIM), lambda bi, h, ti: (bi, h, 0, 0))
    return pl.pallas_call(
        functools.partial(_gla_kernel, layer=layer, chunk=chunk, n_chunks=tt // chunk, hp=hp),
        grid=(b, nh // hp, t // tt),
        in_specs=[tok, tok, tok, pl.BlockSpec((depth, hp * HEAD_DIM), lambda bi, h, ti: (0, h)), st,
                  pl.BlockSpec((1, HEAD_DIM), lambda bi, h, ti: (0, 0))],
        out_specs=[tok, st],
        out_shape=[jax.ShapeDtypeStruct((b, t, w), F32), jax.ShapeDtypeStruct(s0.shape, F32)],
        scratch_shapes=[pltpu.VMEM((hp, HEAD_DIM, HEAD_DIM), F32)],
        compiler_params=_cparams(("arbitrary", "arbitrary", "arbitrary")),
        name="hgrn_gla",
    )(q, f, i_in, lbp, s0, norm_g.reshape(1, HEAD_DIM))


def _gla_dec_kernel(q_ref, f_ref, v_ref, lbp_ref, s0_ref, ng_ref, o_ref, s_ref, *, layer):
    lb = _lower_bound(lbp_ref[...], layer, 0)[0]
    qv = q_ref[...]
    q = qv * _sigmoid(qv)
    g = lb + (1.0 - lb) * _sigmoid(f_ref[...])
    s = g * s0_ref[...] + (1.0 - g) * v_ref[...]
    s_ref[...] = s
    o = jnp.sum(q * s, axis=0, keepdims=True)
    inv = lax.rsqrt(jnp.mean(o * o, axis=-1, keepdims=True) + RMS_EPS)
    o_ref[...] = o * inv * ng_ref[...]


def _gla_decode(q, f, i_in, lbp, s0, norm_g, layer):
    b, nh = s0.shape[:2]
    depth = lbp.shape[0]
    col = pl.BlockSpec((None, None, HEAD_DIM, 1), lambda bi, h: (bi, h, 0, 0))
    row = pl.BlockSpec((None, None, 1, HEAD_DIM), lambda bi, h: (bi, h, 0, 0))
    st = pl.BlockSpec((None, None, HEAD_DIM, HEAD_DIM), lambda bi, h: (bi, h, 0, 0))
    o, s = pl.pallas_call(
        functools.partial(_gla_dec_kernel, layer=layer),
        grid=(b, nh),
        in_specs=[col, col, row, pl.BlockSpec((depth, None, HEAD_DIM, 1), lambda bi, h: (0, h, 0, 0)), st,
                  pl.BlockSpec((1, HEAD_DIM), lambda bi, h: (0, 0))],
        out_specs=[row, st],
        out_shape=[jax.ShapeDtypeStruct((b, nh, 1, HEAD_DIM), F32), jax.ShapeDtypeStruct(s0.shape, F32)],
        compiler_params=_cparams(("parallel", "parallel")),
        name="hgrn_step",
    )(q.reshape(b, nh, HEAD_DIM, 1), f.reshape(b, nh, HEAD_DIM, 1), i_in.reshape(b, nh, 1, HEAD_DIM),
      lbp.reshape(depth, nh, HEAD_DIM, 1), s0, norm_g.reshape(1, HEAD_DIM))
    return o.reshape(b, 1, nh * HEAD_DIM), s


def _as_rows(w):
    if w.shape[1] % 128:
        return jnp.swapaxes(w, 0, 1), True
    return w, False


def _fox_layer(x, bt, norm_g, w_in, b_f, past):
    b, t = bt
    d = x.shape[1]
    nh = d // HEAD_DIM
    wt, tr = _as_rows(w_in)
    assert tr
    q, k, v, z = _norm_proj(x, norm_g, wt, [(n * d, d) for n in range(4)], 512, True)
    (fl,) = _norm_proj(x, norm_g, wt, [(4 * d, nh)], nh, True)
    if past is None:
        q3, k3, v3 = (a.reshape(b, t, d) for a in (q, k, v))
        logf, cum = _fox_prep(fl.reshape(b, t, nh), b_f)
        o = _fox_attn(q3, k3, v3, cum)
    else:
        k_pool, v_pool, lf_pool, page_table = past
        npool, page = k_pool.shape[:2]
        o, logf = _fox_decode(q.reshape(b, nh, HEAD_DIM), k.reshape(b, nh, HEAD_DIM), v.reshape(b, nh, HEAD_DIM),
                              fl.reshape(b, nh, 1), b_f, k_pool.reshape(npool, page * nh, HEAD_DIM),
                              v_pool.reshape(npool, page * nh, HEAD_DIM), jnp.swapaxes(lf_pool, 1, 2), page_table)
    state = (k.reshape(1, b, t, nh, HEAD_DIM), v.reshape(1, b, t, nh, HEAD_DIM), logf.reshape(1, b, t, nh))
    return o.reshape(b * t, d), z, state


def _pad_summaries(summaries, b, n_cmp, n_pad):
    s = summaries.reshape(2, NSA_GROUPS, b, n_cmp, HEAD_DIM)
    return jnp.pad(jnp.swapaxes(s, 1, 2), ((0, 0), (0, 0), (0, 0), (0, n_pad - n_cmp), (0, 0)))


def _nsa_layer(x, bt, norm_g, w_in, cmp_pos, cmp_w1, cmp_w2, t5_bias, past):
    b, t = bt
    d = x.shape[1]
    kvw = NSA_GROUPS * HEAD_DIM
    off_gate = d + 6 * kvw
    n_gate = 3 * (d // HEAD_DIM)
    wt, tr = _as_rows(w_in)
    assert tr
    q, kv4, win = _norm_proj(x, norm_g, wt, [(0, d), (d, 4 * kvw), (d + 4 * kvw, 2 * kvw)], 512, True)
    (gates,) = _norm_proj(x, norm_g, wt[off_gate:off_gate + n_gate], [(0, n_gate)], n_gate, True)
    (z,) = _norm_proj(x, norm_g, wt[off_gate + n_gate:], [(0, d)], 512, True)
    q3, gates3 = q.reshape(b, t, d), gates.reshape(b, t, n_gate)
    inner = NSA_KINDS * NSA_GROUPS
    if past is None:
        kv3, win3 = kv4.reshape(b, t, 4 * kvw), win.reshape(b, t, 2 * kvw)
        n_cmp = t // CMP_BLOCK
        n_pad = -(-n_cmp // 128) * 128
        summaries = _pad_summaries(_compress(kv4, cmp_pos, cmp_w1, cmp_w2), b, n_cmp, n_pad)
        o, mask = _nsa_cmp(q3, summaries[0], summaries[1], gates3, 0, t, False)
        o = _nsa_attn(q3, kv3, 4, 6, _t5_tiles(t5_bias, 128, math.gcd(t, 512)), gates3, o, mask, 1)
        o = _nsa_attn(q3, win3, 0, 2, _t5_tiles(t5_bias, 128, math.gcd(t, 256)), gates3, o, None, 2)
        keep = min(WINDOW, t)
        win_state = win3[:, t - keep:]
    else:
        kv_pool, win_past, page_table = past
        assert t == 1
        npool, page = kv_pool.shape[:2]
        n_pages = page_table.shape[1]
        plen = n_pages * page
        pool = kv_pool.reshape(npool, page * inner, HEAD_DIM)
        n_cmp = plen // CMP_BLOCK
        n_pad = -(-(n_cmp + 1) // 128) * 128
        summaries = _pad_summaries(_compress(_cmp_gather(pool, page_table, page), cmp_pos, cmp_w1, cmp_w2),
                                   b, n_cmp, n_pad)
        rows = SUBLANES
        q_pad = jnp.pad(q3, ((0, 0), (0, rows - t), (0, 0)))
        g_pad = jnp.pad(gates3, ((0, 0), (0, rows - t), (0, 0)))
        o, sel_idx = _nsa_cmp(q_pad, summaries[0], summaries[1], g_pad, plen, plen + t, True)
        o4 = o[:, :1].reshape(b, NSA_GROUPS, NSA_REP, HEAD_DIM)
        sel_idx = sel_idx[:, :, 0, :].reshape(-1)
        q4 = q3.reshape(b, NSA_GROUPS, NSA_REP, HEAD_DIM)
        gates4 = gates3.reshape(b, NSA_GROUPS, NSA_REP, 3)
        tab_t = t5_bias.T
        blocks = pool.reshape(npool * page // CMP_BLOCK, CMP_BLOCK * inner, HEAD_DIM)
        o4 = _nsa_sel_decode(q4, blocks, kv4.reshape(b, inner, HEAD_DIM), tab_t, gates4, o4, sel_idx,
                             page_table, plen, page)
        lw = win_past.shape[1]
        wp = win_past.reshape(b, lw * 2 * NSA_GROUPS, HEAD_DIM)
        wn = win.reshape(b, 2 * NSA_GROUPS, HEAD_DIM)
        o4 = _nsa_win_decode(q4, wp, wn, tab_t, gates4, o4)
        o = o4.reshape(b, t, d)
        keep = min(WINDOW, lw + t)
        win_state = jnp.concatenate([wp, wn], axis=1)[:, (lw + t - keep) * 2 * NSA_GROUPS:]
    state = (kv4.reshape(1, b, t, NSA_KINDS, NSA_GROUPS, HEAD_DIM),
             win_state.reshape(1, b, -1, 2, NSA_GROUPS, HEAD_DIM))
    return o.reshape(b * t, d), z, state


def _hgrn_layer(x, bt, norm_g, w_in, lbp, hgrn_norm_g, layer, s0):
    b, t = bt
    d = x.shape[1]
    q, f, i_in, z = _norm_proj(x, norm_g, w_in, [(n * d, d) for n in range(4)], 512)
    if t == 1:
        o, s_new = _gla_decode(q, f, i_in, lbp, s0, hgrn_norm_g, layer)
    else:
        o, s_new = _gla(q.reshape(b, t, d), f.reshape(b, t, d), i_in.reshape(b, t, d), lbp, s0, hgrn_norm_g, layer)
    return o.reshape(b * t, d), z, (s_new[None],)


def _sb_layer(x, bt, norm_g, w_in, past):
    b, t = bt
    d = x.shape[1]
    nh = d // HEAD_DIM
    q, k, v, z = _norm_proj(x, norm_g, w_in, [(n * d, d) for n in range(4)], 512)
    if past is None:
        o = _sb_attn(q.reshape(b, t, d), k.reshape(b, t, d), v.reshape(b, t, d))
    else:
        k_pool, v_pool, page_table = past
        npool, page = k_pool.shape[:2]
        o = _sb_decode(q.reshape(b, nh, HEAD_DIM), k_pool.reshape(npool, page * nh, HEAD_DIM),
                       v_pool.reshape(npool, page * nh, HEAD_DIM), page_table, page)
    state = (k.reshape(1, b, t, nh, HEAD_DIM), v.reshape(1, b, t, nh, HEAD_DIM))
    return o.reshape(b * t, d), z, state


def kernel(x_prompt, x_sample, cache_fox_k, cache_fox_v, cache_fox_logf, cache_nsa_kv, state_nsa_win, state_hgrn, cache_sb_k, cache_sb_v, page_table, p_prompt, p_sample, norm_g, final_norm_g, ple_gate_w, ple_proj_w, fox_w_in, fox_b_f, fox_w_out, nsa_w_in, nsa_cmp_pos, nsa_cmp_w1, nsa_cmp_w2, t5_bias, nsa_w_out, hgrn_w_in, hgrn_lower_bounds, hgrn_norm_g, hgrn_w_out, sb_w_in, sb_w_out):
    depth = norm_g.shape[0]
    assert depth == 4 and all(a.shape[0] == 1 for a in (fox_w_in, nsa_w_in, hgrn_w_in, sb_w_in))
    w_out = [w[0].astype(BF16) for w in (fox_w_out, nsa_w_out, hgrn_w_out, sb_w_out)]
    w_gate = [ple_gate_w[i].astype(BF16) for i in range(depth)]
    w_proj = [ple_proj_w[i].astype(BF16) for i in range(depth)]

    def trunk(x3, p4, decode):
        b, t, d = x3.shape
        bt = (b, t)
        x = x3.reshape(b * t, d)
        p = p4.reshape(depth, b * t, -1)
        if decode:
            past = ((cache_fox_k[0], cache_fox_v[0], cache_fox_logf[0], page_table),
                    (cache_nsa_kv[0], state_nsa_win[0], page_table),
                    state_hgrn[0],
                    (cache_sb_k[0], cache_sb_v[0], page_table))
        else:
            past = (None, None, jnp.zeros((b,) + state_hgrn.shape[2:], F32), None)
        o, z, st_a = _fox_layer(x, bt, norm_g[0], fox_w_in[0], fox_b_f[0], past[0])
        x = _layer_tail(o, z, x, p[0], w_out[0], w_gate[0], w_proj[0])
        o, z, st_b = _nsa_layer(x, bt, norm_g[1], nsa_w_in[0], nsa_cmp_pos[0], nsa_cmp_w1[0], nsa_cmp_w2[0],
                                t5_bias, past[1])
        x = _layer_tail(o, z, x, p[1], w_out[1], w_gate[1], w_proj[1])
        o, z, st_c = _hgrn_layer(x, bt, norm_g[2], hgrn_w_in[0], hgrn_lower_bounds, hgrn_norm_g[0], 2, past[2])
        x = _layer_tail(o, z, x, p[2], w_out[2], w_gate[2], w_proj[2])
        o, z, st_d = _sb_layer(x, bt, norm_g[3], sb_w_in[0], past[3])
        x = _layer_tail(o, z, x, p[3], w_out[3], w_gate[3], w_proj[3])
        y = _rmsnorm(x, final_norm_g).reshape(b, t, d)
        return y, st_a, st_b, st_c, st_d

    yp, pa, pb, pc, pd = trunk(x_prompt, p_prompt, False)
    ys, sa, sb, sc, sd = trunk(x_sample, p_sample, True)
    return (yp, ys, pa[0], sa[0], pa[1], sa[1], pa[2], sa[2], pb[0], sb[0], pb[1], sb[1],
            pc[0], sc[0], pd[0], sd[0], pd[1], sd[1])
```

```python
import functools
import math

import jax
import jax.numpy as jnp
from jax import lax
from jax.experimental import pallas as pl
from jax.experimental.pallas import tpu as pltpu

F32 = jnp.float32
BF16 = jnp.bfloat16
I32 = jnp.int32

HEAD_DIM = 128
RMS_EPS = 1e-6
NEG_INF = -1e30
FORCE_SCORE = 1e9
CMP_BLOCK = 64
N_SEL = 16
WINDOW = 512
N_BUCKETS = 32
MAX_DISTANCE = 128
GLA_CHUNK = 32
NSA_GROUPS = 2
NSA_REP = 8
NSA_KINDS = 4
SUBLANES = 8
SCALE = HEAD_DIM ** -0.5
LOG2E = 1.0 / math.log(2.0)
MIB = 1 << 20


def _cparams(sem, vmem_mib=None):
    kw = dict(dimension_semantics=sem)
    if vmem_mib is not None:
        kw["vmem_limit_bytes"] = vmem_mib * MIB
    return pltpu.CompilerParams(**kw)


def _iota(shape, axis):
    return lax.broadcasted_iota(I32, shape, axis)


def _dot(a, b):
    return jnp.dot(a, b, preferred_element_type=F32)


def _dot_nt(a, b):
    return lax.dot_general(a, b, (((1,), (1,)), ((), ())), preferred_element_type=F32)


def _dot_tn(a, b):
    return lax.dot_general(a, b, (((0,), (0,)), ((), ())), preferred_element_type=F32)


def _split3(x):
    hi = x.astype(BF16)
    r1 = x - hi.astype(F32)
    mid = r1.astype(BF16)
    lo = (r1 - mid.astype(F32)).astype(BF16)
    return hi, mid, lo


def _dot3(x, m, dot=_dot):
    hi, mid, lo = _split3(x)
    return dot(hi, m) + dot(mid, m) + dot(lo, m)


def _dot3_rhs(m, x, dot=_dot):
    hi, mid, lo = _split3(x)
    return dot(m, hi) + dot(m, mid) + dot(m, lo)


def _log_sigmoid_pair(x):
    sp = jnp.log1p(jnp.exp(-jnp.abs(x)))
    return jnp.minimum(x, 0.0) - sp, jnp.minimum(-x, 0.0) - sp


def _log2_sigmoid_pair(x2):
    sp = jnp.log(1.0 + jnp.exp2(-jnp.abs(x2))) * LOG2E
    pos = jnp.minimum(x2, 0.0) - sp
    return pos, pos - x2


def _sigmoid(x):
    return jax.nn.sigmoid(x)


def _t5_bucket(dist):
    exact = N_BUCKETS // 2
    d = jnp.maximum(dist, 0)
    logd = jnp.log(jnp.maximum(d, 1).astype(F32) * (1.0 / exact))
    large = exact + (logd * ((N_BUCKETS - exact) / math.log(MAX_DISTANCE / exact))).astype(I32)
    return jnp.where(d < exact, d, jnp.minimum(large, N_BUCKETS - 1))


def _div_pow2(x, n):
    assert n & (n - 1) == 0
    return lax.shift_right_arithmetic(x, n.bit_length() - 1)


def _row_tile(m, cap):
    t = min(m, cap)
    assert m % t == 0, (m, t)
    return t


def _norm_proj_kernel(x_ref, g_ref, w_ref, *rest, slab_blocks, transposed):
    outs, u_ref = rest[:-1], rest[-1]
    j = pl.program_id(1)

    @pl.when(j == 0)
    def _():
        x = x_ref[...]
        inv = lax.rsqrt(jnp.mean(x * x, axis=-1, keepdims=True) + RMS_EPS)
        u_ref[...] = (x * inv * g_ref[...]).astype(BF16)

    w = w_ref[...].astype(BF16)
    acc = _dot_nt(u_ref[...], w) if transposed else _dot(u_ref[...], w)
    for (lo, n), o_ref in zip(slab_blocks, outs):
        @pl.when((j >= lo) & (j < lo + n))
        def _(o_ref=o_ref):
            o_ref[...] = acc


def _norm_proj(x, g, w, slabs, tn, transposed=False):
    m, d = x.shape
    tm = _row_tile(m, 1024)
    base = slabs[0][0]
    slab_blocks, pos = [], base
    for off, width in slabs:
        assert off == pos and width % tn == 0 and off % tn == 0, (off, width, tn)
        slab_blocks.append(((off - base) // tn, width // tn))
        pos += width
    nj = (pos - base) // tn
    b0 = base // tn

    def out_map(lo, n):
        return lambda i, j: (i, jnp.minimum(jnp.maximum(j - lo, 0), n - 1))

    if transposed:
        w_spec = pl.BlockSpec((tn, d), lambda i, j: (b0 + j, 0))
    else:
        w_spec = pl.BlockSpec((d, tn), lambda i, j: (0, b0 + j))
    return pl.pallas_call(
        functools.partial(_norm_proj_kernel, slab_blocks=tuple(slab_blocks), transposed=transposed),
        grid=(m // tm, nj),
        in_specs=[pl.BlockSpec((tm, d), lambda i, j: (i, 0)), pl.BlockSpec((1, d), lambda i, j: (0, 0)), w_spec],
        out_specs=[pl.BlockSpec((tm, tn), out_map(lo, n)) for lo, n in slab_blocks],
        out_shape=[jax.ShapeDtypeStruct((m, width), F32) for _, width in slabs],
        scratch_shapes=[pltpu.VMEM((tm, d), BF16)],
        compiler_params=_cparams(("parallel", "arbitrary"), 56),
        name="norm_proj",
    )(x, g.reshape(1, d), w)


def _layer_tail_kernel(o_ref, z_ref, x_ref, p_ref, wo_ref, wg_ref, wp_ref, *rest):
    z = z_ref[...]
    a = (o_ref[...] * (z * _sigmoid(z))).astype(BF16)
    h = x_ref[...] + _dot(a, wo_ref[...])
    gate = _sigmoid(_dot(h.astype(BF16), wg_ref[...]))
    y = h + gate * _dot(p_ref[...].astype(BF16), wp_ref[...])
    if len(rest) == 2:
        g_ref, out_ref = rest
        y = y * lax.rsqrt(jnp.mean(y * y, axis=-1, keepdims=True) + RMS_EPS) * g_ref[...]
    else:
        (out_ref,) = rest
    out_ref[...] = y


def _layer_tail(o, z, x, p, wo, wg, wp, final_g=None):
    m, d = o.shape
    pd = p.shape[1]
    tm = _row_tile(m, 256)
    row = pl.BlockSpec((tm, d), lambda i: (i, 0))

    def resident(shape):
        return pl.BlockSpec(shape, lambda i: (0, 0), pipeline_mode=pl.Buffered(1))

    in_specs = [row, row, row, pl.BlockSpec((tm, pd), lambda i: (i, 0)),
                resident((d, d)), resident((d, d)), resident((pd, d))]
    args = [o, z, x, p, wo, wg, wp]
    if final_g is not None:
        in_specs.append(pl.BlockSpec((1, d), lambda i: (0, 0)))
        args.append(final_g.reshape(1, d))
    return pl.pallas_call(
        _layer_tail_kernel,
        grid=(m // tm,),
        in_specs=in_specs,
        out_specs=row,
        out_shape=jax.ShapeDtypeStruct((m, d), F32),
        compiler_params=_cparams(("parallel",), 48),
        name="layer_tail",
    )(*args)


def _fox_prep_kernel(x_ref, bf_ref, logf_ref, cum_ref, *, n_chunks):
    tril = (_iota((128, 128), 0) >= _iota((128, 128), 1)).astype(BF16)
    carry = jnp.zeros((1, x_ref.shape[1]), F32)
    for c in range(n_chunks):
        sl = slice(c * 128, (c + 1) * 128)
        lf, _ = _log_sigmoid_pair(x_ref[sl, :] + bf_ref[...])
        logf_ref[sl, :] = lf
        cum = _dot3_rhs(tril, lf) + carry
        cum_ref[sl, :] = cum
        carry = cum[127:128, :]


def _fox_prep(logits, b_f):
    b, t, h = logits.shape
    assert t % 128 == 0
    spec = pl.BlockSpec((None, t, h), lambda i: (i, 0, 0))
    return pl.pallas_call(
        functools.partial(_fox_prep_kernel, n_chunks=t // 128),
        grid=(b,),
        in_specs=[spec, pl.BlockSpec((1, h), lambda i: (0, 0))],
        out_specs=[spec, spec],
        out_shape=[jax.ShapeDtypeStruct((b, t, h), F32)] * 2,
        compiler_params=_cparams(("parallel",)),
        name="fox_prep",
    )(logits, b_f.reshape(1, h))


def _pick_col(x, c):
    return jnp.sum(jnp.where(_iota(x.shape, 1) == c, x, 0.0), axis=1, keepdims=True)


def _bias_lanes(col, ones_first):
    rows = col.shape[0]
    hi, mid, lo = (p.astype(F32) for p in _split3(col))
    lane = _iota((rows, HEAD_DIM), 1)
    one, val = (0, 3) if ones_first else (3, 0)
    x = jnp.where(lane == val, hi, jnp.where(lane == val + 1, mid, jnp.where(lane == val + 2, lo, 0.0)))
    x = jnp.where((lane >= one) & (lane < one + 3), 1.0, x)
    return x.astype(BF16)


def _fox_attn_kernel(q_ref, k_ref, v_ref, cq_ref, ck_ref, o_ref, ka_ref, vb_ref, *, tq):
    h = pl.program_id(1)
    qi = pl.program_id(2)
    t = k_ref.shape[0]

    @pl.when(qi == 0)
    def _():
        for c in range(t // tq):
            sl = slice(c * tq, (c + 1) * tq)
            ka_ref[sl, :HEAD_DIM] = k_ref[sl, :].astype(BF16)
            ka_ref[sl, HEAD_DIM:] = _bias_lanes(_pick_col(ck_ref[sl, :], h) * -LOG2E, True)
            vb_ref[sl, :] = v_ref[sl, :].astype(BF16)

    qa = jnp.concatenate([(q_ref[...] * (SCALE * LOG2E)).astype(BF16),
                          _bias_lanes(_pick_col(cq_ref[...], h) * LOG2E, False)], axis=1)
    tri = _iota((tq, tq), 1) <= _iota((tq, tq), 0)

    def step(kc, carry, diagonal):
        m, l, acc = carry
        ks = pl.multiple_of(kc * tq, tq)
        s = _dot_nt(qa, ka_ref[pl.ds(ks, tq), :])
        if diagonal:
            s = jnp.where(tri, s, NEG_INF)
        m_new = jnp.maximum(m, jnp.max(s, axis=-1, keepdims=True))
        p = jnp.exp2(s - m_new)
        alpha = jnp.exp2(m - m_new)
        l = alpha * l + jnp.sum(p, axis=-1, keepdims=True)
        acc = alpha * acc + _dot(p.astype(BF16), vb_ref[pl.ds(ks, tq), :])
        return m_new, l, acc

    init = (jnp.full((tq, 1), NEG_INF, F32), jnp.zeros((tq, 1), F32), jnp.zeros((tq, HEAD_DIM), F32))
    carry = lax.fori_loop(0, qi, lambda kc, c: step(kc, c, False), init)
    _, l, acc = step(qi, carry, True)
    o_ref[...] = acc / l


def _fox_attn(q, k, v, cum, tq=512):
    b, t, w = q.shape
    h = w // HEAD_DIM
    tq = min(tq, t)
    assert t % tq == 0
    return pl.pallas_call(
        functools.partial(_fox_attn_kernel, tq=tq),
        grid=(b, h, t // tq),
        in_specs=[
            pl.BlockSpec((None, tq, HEAD_DIM), lambda bi, hi, qi: (bi, qi, hi)),
            pl.BlockSpec((None, t, HEAD_DIM), lambda bi, hi, qi: (bi, 0, hi)),
            pl.BlockSpec((None, t, HEAD_DIM), lambda bi, hi, qi: (bi, 0, hi)),
            pl.BlockSpec((None, tq, h), lambda bi, hi, qi: (bi, qi, 0)),
            pl.BlockSpec((None, t, h), lambda bi, hi, qi: (bi, 0, 0)),
        ],
        out_specs=pl.BlockSpec((None, tq, HEAD_DIM), lambda bi, hi, qi: (bi, qi, hi)),
        out_shape=jax.ShapeDtypeStruct((b, t, w), F32),
        scratch_shapes=[pltpu.VMEM((t, 2 * HEAD_DIM), BF16), pltpu.VMEM((t, HEAD_DIM), BF16)],
        compiler_params=_cparams(("arbitrary", "arbitrary", "arbitrary")),
        name="fox_attn",
    )(q, k, v, cum, cum)


def _sb_attn_kernel(q_ref, k_ref, v_ref, o_ref, kb_ref, vb_ref, *, tq):
    qi = pl.program_id(2)
    t = k_ref.shape[0]
    sub = HEAD_DIM

    @pl.when(qi == 0)
    def _():
        for c in range(t // tq):
            sl = slice(c * tq, (c + 1) * tq)
            kb_ref[sl, :] = k_ref[sl, :].astype(BF16)
            vb_ref[sl, :] = v_ref[sl, :].astype(BF16)

    q = (q_ref[...] * (SCALE * LOG2E)).astype(BF16)
    later = (_iota((sub, sub), 0) > _iota((sub, sub), 1)).astype(BF16)
    scan = jnp.concatenate([later, jnp.ones((sub, sub), BF16)], axis=1)
    scan = jnp.concatenate([scan, scan], axis=0)
    before = _iota((tq, tq), 1) < _iota((tq, tq), 0)

    def step(kc, carry, diagonal):
        tail, acc = carry
        ks = pl.multiple_of(kc * tq, tq)
        x = _dot_nt(q, kb_ref[pl.ds(ks, tq), :])
        ls_pos, ls_neg = _log2_sigmoid_pair(x)
        if diagonal:
            ls_neg = jnp.where(before, ls_neg, 0.0)
        ws = [None] * (tq // sub)
        for j in reversed(range(tq // sub)):
            sl = slice(j * sub, (j + 1) * sub)
            stay = ls_neg[:, sl]
            hi = stay.astype(BF16)
            lo = (stay - hi.astype(F32)).astype(BF16)
            sums = _dot(jnp.concatenate([hi, lo], axis=1), scan)
            w = jnp.exp2(ls_pos[:, sl] + sums[:, :sub] + tail)
            if diagonal:
                w = jnp.where(before[:, sl], w, 0.0)
            ws[j] = w.astype(BF16)
            tail = tail + sums[:, sub:]
        acc = acc + _dot(jnp.concatenate(ws, axis=1), vb_ref[pl.ds(ks, tq), :])
        return tail, acc

    carry = step(qi, (jnp.zeros((tq, sub), F32), jnp.zeros((tq, HEAD_DIM), F32)), True)
    _, acc = lax.fori_loop(0, qi, lambda i, c: step(qi - 1 - i, c, False), carry)
    o_ref[...] = acc


def _sb_attn(q, k, v, tq=512):
    b, t, w = q.shape
    h = w // HEAD_DIM
    tq = min(tq, t)
    assert t % tq == 0
    return pl.pallas_call(
        functools.partial(_sb_attn_kernel, tq=tq),
        grid=(b, h, t // tq),
        in_specs=[
            pl.BlockSpec((None, tq, HEAD_DIM), lambda bi, hi, qi: (bi, qi, hi)),
            pl.BlockSpec((None, t, HEAD_DIM), lambda bi, hi, qi: (bi, 0, hi)),
            pl.BlockSpec((None, t, HEAD_DIM), lambda bi, hi, qi: (bi, 0, hi)),
        ],
        out_specs=pl.BlockSpec((None, tq, HEAD_DIM), lambda bi, hi, qi: (bi, qi, hi)),
        out_shape=jax.ShapeDtypeStruct((b, t, w), F32),
        scratch_shapes=[pltpu.VMEM((t, HEAD_DIM), BF16), pltpu.VMEM((t, HEAD_DIM), BF16)],
        compiler_params=_cparams(("arbitrary", "arbitrary", "arbitrary")),
        name="sb_attn",
    )(q, k, v)


FOX_DEC_PAGES = 2
SB_DEC_PAGES = 4


def _page_scores(k_ref, q, nh, page):
    y = (k_ref[...].reshape(page, nh, HEAD_DIM) * q[None]).astype(BF16)
    s = _dot(y.reshape(page * nh, HEAD_DIM), jnp.ones((HEAD_DIM, HEAD_DIM), BF16))
    return s.reshape(page, nh, HEAD_DIM)


def _fox_dec_kernel(pt_ref, q_ref, kn_ref, vn_ref, fl_ref, bf_ref, *refs, pp):
    k_refs, v_refs, lft_refs = refs[:pp], refs[pp:2 * pp], refs[2 * pp:3 * pp]
    o_ref, lfo_ref, m_ref, l_ref, c_ref, acc_ref = refs[3 * pp:]
    p = pl.program_id(1)
    nh, page = lft_refs[0].shape
    q = q_ref[...] * (SCALE * LOG2E)

    @pl.when(p == 0)
    def _():
        s_new = jnp.sum(q * kn_ref[...], axis=-1, keepdims=True)
        m_ref[...] = jnp.broadcast_to(s_new, m_ref.shape)
        l_ref[...] = jnp.ones_like(l_ref)
        acc_ref[...] = vn_ref[...]
        lf_new, _ = _log_sigmoid_pair(fl_ref[...] + bf_ref[...])
        lfo_ref[...] = lf_new
        c_ref[...] = lf_new

    later = (_iota((page, page), 0) > _iota((page, page), 1)).astype(BF16)
    for i in range(pp):
        lft = lft_refs[i][...]
        decay = (c_ref[...] + _dot3(lft, later)) * LOG2E
        decay = jnp.stack([jnp.broadcast_to(decay[:, r:r + 1], (nh, HEAD_DIM)) for r in range(page)], axis=0)
        s = _page_scores(k_refs[i], q, nh, page) + decay
        m_old = m_ref[...]
        m_new = jnp.maximum(m_old, jnp.max(s, axis=0))
        pr = jnp.exp2(s - m_new[None])
        alpha = jnp.exp2(m_old - m_new)
        l_ref[...] = alpha * l_ref[...] + jnp.sum(pr, axis=0)
        acc_ref[...] = alpha * acc_ref[...] + jnp.sum(pr * v_refs[i][...].reshape(page, nh, HEAD_DIM), axis=0)
        m_ref[...] = m_new
        c_ref[...] = c_ref[...] + jnp.sum(lft, axis=1, keepdims=True)

    @pl.when(p == pl.num_programs(1) - 1)
    def _():
        o_ref[...] = acc_ref[...] / l_ref[...]


def _fox_decode(q, k_new, v_new, fl, b_f, k_pool, v_pool, lf_pool_t, page_table):
    b, nh, _ = q.shape
    n_pages = page_table.shape[1]
    page = lf_pool_t.shape[2]
    pp = math.gcd(FOX_DEC_PAGES, n_pages)

    def row(bi, p, pt):
        return (bi, 0, 0)

    def pg(i):
        return lambda bi, p, pt: (pt[bi, n_pages - 1 - (p * pp + i)], 0, 0)

    head = pl.BlockSpec((None, nh, HEAD_DIM), row)
    stat = pltpu.VMEM((nh, HEAD_DIM), F32)
    grid_spec = pltpu.PrefetchScalarGridSpec(
        num_scalar_prefetch=1,
        grid=(b, n_pages // pp),
        in_specs=[head, head, head, pl.BlockSpec((None, nh, 1), row), pl.BlockSpec((nh, 1), lambda bi, p, pt: (0, 0))]
        + [pl.BlockSpec((None, page * nh, HEAD_DIM), pg(i)) for i in range(pp)] * 2
        + [pl.BlockSpec((None, nh, page), pg(i)) for i in range(pp)],
        out_specs=[head, pl.BlockSpec((None, nh, 1), row)],
        scratch_shapes=[stat, stat, pltpu.VMEM((nh, 1), F32), stat],
    )
    return pl.pallas_call(
        functools.partial(_fox_dec_kernel, pp=pp),
        grid_spec=grid_spec,
        out_shape=[jax.ShapeDtypeStruct((b, nh, HEAD_DIM), F32), jax.ShapeDtypeStruct((b, nh, 1), F32)],
        compiler_params=_cparams(("arbitrary", "arbitrary")),
        name="fox_decode",
    )(page_table, q, k_new, v_new, fl, b_f.reshape(nh, 1), *([k_pool] * pp), *([v_pool] * pp), *([lf_pool_t] * pp))


def _sb_dec_kernel(pt_ref, q_ref, *refs, pp, page):
    k_refs, v_refs = refs[:pp], refs[pp:2 * pp]
    o_ref, c_ref, acc_ref = refs[2 * pp:]
    p = pl.program_id(1)
    nh = q_ref.shape[0]
    q = q_ref[...] * (SCALE * LOG2E)

    @pl.when(p == 0)
    def _():
        c_ref[...] = jnp.zeros_like(c_ref)
        acc_ref[...] = jnp.zeros_like(acc_ref)

    for i in range(pp):
        ls_pos, ls_neg = _log2_sigmoid_pair(_page_scores(k_refs[i], q, nh, page))
        tail = c_ref[...]
        ws = [None] * page
        for r in reversed(range(page)):
            ws[r] = jnp.exp2(ls_pos[r] + tail)
            tail = tail + ls_neg[r]
        acc_ref[...] = acc_ref[...] + jnp.sum(jnp.stack(ws, axis=0) * v_refs[i][...].reshape(page, nh, HEAD_DIM), axis=0)
        c_ref[...] = tail

    @pl.when(p == pl.num_programs(1) - 1)
    def _():
        o_ref[...] = acc_ref[...]


def _sb_decode(q, k_pool, v_pool, page_table, page):
    b, nh, _ = q.shape
    n_pages = page_table.shape[1]
    pp = math.gcd(SB_DEC_PAGES, n_pages)

    def row(bi, p, pt):
        return (bi, 0, 0)

    def pg(i):
        return lambda bi, p, pt: (pt[bi, n_pages - 1 - (p * pp + i)], 0, 0)

    head = pl.BlockSpec((None, nh, HEAD_DIM), row)
    stat = pltpu.VMEM((nh, HEAD_DIM), F32)
    grid_spec = pltpu.PrefetchScalarGridSpec(
        num_scalar_prefetch=1,
        grid=(b, n_pages // pp),
        in_specs=[head] + [pl.BlockSpec((None, page * nh, HEAD_DIM), pg(i)) for i in range(pp)] * 2,
        out_specs=head,
        scratch_shapes=[stat, stat],
    )
    return pl.pallas_call(
        functools.partial(_sb_dec_kernel, pp=pp, page=page),
        grid_spec=grid_spec,
        out_shape=jax.ShapeDtypeStruct((b, nh, HEAD_DIM), F32),
        compiler_params=_cparams(("arbitrary", "arbitrary")),
        name="sb_decode",
    )(page_table, q, *([k_pool] * pp), *([v_pool] * pp))


def _cmp_gather_kernel(pt_ref, *refs, inner, page):
    x_refs, o_ref = refs[:-1], refs[-1]
    for i, x_ref in enumerate(x_refs):
        for c in range(2 * NSA_GROUPS):
            o_ref[c, i * page:(i + 1) * page, :] = x_ref[pl.ds(c, page, stride=inner), :]


def _cmp_gather(pool, page_table, page, pp=8):
    b, n_pages = page_table.shape
    inner = pool.shape[1] // page
    pp = math.gcd(pp, n_pages)
    steps = n_pages // pp

    def page_spec(i):
        return pl.BlockSpec((None, page * inner, HEAD_DIM), lambda bi, p, pt: (pt[bi, p * pp + i], 0, 0))

    grid_spec = pltpu.PrefetchScalarGridSpec(
        num_scalar_prefetch=1,
        grid=(b, steps),
        in_specs=[page_spec(i) for i in range(pp)],
        out_specs=pl.BlockSpec((2 * NSA_GROUPS, pp * page, HEAD_DIM), lambda bi, p, pt: (0, bi * steps + p, 0)),
    )
    return pl.pallas_call(
        functools.partial(_cmp_gather_kernel, inner=inner, page=page),
        grid_spec=grid_spec,
        out_shape=jax.ShapeDtypeStruct((2 * NSA_GROUPS, b * n_pages * page, HEAD_DIM), F32),
        compiler_params=_cparams(("parallel", "arbitrary")),
        name="nsa_cmp_gather",
    )(page_table, *([pool] * pp))


def _compress_kernel(x_ref, pos_ref, w1_ref, w2_ref, o_ref, *, rb):
    hidden = w1_ref.shape[2]
    hid = jnp.zeros((rb, hidden), F32)
    for c in range(0, CMP_BLOCK, 2):
        xc = [(x_ref[pl.ds(c + i, rb, stride=CMP_BLOCK), :] + pos_ref[c + i:c + i + 1, :]).astype(BF16) for i in (0, 1)]
        w = w1_ref[c:c + 2].reshape(2 * HEAD_DIM, hidden).astype(BF16)
        hid = hid + _dot(jnp.concatenate(xc, axis=1), w)
    hid = hid * _sigmoid(hid)
    o_ref[...] = _dot(hid.astype(BF16), w2_ref[...].astype(BF16))


def _compress(x, pos, w1, w2):
    r = x.shape[-2]
    nb = r // CMP_BLOCK
    rb = _row_tile(nb, 256)
    hidden = w1.shape[-1]
    w1v = w1.reshape(2, CMP_BLOCK, HEAD_DIM, hidden)
    if x.ndim == 2:
        x_spec = pl.BlockSpec((rb * CMP_BLOCK, HEAD_DIM), lambda c, i: (i, c))
    else:
        x_spec = pl.BlockSpec((None, rb * CMP_BLOCK, HEAD_DIM), lambda c, i: (c, i, 0))
    return pl.pallas_call(
        functools.partial(_compress_kernel, rb=rb),
        grid=(2 * NSA_GROUPS, nb // rb),
        in_specs=[
            x_spec,
            pl.BlockSpec((None, CMP_BLOCK, HEAD_DIM), lambda c, i: (c // NSA_GROUPS, 0, 0)),
            pl.BlockSpec((None, CMP_BLOCK, HEAD_DIM, hidden), lambda c, i: (c // NSA_GROUPS, 0, 0, 0)),
            pl.BlockSpec((None, hidden, HEAD_DIM), lambda c, i: (c // NSA_GROUPS, 0, 0)),
        ],
        out_specs=pl.BlockSpec((None, rb, HEAD_DIM), lambda c, i: (c, i, 0)),
        out_shape=jax.ShapeDtypeStruct((2 * NSA_GROUPS, nb, HEAD_DIM), F32),
        compiler_params=_cparams(("parallel", "arbitrary"), 48),
        name="nsa_compress",
    )(x, pos, w1v, w2)


def _gate_col(gl, col):
    lane = _iota(gl.shape, 1)
    return _sigmoid(jnp.sum(jnp.where(lane == col, gl, 0.0), axis=-1, keepdims=True))


def _nsa_cmp_kernel(q_ref, kc_ref, vc_ref, gate_ref, o_ref, sel_ref, *, past, tq, n_cmp, n_blk, t_keys, decode):
    g = pl.program_id(1)
    qi = pl.program_id(2)
    n_pad = kc_ref.shape[0]
    kc = kc_ref[...].astype(BF16)
    vc = vc_ref[...].astype(BF16)
    qpos = past + qi * tq + _iota((tq, 1), 0)
    blk = _iota((tq, n_pad), 1)
    visible = ((blk + 1) * CMP_BLOCK - 1 <= qpos) & (blk < n_cmp)
    gl = gate_ref[...]
    q_all = jnp.concatenate([q_ref[:, r * HEAD_DIM:(r + 1) * HEAD_DIM].astype(BF16) for r in range(NSA_REP)], axis=0)
    s = (_dot_nt(q_all, kc) * SCALE).reshape(NSA_REP, tq, n_pad)
    s = jnp.where(visible[None], s, NEG_INF)
    e = jnp.exp(s - jnp.max(s, axis=-1, keepdims=True))
    pr = jnp.where(visible[None], e / jnp.sum(e, axis=-1, keepdims=True), 0.0)
    imp = jnp.sum(pr, axis=0)
    gate = jnp.concatenate([_gate_col(gl, (g * NSA_REP + r) * 3) for r in range(NSA_REP)], axis=0)
    o = gate * _dot(pr.reshape(NSA_REP * tq, n_pad).astype(BF16), vc)
    o_ref[...] = jnp.concatenate([o[r * tq:(r + 1) * tq] for r in range(NSA_REP)], axis=1)

    cur = _div_pow2(qpos, CMP_BLOCK)
    forced = (blk == 0) | (blk == cur) | (blk == cur - 1)
    score = jnp.where(blk > cur, NEG_INF, jnp.where(forced, FORCE_SCORE, imp))
    if decode:
        row = score[0:1, :]
        col = jnp.sum(jnp.where(_iota((n_pad, n_pad), 0) == _iota((n_pad, n_pad), 1),
                                jnp.broadcast_to(row, (n_pad, n_pad)), 0.0), axis=1, keepdims=True)
        i_blk, j_blk = _iota((n_pad, n_pad), 0), _iota((n_pad, n_pad), 1)
        first = ((col > row) | ((col == row) & (i_blk < j_blk))) & (i_blk < n_blk)
        rank = jnp.sum(jnp.where(first, 1.0, 0.0), axis=0, keepdims=True)
        blk_f = blk[0:1, :].astype(F32)
        lane = _iota((1, N_SEL), 1)
        idx = jnp.zeros((1, N_SEL), F32)
        for k in range(N_SEL):
            ik = jnp.sum(jnp.where(rank == k, blk_f, 0.0), axis=1, keepdims=True)
            idx = jnp.where(lane == k, ik, idx)
        sel_ref[...] = jnp.broadcast_to(idx, sel_ref.shape).astype(I32)
    else:
        rank = jnp.zeros((tq, n_pad), F32)
        for i in range(n_blk):
            col = jnp.sum(jnp.where(blk == i, score, 0.0), axis=1, keepdims=True)
            first = (col > score) | ((col == score) & (i < blk))
            rank = rank + jnp.where(first, 1.0, 0.0)
        sel = jnp.where(rank < N_SEL, 1.0, 0.0).astype(BF16)
        expand = (_div_pow2(_iota((n_pad, t_keys), 1), CMP_BLOCK) == _iota((n_pad, t_keys), 0)).astype(BF16)
        sel_ref[...] = _dot(sel, expand).astype(BF16)


def _nsa_cmp(q, kc, vc, gates, past, n_keys, decode):
    b, t, w = q.shape
    n_pad = kc.shape[2]
    n_cmp = n_keys // CMP_BLOCK
    n_blk = -(-n_keys // CMP_BLOCK)
    assert N_SEL <= n_blk <= n_pad
    tq = min(t, 128)
    gw = w // NSA_GROUPS
    if decode:
        sel_shape, sel_block, sel_dtype = (b, NSA_GROUPS, t, N_SEL), (None, None, tq, N_SEL), I32
    else:
        sel_shape, sel_block, sel_dtype = (b, NSA_GROUPS, t, t), (None, None, tq, t), BF16
    return pl.pallas_call(
        functools.partial(_nsa_cmp_kernel, past=past, tq=tq, n_cmp=n_cmp, n_blk=n_blk, t_keys=t, decode=decode),
        grid=(b, NSA_GROUPS, t // tq),
        in_specs=[
            pl.BlockSpec((None, tq, gw), lambda bi, g, qi: (bi, qi, g)),
            pl.BlockSpec((None, None, n_pad, HEAD_DIM), lambda bi, g, qi: (bi, g, 0, 0)),
            pl.BlockSpec((None, None, n_pad, HEAD_DIM), lambda bi, g, qi: (bi, g, 0, 0)),
            pl.BlockSpec((None, tq, gates.shape[-1]), lambda bi, g, qi: (bi, qi, 0)),
        ],
        out_specs=[
            pl.BlockSpec((None, tq, gw), lambda bi, g, qi: (bi, qi, g)),
            pl.BlockSpec(sel_block, lambda bi, g, qi: (bi, g, qi, 0)),
        ],
        out_shape=[jax.ShapeDtypeStruct((b, t, w), F32), jax.ShapeDtypeStruct(sel_shape, sel_dtype)],
        compiler_params=_cparams(("parallel", "parallel", "arbitrary")),
        name="nsa_cmp_select",
    )(q, kc, vc, gates)


def _t5_tiles_kernel(tab_ref, o_ref, *, tq, tk, n_tiles):
    h = pl.program_id(0)
    d0 = _iota((tq, tk), 0) - _iota((tq, tk), 1)
    for t in range(n_tiles):
        bucket = _t5_bucket(d0 + t * tq)
        acc = jnp.zeros((tq, tk), F32)
        for n in range(N_BUCKETS):
            acc = jnp.where(bucket == n, tab_ref[n, h], acc)
        o_ref[t] = acc * LOG2E


def _t5_tiles(t5_bias, tq, tk):
    nh = t5_bias.shape[1]
    n_tiles = -(-(MAX_DISTANCE + tk - 1) // tq) + 1
    return pl.pallas_call(
        functools.partial(_t5_tiles_kernel, tq=tq, tk=tk, n_tiles=n_tiles),
        grid=(nh,),
        in_specs=[pl.BlockSpec(memory_space=pltpu.SMEM)],
        out_specs=pl.BlockSpec((None, n_tiles, tq, tk), lambda h: (h, 0, 0, 0)),
        out_shape=jax.ShapeDtypeStruct((nh, n_tiles, tq, tk), F32),
        compiler_params=_cparams(("arbitrary",)),
        name="t5_tiles",
    )(t5_bias)


def _nsa_attn_kernel(*refs, tq, windowed, branch):
    if windowed:
        q_ref, k_ref, v_ref, tiles_ref, gate_ref, prev_ref, o_ref, kb_ref, vb_ref = refs
        mask_ref = None
    else:
        q_ref, k_ref, v_ref, tiles_ref, gate_ref, prev_ref, mask_ref, o_ref, kb_ref, vb_ref = refs
    g = pl.program_id(1)
    qi = pl.program_id(2)
    t = k_ref.shape[0]

    @pl.when(qi == 0)
    def _():
        for c in range(t // tq):
            sl = slice(c * tq, (c + 1) * tq)
            kb_ref[sl, :] = k_ref[sl, :].astype(BF16)
            vb_ref[sl, :] = v_ref[sl, :].astype(BF16)

    rows = NSA_REP * tq
    q_all = jnp.concatenate([(q_ref[:, r * HEAD_DIM:(r + 1) * HEAD_DIM] * (SCALE * LOG2E)).astype(BF16)
                             for r in range(NSA_REP)], axis=0)
    tk = tiles_ref.shape[-1]
    per = tk // tq
    far = tiles_ref.shape[1] - 1
    dist0 = _iota((tq, tk), 0) - _iota((tq, tk), 1)

    def body(kc, carry):
        m, l, acc = carry
        ks = pl.multiple_of(kc * tk, tk)
        off = qi - kc * per
        dist = dist0 + off * tq
        ok = dist >= 0
        if windowed:
            ok = ok & (dist < WINDOW)
        else:
            ok = ok & (mask_ref[:, pl.ds(ks, tk)].astype(F32) > 0.5)
        s = _dot_nt(q_all, kb_ref[pl.ds(ks, tk), :]).reshape(NSA_REP, tq, tk)
        s = jnp.where(ok[None], s + tiles_ref[:, jnp.minimum(off, far)], NEG_INF).reshape(rows, tk)
        m_new = jnp.maximum(m, jnp.max(s, axis=-1, keepdims=True))
        p = jnp.exp2(s - m_new)
        if windowed:
            p = jnp.where(s > 0.5 * NEG_INF, p, 0.0)
        alpha = jnp.exp2(m - m_new)
        l = alpha * l + jnp.sum(p, axis=-1, keepdims=True)
        acc = alpha * acc + _dot(p.astype(BF16), vb_ref[pl.ds(ks, tk), :])
        return m_new, l, acc

    lo = jnp.maximum(lax.shift_right_arithmetic(qi * tq - (WINDOW - 1), tk.bit_length() - 1), 0) if windowed else 0
    init = (jnp.full((rows, 1), NEG_INF, F32), jnp.zeros((rows, 1), F32), jnp.zeros((rows, HEAD_DIM), F32))
    _, l, acc = lax.fori_loop(lo, qi // per + 1, body, init)
    gl = gate_ref[...]
    gate = jnp.concatenate([_gate_col(gl, (g * NSA_REP + r) * 3 + branch) for r in range(NSA_REP)], axis=0)
    o = gate * (acc / l)
    o_ref[...] = prev_ref[...] + jnp.concatenate([o[r * tq:(r + 1) * tq] for r in range(NSA_REP)], axis=1)


def _nsa_attn(q, kv, k_blk, v_blk, tiles, gates, prev, mask, branch):
    b, t, w = q.shape
    gw = w // NSA_GROUPS
    n_tiles, tq, tk = tiles.shape[1:]
    assert t % tk == 0 and tk % tq == 0 and tk & (tk - 1) == 0
    windowed = mask is None
    in_specs = [
        pl.BlockSpec((None, tq, gw), lambda bi, g, qi: (bi, qi, g)),
        pl.BlockSpec((None, t, HEAD_DIM), lambda bi, g, qi: (bi, 0, k_blk + g)),
        pl.BlockSpec((None, t, HEAD_DIM), lambda bi, g, qi: (bi, 0, v_blk + g)),
        pl.BlockSpec((NSA_REP, n_tiles, tq, tk), lambda bi, g, qi: (g, 0, 0, 0)),
        pl.BlockSpec((None, tq, gates.shape[-1]), lambda bi, g, qi: (bi, qi, 0)),
        pl.BlockSpec((None, tq, gw), lambda bi, g, qi: (bi, qi, g)),
    ]
    args = [q, kv, kv, tiles, gates, prev]
    if not windowed:
        in_specs.append(pl.BlockSpec((None, None, tq, t), lambda bi, g, qi: (bi, g, qi, 0)))
        args.append(mask)
    return pl.pallas_call(
        functools.partial(_nsa_attn_kernel, tq=tq, windowed=windowed, branch=branch),
        grid=(b, NSA_GROUPS, t // tq),
        in_specs=in_specs,
        out_specs=pl.BlockSpec((None, tq, gw), lambda bi, g, qi: (bi, qi, g)),
        out_shape=jax.ShapeDtypeStruct((b, t, w), F32),
        scratch_shapes=[pltpu.VMEM((t, HEAD_DIM), BF16), pltpu.VMEM((t, HEAD_DIM), BF16)],
        compiler_params=_cparams(("arbitrary", "arbitrary", "arbitrary")),
        name="nsa_window" if windowed else "nsa_select",
    )(*args)


def _t5_rows(tab_t, dist):
    n = dist.shape[1]
    onehot = (_iota((N_BUCKETS, n), 0) == _t5_bucket(dist)).astype(BF16)
    return _dot3(tab_t, onehot)


def _pick_row(x, r):
    return jnp.sum(jnp.where(_iota(x.shape, 0) == r, x, 0.0), axis=0, keepdims=True)


def _nsa_sel_dec_kernel(pt_ref, idx_ref, q_ref, *refs, past, n_past_blk, inner):
    kv_refs = refs[:N_SEL]
    new_ref, tab_ref, gate_ref, prev_ref, o_ref = refs[N_SEL:]
    bi, g = pl.program_id(0), pl.program_id(1)
    k_row = 2 * NSA_GROUPS + g
    v_row = 3 * NSA_GROUPS + g
    new = new_ref[...]
    k_new = jnp.broadcast_to(_pick_row(new, k_row), (CMP_BLOCK, HEAD_DIM))
    v_new = jnp.broadcast_to(_pick_row(new, v_row), (CMP_BLOCK, HEAD_DIM))
    ks, vs, dists = [], [], []
    for j in range(N_SEL):
        blk = idx_ref[(bi * NSA_GROUPS + g) * N_SEL + j]
        is_new = blk >= n_past_blk
        ks.append(jnp.where(is_new, k_new, kv_refs[j][pl.ds(k_row, CMP_BLOCK, stride=inner), :]).astype(BF16))
        vs.append(jnp.where(is_new, v_new, kv_refs[j][pl.ds(v_row, CMP_BLOCK, stride=inner), :]).astype(BF16))
        dists.append(past - (blk * CMP_BLOCK + _iota((1, CMP_BLOCK), 1)))
    dist = jnp.concatenate(dists, axis=1)
    ok = dist >= 0
    s = _dot_nt(q_ref[...].astype(BF16), jnp.concatenate(ks, axis=0)) * SCALE + _t5_rows(tab_ref[...], dist)
    s = jnp.where(ok, s, NEG_INF)
    p = jnp.where(ok, jnp.exp(s - jnp.max(s, axis=-1, keepdims=True)), 0.0)
    o = _dot(p.astype(BF16), jnp.concatenate(vs, axis=0)) / jnp.sum(p, axis=-1, keepdims=True)
    o_ref[...] = prev_ref[...] + _sigmoid(gate_ref[:, 1:2]) * o


def _nsa_sel_decode(q4, pool_blocks, kv_new, tab_t, gates4, prev4, sel_idx, page_table, past, page):
    b = q4.shape[0]
    assert past % page == 0 and page % CMP_BLOCK == 0
    n_past_blk = past // CMP_BLOCK
    per_page = page // CMP_BLOCK
    inner = pool_blocks.shape[1] // CMP_BLOCK

    def blk_spec(j):
        def index(bi, g, pt, idx):
            n = jnp.minimum(idx[(bi * NSA_GROUPS + g) * N_SEL + j], n_past_blk - 1)
            return (pt[bi, n // per_page] * per_page + n % per_page, 0, 0)
        return pl.BlockSpec((None, CMP_BLOCK * inner, HEAD_DIM), index)

    def grp(bi, g, pt, idx):
        return (bi, g, 0, 0)

    grid_spec = pltpu.PrefetchScalarGridSpec(
        num_scalar_prefetch=2,
        grid=(b, NSA_GROUPS),
        in_specs=[pl.BlockSpec((None, None, NSA_REP, HEAD_DIM), grp)] + [blk_spec(j) for j in range(N_SEL)] + [
            pl.BlockSpec((None, inner, HEAD_DIM), lambda bi, g, pt, idx: (bi, 0, 0)),
            pl.BlockSpec((NSA_REP, N_BUCKETS), lambda bi, g, pt, idx: (g, 0)),
            pl.BlockSpec((None, None, NSA_REP, 3), grp),
            pl.BlockSpec((None, None, NSA_REP, HEAD_DIM), grp),
        ],
        out_specs=pl.BlockSpec((None, None, NSA_REP, HEAD_DIM), grp),
    )
    return pl.pallas_call(
        functools.partial(_nsa_sel_dec_kernel, past=past, n_past_blk=n_past_blk, inner=inner),
        grid_spec=grid_spec,
        out_shape=jax.ShapeDtypeStruct(q4.shape, F32),
        compiler_params=_cparams(("parallel", "parallel")),
        name="nsa_select_decode",
    )(page_table, sel_idx, q4, *([pool_blocks] * N_SEL), kv_new, tab_t, gates4, prev4)


def _nsa_win_dec_kernel(q_ref, win_ref, new_ref, tab_ref, gate_ref, prev_ref, o_ref, *, lw):
    g = pl.program_id(1)
    inner = 2 * NSA_GROUPS
    q = q_ref[...].astype(BF16)
    k = win_ref[pl.ds(g, lw, stride=inner), :].astype(BF16)
    v = win_ref[pl.ds(NSA_GROUPS + g, lw, stride=inner), :].astype(BF16)
    new = new_ref[...]
    dist = lw - _iota((1, lw), 1)
    ok = dist < WINDOW
    tab = tab_ref[...]
    s = jnp.where(ok, _dot_nt(q, k) * SCALE + _t5_rows(tab, dist), NEG_INF)
    kn = _pick_row(new, g).astype(BF16).astype(F32)
    s_new = jnp.sum(q.astype(F32) * kn, axis=-1, keepdims=True) * SCALE + tab[:, 0:1]
    m = jnp.maximum(jnp.max(s, axis=-1, keepdims=True), s_new)
    p = jnp.where(ok, jnp.exp(s - m), 0.0)
    p_new = jnp.exp(s_new - m)
    l = jnp.sum(p, axis=-1, keepdims=True) + p_new
    o = (_dot(p.astype(BF16), v) + p_new * _pick_row(new, NSA_GROUPS + g)) / l
    o_ref[...] = prev_ref[...] + _sigmoid(gate_ref[:, 2:3]) * o


def _nsa_win_decode(q4, win_past, win_new, tab_t, gates4, prev4):
    b = q4.shape[0]
    inner = 2 * NSA_GROUPS
    lw = win_past.shape[1] // inner
    assert lw == WINDOW

    def grp(bi, g):
        return (bi, g, 0, 0)

    return pl.pallas_call(
        functools.partial(_nsa_win_dec_kernel, lw=lw),
        grid=(b, NSA_GROUPS),
        in_specs=[
            pl.BlockSpec((None, None, NSA_REP, HEAD_DIM), grp),
            pl.BlockSpec((None, lw * inner, HEAD_DIM), lambda bi, g: (bi, 0, 0)),
            pl.BlockSpec((None, inner, HEAD_DIM), lambda bi, g: (bi, 0, 0)),
            pl.BlockSpec((NSA_REP, N_BUCKETS), lambda bi, g: (g, 0)),
            pl.BlockSpec((None, None, NSA_REP, 3), grp),
            pl.BlockSpec((None, None, NSA_REP, HEAD_DIM), grp),
        ],
        out_specs=pl.BlockSpec((None, None, NSA_REP, HEAD_DIM), grp),
        out_shape=jax.ShapeDtypeStruct(q4.shape, F32),
        compiler_params=_cparams(("parallel", "parallel")),
        name="nsa_window_decode",
    )(q4, win_past, win_new, tab_t, gates4, prev4)


def _lower_bound(lbp, layer, axis):
    e = jnp.exp(lbp - jnp.max(lbp, axis=axis, keepdims=True))
    sm = e / jnp.sum(e, axis=axis, keepdims=True)
    idx = _iota(lbp.shape, axis)
    return jnp.sum(jnp.where((idx >= 1) & (idx <= layer), sm, 0.0), axis=axis, keepdims=True)


def _gla_kernel(q_ref, f_ref, i_ref, lbp_ref, s0_ref, ng_ref, o_ref, s_ref, st_ref, *, layer, chunk, n_chunks, hp):
    ti = pl.program_id(2)
    lb = _lower_bound(lbp_ref[...], layer, 0)
    tril = _iota((chunk, chunk), 0) >= _iota((chunk, chunk), 1)
    tril_b = tril.astype(BF16)
    ng = ng_ref[...]

    @pl.when(ti == 0)
    def _():
        for r in range(hp):
            st_ref[r] = s0_ref[r].T

    def heads(x):
        return jnp.stack([x[:, r * HEAD_DIM:(r + 1) * HEAD_DIM] for r in range(hp)], axis=0)

    def bdot(a, b, ca, cb):
        return lax.dot_general(a, b, (((ca,), (cb,)), ((0,), (0,))), preferred_element_type=F32)

    def body(c, _):
        r0 = pl.multiple_of(c * chunk, chunk)
        qv = q_ref[pl.ds(r0, chunk), :]
        q = qv * _sigmoid(qv)
        g = lb + (1.0 - lb) * _sigmoid(f_ref[pl.ds(r0, chunk), :])
        k = 1.0 - g
        b = _dot3_rhs(tril_b, jnp.log(g))
        b_last = b[chunk - 1:chunk, :]
        qe = heads((q * jnp.exp(b)).astype(BF16))
        ke = heads((k * jnp.exp(-b)).astype(BF16))
        kt = heads((k * jnp.exp(b_last - b)).astype(BF16))
        v = heads(i_ref[pl.ds(r0, chunk), :].astype(BF16))
        st = st_ref[...]
        att = jnp.where(tril[None], bdot(qe, ke, 2, 2), 0.0)
        o = bdot(att.astype(BF16), v, 2, 1) + bdot(qe, st.astype(BF16), 2, 2)
        st_ref[...] = heads(jnp.exp(b_last)) * st + bdot(v, kt, 1, 1)
        inv = lax.rsqrt(jnp.mean(o * o, axis=-1, keepdims=True) + RMS_EPS)
        o = o * inv * ng
        o_ref[pl.ds(r0, chunk), :] = jnp.concatenate([o[r] for r in range(hp)], axis=1)
        return 0

    lax.fori_loop(0, n_chunks, body, 0)

    @pl.when(ti == pl.num_programs(2) - 1)
    def _():
        for r in range(hp):
            s_ref[r] = st_ref[r].T


def _gla(q, f, i_in, lbp, s0, norm_g, layer, hp=16, tt=512):
    b, t, w = q.shape
    nh = w // HEAD_DIM
    chunk = math.gcd(t, GLA_CHUNK)
    depth = lbp.shape[0]
    tt = min(tt, t)
    assert nh % hp == 0 and t % tt == 0 and tt % chunk == 0
    tok = pl.BlockSpec((None, tt, hp * HEAD_DIM), lambda bi, h, ti: (bi, ti, h))
    st = pl.BlockSpec((None, hp, HEAD_DIM, HEAD_DIM), lambda bi, h, ti: (bi, h, 0, 0))
    return pl.pallas_call(
        functools.partial(_gla_kernel, layer=layer, chunk=chunk, n_chunks=tt // chunk, hp=hp),
        grid=(b, nh // hp, t // tt),
        in_specs=[tok, tok, tok, pl.BlockSpec((depth, hp * HEAD_DIM), lambda bi, h, ti: (0, h)), st,
                  pl.BlockSpec((1, HEAD_DIM), lambda bi, h, ti: (0, 0))],
        out_specs=[tok, st],
        out_shape=[jax.ShapeDtypeStruct((b, t, w), F32), jax.ShapeDtypeStruct(s0.shape, F32)],
        scratch_shapes=[pltpu.VMEM((hp, HEAD_DIM, HEAD_DIM), F32)],
        compiler_params=_cparams(("arbitrary", "arbitrary", "arbitrary")),
        name="hgrn_gla",
    )(q, f, i_in, lbp, s0, norm_g.reshape(1, HEAD_DIM))


def _gla_dec_kernel(q_ref, f_ref, v_ref, lbp_ref, s0_ref, ng_ref, o_ref, s_ref, *, layer):
    lb = _lower_bound(lbp_ref[...], layer, 0)[0]
    qv = q_ref[...]
    q = qv * _sigmoid(qv)
    g = lb + (1.0 - lb) * _sigmoid(f_ref[...])
    s = g * s0_ref[...] + (1.0 - g) * v_ref[...]
    s_ref[...] = s
    o = jnp.sum(q * s, axis=1, keepdims=True)
    inv = lax.rsqrt(jnp.mean(o * o, axis=-1, keepdims=True) + RMS_EPS)
    o_ref[...] = o * inv * ng_ref[...]


def _gla_decode(q, f, i_in, lbp, s0, norm_g, layer):
    b, nh = s0.shape[:2]
    depth = lbp.shape[0]
    col = pl.BlockSpec((None, nh, HEAD_DIM, 1), lambda bi: (bi, 0, 0, 0))
    row = pl.BlockSpec((None, nh, 1, HEAD_DIM), lambda bi: (bi, 0, 0, 0))
    st = pl.BlockSpec((None, nh, HEAD_DIM, HEAD_DIM), lambda bi: (bi, 0, 0, 0))
    o, s = pl.pallas_call(
        functools.partial(_gla_dec_kernel, layer=layer),
        grid=(b,),
        in_specs=[col, col, row, pl.BlockSpec((depth, nh, HEAD_DIM, 1), lambda bi: (0, 0, 0, 0)), st,
                  pl.BlockSpec((1, HEAD_DIM), lambda bi: (0, 0))],
        out_specs=[row, st],
        out_shape=[jax.ShapeDtypeStruct((b, nh, 1, HEAD_DIM), F32), jax.ShapeDtypeStruct(s0.shape, F32)],
        compiler_params=_cparams(("parallel",)),
        name="hgrn_step",
    )(q.reshape(b, nh, HEAD_DIM, 1), f.reshape(b, nh, HEAD_DIM, 1), i_in.reshape(b, nh, 1, HEAD_DIM),
      lbp.reshape(depth, nh, HEAD_DIM, 1), s0, norm_g.reshape(1, HEAD_DIM))
    return o.reshape(b, 1, nh * HEAD_DIM), s


def _as_rows(w):
    if w.shape[1] % 128:
        return jnp.swapaxes(w, 0, 1), True
    return w, False


def _fox_layer(x, bt, norm_g, w_in, b_f, past):
    b, t = bt
    d = x.shape[1]
    nh = d // HEAD_DIM
    wt, tr = _as_rows(w_in)
    assert tr
    q, k, v, z = _norm_proj(x, norm_g, wt, [(n * d, d) for n in range(4)], 512, True)
    (fl,) = _norm_proj(x, norm_g, wt, [(4 * d, nh)], nh, True)
    if past is None:
        q3, k3, v3 = (a.reshape(b, t, d) for a in (q, k, v))
        logf, cum = _fox_prep(fl.reshape(b, t, nh), b_f)
        o = _fox_attn(q3, k3, v3, cum)
    else:
        k_pool, v_pool, lf_pool, page_table = past
        npool, page = k_pool.shape[:2]
        o, logf = _fox_decode(q.reshape(b, nh, HEAD_DIM), k.reshape(b, nh, HEAD_DIM), v.reshape(b, nh, HEAD_DIM),
                              fl.reshape(b, nh, 1), b_f, k_pool.reshape(npool, page * nh, HEAD_DIM),
                              v_pool.reshape(npool, page * nh, HEAD_DIM), jnp.swapaxes(lf_pool, 1, 2), page_table)
    state = (k.reshape(1, b, t, nh, HEAD_DIM), v.reshape(1, b, t, nh, HEAD_DIM), logf.reshape(1, b, t, nh))
    return o.reshape(b * t, d), z, state


def _pad_summaries(summaries, b, n_cmp, n_pad):
    s = summaries.reshape(2, NSA_GROUPS, b, n_cmp, HEAD_DIM)
    return jnp.pad(jnp.swapaxes(s, 1, 2), ((0, 0), (0, 0), (0, 0), (0, n_pad - n_cmp), (0, 0)))


def _nsa_layer(x, bt, norm_g, w_in, cmp_pos, cmp_w1, cmp_w2, t5_bias, past):
    b, t = bt
    d = x.shape[1]
    kvw = NSA_GROUPS * HEAD_DIM
    off_gate = d + 6 * kvw
    n_gate = 3 * (d // HEAD_DIM)
    wt, tr = _as_rows(w_in)
    assert tr
    q, kv4, win = _norm_proj(x, norm_g, wt, [(0, d), (d, 4 * kvw), (d + 4 * kvw, 2 * kvw)], 512, True)
    (gates,) = _norm_proj(x, norm_g, wt[off_gate:off_gate + n_gate], [(0, n_gate)], n_gate, True)
    (z,) = _norm_proj(x, norm_g, wt[off_gate + n_gate:], [(0, d)], 512, True)
    q3, gates3 = q.reshape(b, t, d), gates.reshape(b, t, n_gate)
    inner = NSA_KINDS * NSA_GROUPS
    if past is None:
        kv3, win3 = kv4.reshape(b, t, 4 * kvw), win.reshape(b, t, 2 * kvw)
        n_cmp = t // CMP_BLOCK
        n_pad = -(-n_cmp // 128) * 128
        summaries = _pad_summaries(_compress(kv4, cmp_pos, cmp_w1, cmp_w2), b, n_cmp, n_pad)
        o, mask = _nsa_cmp(q3, summaries[0], summaries[1], gates3, 0, t, False)
        o = _nsa_attn(q3, kv3, 4, 6, _t5_tiles(t5_bias, 128, math.gcd(t, 512)), gates3, o, mask, 1)
        o = _nsa_attn(q3, win3, 0, 2, _t5_tiles(t5_bias, 128, math.gcd(t, 256)), gates3, o, None, 2)
        keep = min(WINDOW, t)
        win_state = win3[:, t - keep:]
    else:
        kv_pool, win_past, page_table = past
        assert t == 1
        npool, page = kv_pool.shape[:2]
        n_pages = page_table.shape[1]
        plen = n_pages * page
        pool = kv_pool.reshape(npool, page * inner, HEAD_DIM)
        n_cmp = plen // CMP_BLOCK
        n_pad = -(-(n_cmp + 1) // 128) * 128
        summaries = _pad_summaries(_compress(_cmp_gather(pool, page_table, page), cmp_pos, cmp_w1, cmp_w2),
                                   b, n_cmp, n_pad)
        rows = SUBLANES
        q_pad = jnp.pad(q3, ((0, 0), (0, rows - t), (0, 0)))
        g_pad = jnp.pad(gates3, ((0, 0), (0, rows - t), (0, 0)))
        o, sel_idx = _nsa_cmp(q_pad, summaries[0], summaries[1], g_pad, plen, plen + t, True)
        o4 = o[:, :1].reshape(b, NSA_GROUPS, NSA_REP, HEAD_DIM)
        sel_idx = sel_idx[:, :, 0, :].reshape(-1)
        q4 = q3.reshape(b, NSA_GROUPS, NSA_REP, HEAD_DIM)
        gates4 = gates3.reshape(b, NSA_GROUPS, NSA_REP, 3)
        tab_t = t5_bias.T
        blocks = pool.reshape(npool * page // CMP_BLOCK, CMP_BLOCK * inner, HEAD_DIM)
        o4 = _nsa_sel_decode(q4, blocks, kv4.reshape(b, inner, HEAD_DIM), tab_t, gates4, o4, sel_idx,
                             page_table, plen, page)
        lw = win_past.shape[1]
        wp = win_past.reshape(b, lw * 2 * NSA_GROUPS, HEAD_DIM)
        wn = win.reshape(b, 2 * NSA_GROUPS, HEAD_DIM)
        o4 = _nsa_win_decode(q4, wp, wn, tab_t, gates4, o4)
        o = o4.reshape(b, t, d)
        keep = min(WINDOW, lw + t)
        win_state = jnp.concatenate([wp, wn], axis=1)[:, (lw + t - keep) * 2 * NSA_GROUPS:]
    state = (kv4.reshape(1, b, t, NSA_KINDS, NSA_GROUPS, HEAD_DIM),
             win_state.reshape(1, b, -1, 2, NSA_GROUPS, HEAD_DIM))
    return o.reshape(b * t, d), z, state


def _hgrn_layer(x, bt, norm_g, w_in, lbp, hgrn_norm_g, layer, s0):
    b, t = bt
    d = x.shape[1]
    q, f, i_in, z = _norm_proj(x, norm_g, w_in, [(n * d, d) for n in range(4)], 512)
    if t == 1:
        o, s_new = _gla_decode(q, f, i_in, lbp, s0, hgrn_norm_g, layer)
    else:
        o, s_new = _gla(q.reshape(b, t, d), f.reshape(b, t, d), i_in.reshape(b, t, d), lbp, s0, hgrn_norm_g, layer)
    return o.reshape(b * t, d), z, (s_new[None],)


def _sb_layer(x, bt, norm_g, w_in, past):
    b, t = bt
    d = x.shape[1]
    nh = d // HEAD_DIM
    q, k, v, z = _norm_proj(x, norm_g, w_in, [(n * d, d) for n in range(4)], 512)
    if past is None:
        o = _sb_attn(q.reshape(b, t, d), k.reshape(b, t, d), v.reshape(b, t, d))
    else:
        k_pool, v_pool, page_table = past
        npool, page = k_pool.shape[:2]
        o = _sb_decode(q.reshape(b, nh, HEAD_DIM), k_pool.reshape(npool, page * nh, HEAD_DIM),
                       v_pool.reshape(npool, page * nh, HEAD_DIM), page_table, page)
    state = (k.reshape(1, b, t, nh, HEAD_DIM), v.reshape(1, b, t, nh, HEAD_DIM))
    return o.reshape(b * t, d), z, state


def kernel(x_prompt, x_sample, cache_fox_k, cache_fox_v, cache_fox_logf, cache_nsa_kv, state_nsa_win, state_hgrn, cache_sb_k, cache_sb_v, page_table, p_prompt, p_sample, norm_g, final_norm_g, ple_gate_w, ple_proj_w, fox_w_in, fox_b_f, fox_w_out, nsa_w_in, nsa_cmp_pos, nsa_cmp_w1, nsa_cmp_w2, t5_bias, nsa_w_out, hgrn_w_in, hgrn_lower_bounds, hgrn_norm_g, hgrn_w_out, sb_w_in, sb_w_out):
    depth = norm_g.shape[0]
    assert depth == 4 and all(a.shape[0] == 1 for a in (fox_w_in, nsa_w_in, hgrn_w_in, sb_w_in))
    w_out = [w[0].astype(BF16) for w in (fox_w_out, nsa_w_out, hgrn_w_out, sb_w_out)]
    w_gate = [ple_gate_w[i].astype(BF16) for i in range(depth)]
    w_proj = [ple_proj_w[i].astype(BF16) for i in range(depth)]

    def trunk(x3, p4, decode):
        b, t, d = x3.shape
        bt = (b, t)
        x = x3.reshape(b * t, d)
        p = p4.reshape(depth, b * t, -1)
        if decode:
            past = ((cache_fox_k[0], cache_fox_v[0], cache_fox_logf[0], page_table),
                    (cache_nsa_kv[0], state_nsa_win[0], page_table),
                    state_hgrn[0],
                    (cache_sb_k[0], cache_sb_v[0], page_table))
        else:
            past = (None, None, jnp.zeros((b,) + state_hgrn.shape[2:], F32), None)
        o, z, st_a = _fox_layer(x, bt, norm_g[0], fox_w_in[0], fox_b_f[0], past[0])
        x = _layer_tail(o, z, x, p[0], w_out[0], w_gate[0], w_proj[0])
        o, z, st_b = _nsa_layer(x, bt, norm_g[1], nsa_w_in[0], nsa_cmp_pos[0], nsa_cmp_w1[0], nsa_cmp_w2[0],
                                t5_bias, past[1])
        x = _layer_tail(o, z, x, p[1], w_out[1], w_gate[1], w_proj[1])
        o, z, st_c = _hgrn_layer(x, bt, norm_g[2], hgrn_w_in[0], hgrn_lower_bounds, hgrn_norm_g[0], 2, past[2])
        x = _layer_tail(o, z, x, p[2], w_out[2], w_gate[2], w_proj[2])
        o, z, st_d = _sb_layer(x, bt, norm_g[3], sb_w_in[0], past[3])
        y = _layer_tail(o, z, x, p[3], w_out[3], w_gate[3], w_proj[3], final_norm_g).reshape(b, t, d)
        return y, st_a, st_b, st_c, st_d

    yp, pa, pb, pc, pd = trunk(x_prompt, p_prompt, False)
    ys, sa, sb, sc, sd = trunk(x_sample, p_sample, True)
    return (yp, ys, pa[0], sa[0], pa[1], sa[1], pa[2], sa[2], pb[0], sb[0], pb[1], sb[1],
            pc[0], sc[0], pd[0], sd[0], pd[1], sd[1])
```

```python
import functools
import math

import jax
import jax.numpy as jnp
from jax import lax
from jax.experimental import pallas as pl
from jax.experimental.pallas import tpu as pltpu

F32 = jnp.float32
BF16 = jnp.bfloat16
I32 = jnp.int32

HEAD_DIM = 128
RMS_EPS = 1e-6
NEG_INF = -1e30
FORCE_SCORE = 1e9
CMP_BLOCK = 64
N_SEL = 16
WINDOW = 512
N_BUCKETS = 32
MAX_DISTANCE = 128
GLA_CHUNK = 32
NSA_GROUPS = 2
NSA_REP = 8
NSA_KINDS = 4
SUBLANES = 8
SCALE = HEAD_DIM ** -0.5
LOG2E = 1.0 / math.log(2.0)
SB_DEAD_TAIL = -160.0
MIB = 1 << 20


def _cparams(sem, vmem_mib=None):
    kw = dict(dimension_semantics=sem)
    if vmem_mib is not None:
        kw["vmem_limit_bytes"] = vmem_mib * MIB
    return pltpu.CompilerParams(**kw)


def _iota(shape, axis):
    return lax.broadcasted_iota(I32, shape, axis)


def _dot(a, b):
    return jnp.dot(a, b, preferred_element_type=F32)


def _dot_nt(a, b):
    return lax.dot_general(a, b, (((1,), (1,)), ((), ())), preferred_element_type=F32)


def _dot_tn(a, b):
    return lax.dot_general(a, b, (((0,), (0,)), ((), ())), preferred_element_type=F32)


def _split3(x):
    hi = x.astype(BF16)
    r1 = x - hi.astype(F32)
    mid = r1.astype(BF16)
    lo = (r1 - mid.astype(F32)).astype(BF16)
    return hi, mid, lo


def _dot3(x, m, dot=_dot):
    hi, mid, lo = _split3(x)
    return dot(hi, m) + dot(mid, m) + dot(lo, m)


def _dot3_rhs(m, x, dot=_dot):
    hi, mid, lo = _split3(x)
    return dot(m, hi) + dot(m, mid) + dot(m, lo)


def _log_sigmoid_pair(x):
    sp = jnp.log1p(jnp.exp(-jnp.abs(x)))
    return jnp.minimum(x, 0.0) - sp, jnp.minimum(-x, 0.0) - sp


def _log2_sigmoid_pair(x2):
    sp = jnp.log(1.0 + jnp.exp2(-jnp.abs(x2))) * LOG2E
    pos = jnp.minimum(x2, 0.0) - sp
    return pos, pos - x2


def _sigmoid(x):
    return jax.nn.sigmoid(x)


def _t5_bucket(dist):
    exact = N_BUCKETS // 2
    d = jnp.maximum(dist, 0)
    logd = jnp.log(jnp.maximum(d, 1).astype(F32) * (1.0 / exact))
    large = exact + (logd * ((N_BUCKETS - exact) / math.log(MAX_DISTANCE / exact))).astype(I32)
    return jnp.where(d < exact, d, jnp.minimum(large, N_BUCKETS - 1))


def _div_pow2(x, n):
    assert n & (n - 1) == 0
    return lax.shift_right_arithmetic(x, n.bit_length() - 1)


def _row_tile(m, cap):
    t = min(m, cap)
    assert m % t == 0, (m, t)
    return t


def _norm_proj_kernel(x_ref, g_ref, w_ref, *rest, slab_blocks, transposed):
    outs, u_ref = rest[:-1], rest[-1]
    j = pl.program_id(1)

    @pl.when(j == 0)
    def _():
        x = x_ref[...]
        inv = lax.rsqrt(jnp.mean(x * x, axis=-1, keepdims=True) + RMS_EPS)
        u_ref[...] = (x * inv * g_ref[...]).astype(BF16)

    w = w_ref[...].astype(BF16)
    acc = _dot_nt(u_ref[...], w) if transposed else _dot(u_ref[...], w)
    for (lo, n), o_ref in zip(slab_blocks, outs):
        @pl.when((j >= lo) & (j < lo + n))
        def _(o_ref=o_ref):
            o_ref[...] = acc


def _norm_proj(x, g, w, slabs, tn, transposed=False):
    m, d = x.shape
    tm = _row_tile(m, 1024)
    base = slabs[0][0]
    slab_blocks, pos = [], base
    for off, width in slabs:
        assert off == pos and width % tn == 0 and off % tn == 0, (off, width, tn)
        slab_blocks.append(((off - base) // tn, width // tn))
        pos += width
    nj = (pos - base) // tn
    b0 = base // tn

    def out_map(lo, n):
        return lambda i, j: (i, jnp.minimum(jnp.maximum(j - lo, 0), n - 1))

    if transposed:
        w_spec = pl.BlockSpec((tn, d), lambda i, j: (b0 + j, 0))
    else:
        w_spec = pl.BlockSpec((d, tn), lambda i, j: (0, b0 + j))
    return pl.pallas_call(
        functools.partial(_norm_proj_kernel, slab_blocks=tuple(slab_blocks), transposed=transposed),
        grid=(m // tm, nj),
        in_specs=[pl.BlockSpec((tm, d), lambda i, j: (i, 0)), pl.BlockSpec((1, d), lambda i, j: (0, 0)), w_spec],
        out_specs=[pl.BlockSpec((tm, tn), out_map(lo, n)) for lo, n in slab_blocks],
        out_shape=[jax.ShapeDtypeStruct((m, width), F32) for _, width in slabs],
        scratch_shapes=[pltpu.VMEM((tm, d), BF16)],
        compiler_params=_cparams(("parallel", "arbitrary"), 56),
        name="norm_proj",
    )(x, g.reshape(1, d), w)


def _layer_tail_kernel(o_ref, z_ref, x_ref, p_ref, wo_ref, wg_ref, wp_ref, *rest):
    z = z_ref[...]
    a = (o_ref[...] * (z * _sigmoid(z))).astype(BF16)
    h = x_ref[...] + _dot(a, wo_ref[...])
    gate = _sigmoid(_dot(h.astype(BF16), wg_ref[...]))
    y = h + gate * _dot(p_ref[...].astype(BF16), wp_ref[...])
    if len(rest) == 2:
        g_ref, out_ref = rest
        y = y * lax.rsqrt(jnp.mean(y * y, axis=-1, keepdims=True) + RMS_EPS) * g_ref[...]
    else:
        (out_ref,) = rest
    out_ref[...] = y


def _layer_tail(o, z, x, p, wo, wg, wp, final_g=None):
    m, d = o.shape
    pd = p.shape[1]
    tm = _row_tile(m, 256)
    row = pl.BlockSpec((tm, d), lambda i: (i, 0))

    def resident(shape):
        return pl.BlockSpec(shape, lambda i: (0, 0), pipeline_mode=pl.Buffered(1))

    in_specs = [row, row, row, pl.BlockSpec((tm, pd), lambda i: (i, 0)),
                resident((d, d)), resident((d, d)), resident((pd, d))]
    args = [o, z, x, p, wo, wg, wp]
    if final_g is not None:
        in_specs.append(pl.BlockSpec((1, d), lambda i: (0, 0)))
        args.append(final_g.reshape(1, d))
    return pl.pallas_call(
        _layer_tail_kernel,
        grid=(m // tm,),
        in_specs=in_specs,
        out_specs=row,
        out_shape=jax.ShapeDtypeStruct((m, d), F32),
        compiler_params=_cparams(("parallel",), 48),
        name="layer_tail",
    )(*args)


def _fox_prep_kernel(x_ref, bf_ref, logf_ref, cum_ref, *, n_chunks):
    tril = (_iota((128, 128), 0) >= _iota((128, 128), 1)).astype(BF16)
    carry = jnp.zeros((1, x_ref.shape[1]), F32)
    for c in range(n_chunks):
        sl = slice(c * 128, (c + 1) * 128)
        lf, _ = _log_sigmoid_pair(x_ref[sl, :] + bf_ref[...])
        logf_ref[sl, :] = lf
        cum = _dot3_rhs(tril, lf) + carry
        cum_ref[sl, :] = cum
        carry = cum[127:128, :]


def _fox_prep(logits, b_f):
    b, t, h = logits.shape
    assert t % 128 == 0
    spec = pl.BlockSpec((None, t, h), lambda i: (i, 0, 0))
    return pl.pallas_call(
        functools.partial(_fox_prep_kernel, n_chunks=t // 128),
        grid=(b,),
        in_specs=[spec, pl.BlockSpec((1, h), lambda i: (0, 0))],
        out_specs=[spec, spec],
        out_shape=[jax.ShapeDtypeStruct((b, t, h), F32)] * 2,
        compiler_params=_cparams(("parallel",)),
        name="fox_prep",
    )(logits, b_f.reshape(1, h))


def _pick_col(x, c):
    return jnp.sum(jnp.where(_iota(x.shape, 1) == c, x, 0.0), axis=1, keepdims=True)


def _bias_lanes(col, ones_first):
    rows = col.shape[0]
    hi, mid, lo = (p.astype(F32) for p in _split3(col))
    lane = _iota((rows, HEAD_DIM), 1)
    one, val = (0, 3) if ones_first else (3, 0)
    x = jnp.where(lane == val, hi, jnp.where(lane == val + 1, mid, jnp.where(lane == val + 2, lo, 0.0)))
    x = jnp.where((lane >= one) & (lane < one + 3), 1.0, x)
    return x.astype(BF16)


def _fox_attn_kernel(q_ref, k_ref, v_ref, cq_ref, ck_ref, o_ref, ka_ref, vb_ref, *, tq):
    h = pl.program_id(1)
    qi = pl.program_id(2)
    t = k_ref.shape[0]

    @pl.when(qi == 0)
    def _():
        for c in range(t // tq):
            sl = slice(c * tq, (c + 1) * tq)
            ka_ref[sl, :HEAD_DIM] = k_ref[sl, :].astype(BF16)
            ka_ref[sl, HEAD_DIM:] = _bias_lanes(_pick_col(ck_ref[sl, :], h) * -LOG2E, True)
            vb_ref[sl, :] = v_ref[sl, :].astype(BF16)

    qa = jnp.concatenate([(q_ref[...] * (SCALE * LOG2E)).astype(BF16),
                          _bias_lanes(_pick_col(cq_ref[...], h) * LOG2E, False)], axis=1)
    tri = _iota((tq, tq), 1) <= _iota((tq, tq), 0)

    def step(kc, carry, diagonal):
        m, l, acc = carry
        ks = pl.multiple_of(kc * tq, tq)
        s = _dot_nt(qa, ka_ref[pl.ds(ks, tq), :])
        if diagonal:
            s = jnp.where(tri, s, NEG_INF)
        m_new = jnp.maximum(m, jnp.max(s, axis=-1, keepdims=True))
        p = jnp.exp2(s - m_new)
        alpha = jnp.exp2(m - m_new)
        l = alpha * l + jnp.sum(p, axis=-1, keepdims=True)
        acc = alpha * acc + _dot(p.astype(BF16), vb_ref[pl.ds(ks, tq), :])
        return m_new, l, acc

    init = (jnp.full((tq, 1), NEG_INF, F32), jnp.zeros((tq, 1), F32), jnp.zeros((tq, HEAD_DIM), F32))
    carry = lax.fori_loop(0, qi, lambda kc, c: step(kc, c, False), init)
    _, l, acc = step(qi, carry, True)
    o_ref[...] = acc / l


def _fox_attn(q, k, v, cum, tq=512):
    b, t, w = q.shape
    h = w // HEAD_DIM
    tq = min(tq, t)
    assert t % tq == 0
    return pl.pallas_call(
        functools.partial(_fox_attn_kernel, tq=tq),
        grid=(b, h, t // tq),
        in_specs=[
            pl.BlockSpec((None, tq, HEAD_DIM), lambda bi, hi, qi: (bi, qi, hi)),
            pl.BlockSpec((None, t, HEAD_DIM), lambda bi, hi, qi: (bi, 0, hi)),
            pl.BlockSpec((None, t, HEAD_DIM), lambda bi, hi, qi: (bi, 0, hi)),
            pl.BlockSpec((None, tq, h), lambda bi, hi, qi: (bi, qi, 0)),
            pl.BlockSpec((None, t, h), lambda bi, hi, qi: (bi, 0, 0)),
        ],
        out_specs=pl.BlockSpec((None, tq, HEAD_DIM), lambda bi, hi, qi: (bi, qi, hi)),
        out_shape=jax.ShapeDtypeStruct((b, t, w), F32),
        scratch_shapes=[pltpu.VMEM((t, 2 * HEAD_DIM), BF16), pltpu.VMEM((t, HEAD_DIM), BF16)],
        compiler_params=_cparams(("arbitrary", "arbitrary", "arbitrary")),
        name="fox_attn",
    )(q, k, v, cum, cum)


def _sb_attn_kernel(q_ref, k_ref, v_ref, o_ref, kb_ref, vb_ref, *, tq):
    qi = pl.program_id(2)
    t = k_ref.shape[0]
    sub = HEAD_DIM

    @pl.when(qi == 0)
    def _():
        for c in range(t // tq):
            sl = slice(c * tq, (c + 1) * tq)
            kb_ref[sl, :] = k_ref[sl, :].astype(BF16)
            vb_ref[sl, :] = v_ref[sl, :].astype(BF16)

    q = (q_ref[...] * (SCALE * LOG2E)).astype(BF16)
    later = (_iota((sub, sub), 0) > _iota((sub, sub), 1)).astype(BF16)
    scan = jnp.concatenate([later, jnp.ones((sub, sub), BF16)], axis=1)
    scan = jnp.concatenate([scan, scan], axis=0)
    before = _iota((tq, tq), 1) < _iota((tq, tq), 0)

    def step(kc, carry, diagonal):
        tail, acc = carry
        ks = pl.multiple_of(kc * tq, tq)
        x = _dot_nt(q, kb_ref[pl.ds(ks, tq), :])
        ls_pos, ls_neg = _log2_sigmoid_pair(x)
        if diagonal:
            ls_neg = jnp.where(before, ls_neg, 0.0)
        ws = [None] * (tq // sub)
        for j in reversed(range(tq // sub)):
            sl = slice(j * sub, (j + 1) * sub)
            stay = ls_neg[:, sl]
            hi = stay.astype(BF16)
            lo = (stay - hi.astype(F32)).astype(BF16)
            sums = _dot(jnp.concatenate([hi, lo], axis=1), scan)
            w = jnp.exp2(ls_pos[:, sl] + sums[:, :sub] + tail)
            if diagonal:
                w = jnp.where(before[:, sl], w, 0.0)
            ws[j] = w.astype(BF16)
            tail = tail + sums[:, sub:]
        acc = acc + _dot(jnp.concatenate(ws, axis=1), vb_ref[pl.ds(ks, tq), :])
        return tail, acc

    tail, acc = step(qi, (jnp.zeros((tq, sub), F32), jnp.zeros((tq, HEAD_DIM), F32)), True)

    def alive(c):
        return (c[0] < qi) & (jnp.max(c[1]) > SB_DEAD_TAIL)

    def older(c):
        i, tail, acc = c
        tail, acc = step(qi - 1 - i, (tail, acc), False)
        return i + 1, tail, acc

    _, _, acc = lax.while_loop(alive, older, (jnp.int32(0), tail, acc))
    o_ref[...] = acc


def _sb_attn(q, k, v, tq=512):
    b, t, w = q.shape
    h = w // HEAD_DIM
    tq = min(tq, t)
    assert t % tq == 0
    return pl.pallas_call(
        functools.partial(_sb_attn_kernel, tq=tq),
        grid=(b, h, t // tq),
        in_specs=[
            pl.BlockSpec((None, tq, HEAD_DIM), lambda bi, hi, qi: (bi, qi, hi)),
            pl.BlockSpec((None, t, HEAD_DIM), lambda bi, hi, qi: (bi, 0, hi)),
            pl.BlockSpec((None, t, HEAD_DIM), lambda bi, hi, qi: (bi, 0, hi)),
        ],
        out_specs=pl.BlockSpec((None, tq, HEAD_DIM), lambda bi, hi, qi: (bi, qi, hi)),
        out_shape=jax.ShapeDtypeStruct((b, t, w), F32),
        scratch_shapes=[pltpu.VMEM((t, HEAD_DIM), BF16), pltpu.VMEM((t, HEAD_DIM), BF16)],
        compiler_params=_cparams(("arbitrary", "arbitrary", "arbitrary")),
        name="sb_attn",
    )(q, k, v)


FOX_DEC_PAGES = 2
SB_DEC_PAGES = 4


def _page_scores(k_ref, q, nh, page):
    y = (k_ref[...].reshape(page, nh, HEAD_DIM) * q[None]).astype(BF16)
    s = _dot(y.reshape(page * nh, HEAD_DIM), jnp.ones((HEAD_DIM, HEAD_DIM), BF16))
    return s.reshape(page, nh, HEAD_DIM)


def _fox_dec_kernel(pt_ref, q_ref, kn_ref, vn_ref, fl_ref, bf_ref, *refs, pp):
    k_refs, v_refs, lft_refs = refs[:pp], refs[pp:2 * pp], refs[2 * pp:3 * pp]
    o_ref, lfo_ref, m_ref, l_ref, c_ref, acc_ref = refs[3 * pp:]
    p = pl.program_id(1)
    nh, page = lft_refs[0].shape
    q = q_ref[...] * (SCALE * LOG2E)

    @pl.when(p == 0)
    def _():
        s_new = jnp.sum(q * kn_ref[...], axis=-1, keepdims=True)
        m_ref[...] = jnp.broadcast_to(s_new, m_ref.shape)
        l_ref[...] = jnp.ones_like(l_ref)
        acc_ref[...] = vn_ref[...]
        lf_new, _ = _log_sigmoid_pair(fl_ref[...] + bf_ref[...])
        lfo_ref[...] = lf_new
        c_ref[...] = lf_new

    later = (_iota((page, page), 0) > _iota((page, page), 1)).astype(BF16)
    for i in range(pp):
        lft = lft_refs[i][...]
        decay = (c_ref[...] + _dot3(lft, later)) * LOG2E
        decay = jnp.stack([jnp.broadcast_to(decay[:, r:r + 1], (nh, HEAD_DIM)) for r in range(page)], axis=0)
        s = _page_scores(k_refs[i], q, nh, page) + decay
        m_old = m_ref[...]
        m_new = jnp.maximum(m_old, jnp.max(s, axis=0))
        pr = jnp.exp2(s - m_new[None])
        alpha = jnp.exp2(m_old - m_new)
        l_ref[...] = alpha * l_ref[...] + jnp.sum(pr, axis=0)
        acc_ref[...] = alpha * acc_ref[...] + jnp.sum(pr * v_refs[i][...].reshape(page, nh, HEAD_DIM), axis=0)
        m_ref[...] = m_new
        c_ref[...] = c_ref[...] + jnp.sum(lft, axis=1, keepdims=True)

    @pl.when(p == pl.num_programs(1) - 1)
    def _():
        o_ref[...] = acc_ref[...] / l_ref[...]


def _fox_decode(q, k_new, v_new, fl, b_f, k_pool, v_pool, lf_pool_t, page_table):
    b, nh, _ = q.shape
    n_pages = page_table.shape[1]
    page = lf_pool_t.shape[2]
    pp = math.gcd(FOX_DEC_PAGES, n_pages)

    def row(bi, p, pt):
        return (bi, 0, 0)

    def pg(i):
        return lambda bi, p, pt: (pt[bi, n_pages - 1 - (p * pp + i)], 0, 0)

    head = pl.BlockSpec((None, nh, HEAD_DIM), row)
    stat = pltpu.VMEM((nh, HEAD_DIM), F32)
    grid_spec = pltpu.PrefetchScalarGridSpec(
        num_scalar_prefetch=1,
        grid=(b, n_pages // pp),
        in_specs=[head, head, head, pl.BlockSpec((None, nh, 1), row), pl.BlockSpec((nh, 1), lambda bi, p, pt: (0, 0))]
        + [pl.BlockSpec((None, page * nh, HEAD_DIM), pg(i)) for i in range(pp)] * 2
        + [pl.BlockSpec((None, nh, page), pg(i)) for i in range(pp)],
        out_specs=[head, pl.BlockSpec((None, nh, 1), row)],
        scratch_shapes=[stat, stat, pltpu.VMEM((nh, 1), F32), stat],
    )
    return pl.pallas_call(
        functools.partial(_fox_dec_kernel, pp=pp),
        grid_spec=grid_spec,
        out_shape=[jax.ShapeDtypeStruct((b, nh, HEAD_DIM), F32), jax.ShapeDtypeStruct((b, nh, 1), F32)],
        compiler_params=_cparams(("arbitrary", "arbitrary")),
        name="fox_decode",
    )(page_table, q, k_new, v_new, fl, b_f.reshape(nh, 1), *([k_pool] * pp), *([v_pool] * pp), *([lf_pool_t] * pp))


def _sb_dec_kernel(pt_ref, live_ref, q_ref, c0_ref, acc0_ref, *refs, pp, page):
    k_refs, v_refs = refs[:pp], refs[pp:2 * pp]
    o_ref, cout_ref, c_ref, acc_ref = refs[2 * pp:]
    p = pl.program_id(1)
    nh = q_ref.shape[0]
    q = q_ref[...] * (SCALE * LOG2E)

    @pl.when(p == 0)
    def _():
        c_ref[...] = c0_ref[...]
        acc_ref[...] = acc0_ref[...]

    @pl.when(jnp.max(c_ref[...]) > SB_DEAD_TAIL)
    def _():
        for i in range(pp):
            ls_pos, ls_neg = _log2_sigmoid_pair(_page_scores(k_refs[i], q, nh, page))
            tail = c_ref[...]
            ws = [None] * page
            for r in reversed(range(page)):
                ws[r] = jnp.exp2(ls_pos[r] + tail)
                tail = tail + ls_neg[r]
            acc_ref[...] = acc_ref[...] + jnp.sum(
                jnp.stack(ws, axis=0) * v_refs[i][...].reshape(page, nh, HEAD_DIM), axis=0)
            c_ref[...] = tail

    @pl.when(p == pl.num_programs(1) - 1)
    def _():
        o_ref[...] = acc_ref[...]
        cout_ref[...] = c_ref[...]


def _sb_decode_steps(q, c0, acc0, live, k_pool, v_pool, page_table, page, pp, first, n_steps):
    b, nh, _ = q.shape
    n_pages = page_table.shape[1]

    def row(bi, p, pt, lv):
        return (bi, 0, 0)

    def pg(i):
        def index(bi, p, pt, lv):
            step = first + jnp.where(lv[bi] > 0, p, 0)
            return (pt[bi, n_pages - 1 - (step * pp + i)], 0, 0)
        return index

    head = pl.BlockSpec((None, nh, HEAD_DIM), row)
    stat = pltpu.VMEM((nh, HEAD_DIM), F32)
    grid_spec = pltpu.PrefetchScalarGridSpec(
        num_scalar_prefetch=2,
        grid=(b, n_steps),
        in_specs=[head, head, head] + [pl.BlockSpec((None, page * nh, HEAD_DIM), pg(i)) for i in range(pp)] * 2,
        out_specs=[head, head],
        scratch_shapes=[stat, stat],
    )
    return pl.pallas_call(
        functools.partial(_sb_dec_kernel, pp=pp, page=page),
        grid_spec=grid_spec,
        out_shape=[jax.ShapeDtypeStruct((b, nh, HEAD_DIM), F32)] * 2,
        compiler_params=_cparams(("arbitrary", "arbitrary")),
        name="sb_decode",
    )(page_table, live, q, c0, acc0, *([k_pool] * pp), *([v_pool] * pp))


def _sb_decode(q, k_pool, v_pool, page_table, page):
    b = q.shape[0]
    n_pages = page_table.shape[1]
    pp = math.gcd(SB_DEC_PAGES, n_pages)
    steps = n_pages // pp
    zero = jnp.zeros(q.shape, F32)
    acc, tail = _sb_decode_steps(q, zero, zero, jnp.ones((b,), I32), k_pool, v_pool, page_table, page, pp, 0, 1)
    if steps > 1:
        live = (jnp.max(tail, axis=(1, 2)) > SB_DEAD_TAIL).astype(I32)
        acc, _ = _sb_decode_steps(q, tail, acc, live, k_pool, v_pool, page_table, page, pp, 1, steps - 1)
    return acc


def _cmp_gather_kernel(pt_ref, *refs, inner, page):
    x_refs, o_ref = refs[:-1], refs[-1]
    for i, x_ref in enumerate(x_refs):
        for c in range(2 * NSA_GROUPS):
            o_ref[c, i * page:(i + 1) * page, :] = x_ref[pl.ds(c, page, stride=inner), :]


def _cmp_gather(pool, page_table, page, pp=8):
    b, n_pages = page_table.shape
    inner = pool.shape[1] // page
    pp = math.gcd(pp, n_pages)
    steps = n_pages // pp

    def page_spec(i):
        return pl.BlockSpec((None, page * inner, HEAD_DIM), lambda bi, p, pt: (pt[bi, p * pp + i], 0, 0))

    grid_spec = pltpu.PrefetchScalarGridSpec(
        num_scalar_prefetch=1,
        grid=(b, steps),
        in_specs=[page_spec(i) for i in range(pp)],
        out_specs=pl.BlockSpec((2 * NSA_GROUPS, pp * page, HEAD_DIM), lambda bi, p, pt: (0, bi * steps + p, 0)),
    )
    return pl.pallas_call(
        functools.partial(_cmp_gather_kernel, inner=inner, page=page),
        grid_spec=grid_spec,
        out_shape=jax.ShapeDtypeStruct((2 * NSA_GROUPS, b * n_pages * page, HEAD_DIM), F32),
        compiler_params=_cparams(("parallel", "arbitrary")),
        name="nsa_cmp_gather",
    )(page_table, *([pool] * pp))


def _compress_kernel(x_ref, pos_ref, w1_ref, w2_ref, o_ref, *, rb):
    hidden = w1_ref.shape[2]
    hid = jnp.zeros((rb, hidden), F32)
    for c in range(0, CMP_BLOCK, 2):
        xc = [(x_ref[pl.ds(c + i, rb, stride=CMP_BLOCK), :] + pos_ref[c + i:c + i + 1, :]).astype(BF16) for i in (0, 1)]
        w = w1_ref[c:c + 2].reshape(2 * HEAD_DIM, hidden).astype(BF16)
        hid = hid + _dot(jnp.concatenate(xc, axis=1), w)
    hid = hid * _sigmoid(hid)
    o_ref[...] = _dot(hid.astype(BF16), w2_ref[...].astype(BF16))


def _compress(x, pos, w1, w2):
    r = x.shape[-2]
    nb = r // CMP_BLOCK
    rb = _row_tile(nb, 256)
    hidden = w1.shape[-1]
    w1v = w1.reshape(2, CMP_BLOCK, HEAD_DIM, hidden)
    if x.ndim == 2:
        x_spec = pl.BlockSpec((rb * CMP_BLOCK, HEAD_DIM), lambda c, i: (i, c))
    else:
        x_spec = pl.BlockSpec((None, rb * CMP_BLOCK, HEAD_DIM), lambda c, i: (c, i, 0))
    return pl.pallas_call(
        functools.partial(_compress_kernel, rb=rb),
        grid=(2 * NSA_GROUPS, nb // rb),
        in_specs=[
            x_spec,
            pl.BlockSpec((None, CMP_BLOCK, HEAD_DIM), lambda c, i: (c // NSA_GROUPS, 0, 0)),
            pl.BlockSpec((None, CMP_BLOCK, HEAD_DIM, hidden), lambda c, i: (c // NSA_GROUPS, 0, 0, 0)),
            pl.BlockSpec((None, hidden, HEAD_DIM), lambda c, i: (c // NSA_GROUPS, 0, 0)),
        ],
        out_specs=pl.BlockSpec((None, rb, HEAD_DIM), lambda c, i: (c, i, 0)),
        out_shape=jax.ShapeDtypeStruct((2 * NSA_GROUPS, nb, HEAD_DIM), F32),
        compiler_params=_cparams(("parallel", "arbitrary"), 48),
        name="nsa_compress",
    )(x, pos, w1v, w2)


def _gate_col(gl, col):
    lane = _iota(gl.shape, 1)
    return _sigmoid(jnp.sum(jnp.where(lane == col, gl, 0.0), axis=-1, keepdims=True))


def _nsa_cmp_kernel(q_ref, kc_ref, vc_ref, gate_ref, o_ref, sel_ref, *, past, tq, n_cmp, n_blk, t_keys, decode):
    g = pl.program_id(1)
    qi = pl.program_id(2)
    n_pad = kc_ref.shape[0]
    kc = kc_ref[...].astype(BF16)
    vc = vc_ref[...].astype(BF16)
    qpos = past + qi * tq + _iota((tq, 1), 0)
    blk = _iota((tq, n_pad), 1)
    visible = ((blk + 1) * CMP_BLOCK - 1 <= qpos) & (blk < n_cmp)
    gl = gate_ref[...]
    q_all = jnp.concatenate([q_ref[:, r * HEAD_DIM:(r + 1) * HEAD_DIM].astype(BF16) for r in range(NSA_REP)], axis=0)
    s = (_dot_nt(q_all, kc) * SCALE).reshape(NSA_REP, tq, n_pad)
    s = jnp.where(visible[None], s, NEG_INF)
    e = jnp.exp(s - jnp.max(s, axis=-1, keepdims=True))
    pr = jnp.where(visible[None], e / jnp.sum(e, axis=-1, keepdims=True), 0.0)
    imp = jnp.sum(pr, axis=0)
    gate = jnp.concatenate([_gate_col(gl, (g * NSA_REP + r) * 3) for r in range(NSA_REP)], axis=0)
    o = gate * _dot(pr.reshape(NSA_REP * tq, n_pad).astype(BF16), vc)
    o_ref[...] = jnp.concatenate([o[r * tq:(r + 1) * tq] for r in range(NSA_REP)], axis=1)

    cur = _div_pow2(qpos, CMP_BLOCK)
    forced = (blk == 0) | (blk == cur) | (blk == cur - 1)
    score = jnp.where(blk > cur, NEG_INF, jnp.where(forced, FORCE_SCORE, imp))
    if decode:
        row = score[0:1, :]
        col = jnp.sum(jnp.where(_iota((n_pad, n_pad), 0) == _iota((n_pad, n_pad), 1),
                                jnp.broadcast_to(row, (n_pad, n_pad)), 0.0), axis=1, keepdims=True)
        i_blk, j_blk = _iota((n_pad, n_pad), 0), _iota((n_pad, n_pad), 1)
        first = ((col > row) | ((col == row) & (i_blk < j_blk))) & (i_blk < n_blk)
        rank = jnp.sum(jnp.where(first, 1.0, 0.0), axis=0, keepdims=True)
        blk_f = blk[0:1, :].astype(F32)
        lane = _iota((1, N_SEL), 1)
        idx = jnp.zeros((1, N_SEL), F32)
        for k in range(N_SEL):
            ik = jnp.sum(jnp.where(rank == k, blk_f, 0.0), axis=1, keepdims=True)
            idx = jnp.where(lane == k, ik, idx)
        sel_ref[...] = jnp.broadcast_to(idx, sel_ref.shape).astype(I32)
    else:
        rank = jnp.zeros((tq, n_pad), F32)
        for i in range(n_blk):
            col = jnp.sum(jnp.where(blk == i, score, 0.0), axis=1, keepdims=True)
            first = (col > score) | ((col == score) & (i < blk))
            rank = rank + jnp.where(first, 1.0, 0.0)
        sel = jnp.where(rank < N_SEL, 1.0, 0.0).astype(BF16)
        expand = (_div_pow2(_iota((n_pad, t_keys), 1), CMP_BLOCK) == _iota((n_pad, t_keys), 0)).astype(BF16)
        sel_ref[...] = _dot(sel, expand).astype(BF16)


def _nsa_cmp(q, kc, vc, gates, past, n_keys, decode):
    b, t, w = q.shape
    n_pad = kc.shape[2]
    n_cmp = n_keys // CMP_BLOCK
    n_blk = -(-n_keys // CMP_BLOCK)
    assert N_SEL <= n_blk <= n_pad
    tq = min(t, 128)
    gw = w // NSA_GROUPS
    if decode:
        sel_shape, sel_block, sel_dtype = (b, NSA_GROUPS, t, N_SEL), (None, None, tq, N_SEL), I32
    else:
        sel_shape, sel_block, sel_dtype = (b, NSA_GROUPS, t, t), (None, None, tq, t), BF16
    return pl.pallas_call(
        functools.partial(_nsa_cmp_kernel, past=past, tq=tq, n_cmp=n_cmp, n_blk=n_blk, t_keys=t, decode=decode),
        grid=(b, NSA_GROUPS, t // tq),
        in_specs=[
            pl.BlockSpec((None, tq, gw), lambda bi, g, qi: (bi, qi, g)),
            pl.BlockSpec((None, None, n_pad, HEAD_DIM), lambda bi, g, qi: (bi, g, 0, 0)),
            pl.BlockSpec((None, None, n_pad, HEAD_DIM), lambda bi, g, qi: (bi, g, 0, 0)),
            pl.BlockSpec((None, tq, gates.shape[-1]), lambda bi, g, qi: (bi, qi, 0)),
        ],
        out_specs=[
            pl.BlockSpec((None, tq, gw), lambda bi, g, qi: (bi, qi, g)),
            pl.BlockSpec(sel_block, lambda bi, g, qi: (bi, g, qi, 0)),
        ],
        out_shape=[jax.ShapeDtypeStruct((b, t, w), F32), jax.ShapeDtypeStruct(sel_shape, sel_dtype)],
        compiler_params=_cparams(("parallel", "parallel", "arbitrary")),
        name="nsa_cmp_select",
    )(q, kc, vc, gates)


def _t5_tiles_kernel(tab_ref, o_ref, *, tq, tk, n_tiles):
    h = pl.program_id(0)
    d0 = _iota((tq, tk), 0) - _iota((tq, tk), 1)
    for t in range(n_tiles):
        bucket = _t5_bucket(d0 + t * tq)
        acc = jnp.zeros((tq, tk), F32)
        for n in range(N_BUCKETS):
            acc = jnp.where(bucket == n, tab_ref[n, h], acc)
        o_ref[t] = acc * LOG2E


def _t5_tiles(t5_bias, tq, tk):
    nh = t5_bias.shape[1]
    n_tiles = -(-(MAX_DISTANCE + tk - 1) // tq) + 1
    return pl.pallas_call(
        functools.partial(_t5_tiles_kernel, tq=tq, tk=tk, n_tiles=n_tiles),
        grid=(nh,),
        in_specs=[pl.BlockSpec(memory_space=pltpu.SMEM)],
        out_specs=pl.BlockSpec((None, n_tiles, tq, tk), lambda h: (h, 0, 0, 0)),
        out_shape=jax.ShapeDtypeStruct((nh, n_tiles, tq, tk), F32),
        compiler_params=_cparams(("arbitrary",)),
        name="t5_tiles",
    )(t5_bias)


def _nsa_attn_kernel(*refs, tq, windowed, branch):
    if windowed:
        q_ref, k_ref, v_ref, tiles_ref, gate_ref, prev_ref, o_ref, kb_ref, vb_ref = refs
        mask_ref = None
    else:
        q_ref, k_ref, v_ref, tiles_ref, gate_ref, prev_ref, mask_ref, o_ref, kb_ref, vb_ref = refs
    g = pl.program_id(1)
    qi = pl.program_id(2)
    t = k_ref.shape[0]

    @pl.when(qi == 0)
    def _():
        for c in range(t // tq):
            sl = slice(c * tq, (c + 1) * tq)
            kb_ref[sl, :] = k_ref[sl, :].astype(BF16)
            vb_ref[sl, :] = v_ref[sl, :].astype(BF16)

    rows = NSA_REP * tq
    q_all = jnp.concatenate([(q_ref[:, r * HEAD_DIM:(r + 1) * HEAD_DIM] * (SCALE * LOG2E)).astype(BF16)
                             for r in range(NSA_REP)], axis=0)
    tk = tiles_ref.shape[-1]
    per = tk // tq
    far = tiles_ref.shape[1] - 1
    dist0 = _iota((tq, tk), 0) - _iota((tq, tk), 1)

    def body(kc, carry):
        m, l, acc = carry
        ks = pl.multiple_of(kc * tk, tk)
        off = qi - kc * per
        dist = dist0 + off * tq
        ok = dist >= 0
        if windowed:
            ok = ok & (dist < WINDOW)
        else:
            ok = ok & (mask_ref[:, pl.ds(ks, tk)].astype(F32) > 0.5)
        s = _dot_nt(q_all, kb_ref[pl.ds(ks, tk), :]).reshape(NSA_REP, tq, tk)
        s = jnp.where(ok[None], s + tiles_ref[:, jnp.minimum(off, far)], NEG_INF).reshape(rows, tk)
        m_new = jnp.maximum(m, jnp.max(s, axis=-1, keepdims=True))
        p = jnp.exp2(s - m_new)
        if windowed:
            p = jnp.where(s > 0.5 * NEG_INF, p, 0.0)
        alpha = jnp.exp2(m - m_new)
        l = alpha * l + jnp.sum(p, axis=-1, keepdims=True)
        acc = alpha * acc + _dot(p.astype(BF16), vb_ref[pl.ds(ks, tk), :])
        return m_new, l, acc

    lo = jnp.maximum(lax.shift_right_arithmetic(qi * tq - (WINDOW - 1), tk.bit_length() - 1), 0) if windowed else 0
    init = (jnp.full((rows, 1), NEG_INF, F32), jnp.zeros((rows, 1), F32), jnp.zeros((rows, HEAD_DIM), F32))
    _, l, acc = lax.fori_loop(lo, qi // per + 1, body, init)
    gl = gate_ref[...]
    gate = jnp.concatenate([_gate_col(gl, (g * NSA_REP + r) * 3 + branch) for r in range(NSA_REP)], axis=0)
    o = gate * (acc / l)
    o_ref[...] = prev_ref[...] + jnp.concatenate([o[r * tq:(r + 1) * tq] for r in range(NSA_REP)], axis=1)


def _nsa_attn(q, kv, k_blk, v_blk, tiles, gates, prev, mask, branch):
    b, t, w = q.shape
    gw = w // NSA_GROUPS
    n_tiles, tq, tk = tiles.shape[1:]
    assert t % tk == 0 and tk % tq == 0 and tk & (tk - 1) == 0
    windowed = mask is None
    in_specs = [
        pl.BlockSpec((None, tq, gw), lambda bi, g, qi: (bi, qi, g)),
        pl.BlockSpec((None, t, HEAD_DIM), lambda bi, g, qi: (bi, 0, k_blk + g)),
        pl.BlockSpec((None, t, HEAD_DIM), lambda bi, g, qi: (bi, 0, v_blk + g)),
        pl.BlockSpec((NSA_REP, n_tiles, tq, tk), lambda bi, g, qi: (g, 0, 0, 0)),
        pl.BlockSpec((None, tq, gates.shape[-1]), lambda bi, g, qi: (bi, qi, 0)),
        pl.BlockSpec((None, tq, gw), lambda bi, g, qi: (bi, qi, g)),
    ]
    args = [q, kv, kv, tiles, gates, prev]
    if not windowed:
        in_specs.append(pl.BlockSpec((None, None, tq, t), lambda bi, g, qi: (bi, g, qi, 0)))
        args.append(mask)
    return pl.pallas_call(
        functools.partial(_nsa_attn_kernel, tq=tq, windowed=windowed, branch=branch),
        grid=(b, NSA_GROUPS, t // tq),
        in_specs=in_specs,
        out_specs=pl.BlockSpec((None, tq, gw), lambda bi, g, qi: (bi, qi, g)),
        out_shape=jax.ShapeDtypeStruct((b, t, w), F32),
        scratch_shapes=[pltpu.VMEM((t, HEAD_DIM), BF16), pltpu.VMEM((t, HEAD_DIM), BF16)],
        compiler_params=_cparams(("arbitrary", "arbitrary", "arbitrary")),
        name="nsa_window" if windowed else "nsa_select",
    )(*args)


def _t5_rows(tab_t, dist):
    n = dist.shape[1]
    onehot = (_iota((N_BUCKETS, n), 0) == _t5_bucket(dist)).astype(BF16)
    return _dot3(tab_t, onehot)


def _pick_row(x, r):
    return jnp.sum(jnp.where(_iota(x.shape, 0) == r, x, 0.0), axis=0, keepdims=True)


def _nsa_sel_dec_kernel(pt_ref, idx_ref, q_ref, *refs, past, n_past_blk, inner):
    kv_refs = refs[:N_SEL]
    new_ref, tab_ref, gate_ref, prev_ref, o_ref = refs[N_SEL:]
    bi, g = pl.program_id(0), pl.program_id(1)
    k_row = 2 * NSA_GROUPS + g
    v_row = 3 * NSA_GROUPS + g
    new = new_ref[...]
    k_new = jnp.broadcast_to(_pick_row(new, k_row), (CMP_BLOCK, HEAD_DIM))
    v_new = jnp.broadcast_to(_pick_row(new, v_row), (CMP_BLOCK, HEAD_DIM))
    ks, vs, dists = [], [], []
    for j in range(N_SEL):
        blk = idx_ref[(bi * NSA_GROUPS + g) * N_SEL + j]
        is_new = blk >= n_past_blk
        ks.append(jnp.where(is_new, k_new, kv_refs[j][pl.ds(k_row, CMP_BLOCK, stride=inner), :]).astype(BF16))
        vs.append(jnp.where(is_new, v_new, kv_refs[j][pl.ds(v_row, CMP_BLOCK, stride=inner), :]).astype(BF16))
        dists.append(past - (blk * CMP_BLOCK + _iota((1, CMP_BLOCK), 1)))
    dist = jnp.concatenate(dists, axis=1)
    ok = dist >= 0
    s = _dot_nt(q_ref[...].astype(BF16), jnp.concatenate(ks, axis=0)) * SCALE + _t5_rows(tab_ref[...], dist)
    s = jnp.where(ok, s, NEG_INF)
    p = jnp.where(ok, jnp.exp(s - jnp.max(s, axis=-1, keepdims=True)), 0.0)
    o = _dot(p.astype(BF16), jnp.concatenate(vs, axis=0)) / jnp.sum(p, axis=-1, keepdims=True)
    o_ref[...] = prev_ref[...] + _sigmoid(gate_ref[:, 1:2]) * o


def _nsa_sel_decode(q4, pool_blocks, kv_new, tab_t, gates4, prev4, sel_idx, page_table, past, page):
    b = q4.shape[0]
    assert past % page == 0 and page % CMP_BLOCK == 0
    n_past_blk = past // CMP_BLOCK
    per_page = page // CMP_BLOCK
    inner = pool_blocks.shape[1] // CMP_BLOCK

    def blk_spec(j):
        def index(bi, g, pt, idx):
            n = jnp.minimum(idx[(bi * NSA_GROUPS + g) * N_SEL + j], n_past_blk - 1)
            return (pt[bi, n // per_page] * per_page + n % per_page, 0, 0)
        return pl.BlockSpec((None, CMP_BLOCK * inner, HEAD_DIM), index)

    def grp(bi, g, pt, idx):
        return (bi, g, 0, 0)

    grid_spec = pltpu.PrefetchScalarGridSpec(
        num_scalar_prefetch=2,
        grid=(b, NSA_GROUPS),
        in_specs=[pl.BlockSpec((None, None, NSA_REP, HEAD_DIM), grp)] + [blk_spec(j) for j in range(N_SEL)] + [
            pl.BlockSpec((None, inner, HEAD_DIM), lambda bi, g, pt, idx: (bi, 0, 0)),
            pl.BlockSpec((NSA_REP, N_BUCKETS), lambda bi, g, pt, idx: (g, 0)),
            pl.BlockSpec((None, None, NSA_REP, 3), grp),
            pl.BlockSpec((None, None, NSA_REP, HEAD_DIM), grp),
        ],
        out_specs=pl.BlockSpec((None, None, NSA_REP, HEAD_DIM), grp),
    )
    return pl.pallas_call(
        functools.partial(_nsa_sel_dec_kernel, past=past, n_past_blk=n_past_blk, inner=inner),
        grid_spec=grid_spec,
        out_shape=jax.ShapeDtypeStruct(q4.shape, F32),
        compiler_params=_cparams(("parallel", "parallel")),
        name="nsa_select_decode",
    )(page_table, sel_idx, q4, *([pool_blocks] * N_SEL), kv_new, tab_t, gates4, prev4)


def _nsa_win_dec_kernel(q_ref, win_ref, new_ref, tab_ref, gate_ref, prev_ref, o_ref, *, lw):
    g = pl.program_id(1)
    inner = 2 * NSA_GROUPS
    q = q_ref[...].astype(BF16)
    k = win_ref[pl.ds(g, lw, stride=inner), :].astype(BF16)
    v = win_ref[pl.ds(NSA_GROUPS + g, lw, stride=inner), :].astype(BF16)
    new = new_ref[...]
    dist = lw - _iota((1, lw), 1)
    ok = dist < WINDOW
    tab = tab_ref[...]
    s = jnp.where(ok, _dot_nt(q, k) * SCALE + _t5_rows(tab, dist), NEG_INF)
    kn = _pick_row(new, g).astype(BF16).astype(F32)
    s_new = jnp.sum(q.astype(F32) * kn, axis=-1, keepdims=True) * SCALE + tab[:, 0:1]
    m = jnp.maximum(jnp.max(s, axis=-1, keepdims=True), s_new)
    p = jnp.where(ok, jnp.exp(s - m), 0.0)
    p_new = jnp.exp(s_new - m)
    l = jnp.sum(p, axis=-1, keepdims=True) + p_new
    o = (_dot(p.astype(BF16), v) + p_new * _pick_row(new, NSA_GROUPS + g)) / l
    o_ref[...] = prev_ref[...] + _sigmoid(gate_ref[:, 2:3]) * o


def _nsa_win_decode(q4, win_past, win_new, tab_t, gates4, prev4):
    b = q4.shape[0]
    inner = 2 * NSA_GROUPS
    lw = win_past.shape[1] // inner
    assert lw == WINDOW

    def grp(bi, g):
        return (bi, g, 0, 0)

    return pl.pallas_call(
        functools.partial(_nsa_win_dec_kernel, lw=lw),
        grid=(b, NSA_GROUPS),
        in_specs=[
            pl.BlockSpec((None, None, NSA_REP, HEAD_DIM), grp),
            pl.BlockSpec((None, lw * inner, HEAD_DIM), lambda bi, g: (bi, 0, 0)),
            pl.BlockSpec((None, inner, HEAD_DIM), lambda bi, g: (bi, 0, 0)),
            pl.BlockSpec((NSA_REP, N_BUCKETS), lambda bi, g: (g, 0)),
            pl.BlockSpec((None, None, NSA_REP, 3), grp),
            pl.BlockSpec((None, None, NSA_REP, HEAD_DIM), grp),
        ],
        out_specs=pl.BlockSpec((None, None, NSA_REP, HEAD_DIM), grp),
        out_shape=jax.ShapeDtypeStruct(q4.shape, F32),
        compiler_params=_cparams(("parallel", "parallel")),
        name="nsa_window_decode",
    )(q4, win_past, win_new, tab_t, gates4, prev4)


def _lower_bound(lbp, layer, axis):
    e = jnp.exp(lbp - jnp.max(lbp, axis=axis, keepdims=True))
    sm = e / jnp.sum(e, axis=axis, keepdims=True)
    idx = _iota(lbp.shape, axis)
    return jnp.sum(jnp.where((idx >= 1) & (idx <= layer), sm, 0.0), axis=axis, keepdims=True)


def _gla_kernel(q_ref, f_ref, i_ref, lbp_ref, s0_ref, ng_ref, o_ref, s_ref, st_ref, *, layer, chunk, n_chunks, hp):
    ti = pl.program_id(2)
    lb = _lower_bound(lbp_ref[...], layer, 0)
    tril = _iota((chunk, chunk), 0) >= _iota((chunk, chunk), 1)
    tril_b = tril.astype(BF16)
    ng = ng_ref[...]

    @pl.when(ti == 0)
    def _():
        for r in range(hp):
            st_ref[r] = s0_ref[r].T

    def heads(x):
        return jnp.stack([x[:, r * HEAD_DIM:(r + 1) * HEAD_DIM] for r in range(hp)], axis=0)

    def bdot(a, b, ca, cb):
        return lax.dot_general(a, b, (((ca,), (cb,)), ((0,), (0,))), preferred_element_type=F32)

    def body(c, _):
        r0 = pl.multiple_of(c * chunk, chunk)
        qv = q_ref[pl.ds(r0, chunk), :]
        q = qv * _sigmoid(qv)
        g = lb + (1.0 - lb) * _sigmoid(f_ref[pl.ds(r0, chunk), :])
        k = 1.0 - g
        b = _dot3_rhs(tril_b, jnp.log(g))
        b_last = b[chunk - 1:chunk, :]
        qe = heads((q * jnp.exp(b)).astype(BF16))
        ke = heads((k * jnp.exp(-b)).astype(BF16))
        kt = heads((k * jnp.exp(b_last - b)).astype(BF16))
        v = heads(i_ref[pl.ds(r0, chunk), :].astype(BF16))
        st = st_ref[...]
        att = jnp.where(tril[None], bdot(qe, ke, 2, 2), 0.0)
        o = bdot(att.astype(BF16), v, 2, 1) + bdot(qe, st.astype(BF16), 2, 2)
        st_ref[...] = heads(jnp.exp(b_last)) * st + bdot(v, kt, 1, 1)
        inv = lax.rsqrt(jnp.mean(o * o, axis=-1, keepdims=True) + RMS_EPS)
        o = o * inv * ng
        o_ref[pl.ds(r0, chunk), :] = jnp.concatenate([o[r] for r in range(hp)], axis=1)
        return 0

    lax.fori_loop(0, n_chunks, body, 0)

    @pl.when(ti == pl.num_programs(2) - 1)
    def _():
        for r in range(hp):
            s_ref[r] = st_ref[r].T


def _gla(q, f, i_in, lbp, s0, norm_g, layer, hp=16, tt=512):
    b, t, w = q.shape
    nh = w // HEAD_DIM
    chunk = math.gcd(t, GLA_CHUNK)
    depth = lbp.shape[0]
    tt = min(tt, t)
    assert nh % hp == 0 and t % tt == 0 and tt % chunk == 0
    tok = pl.BlockSpec((None, tt, hp * HEAD_DIM), lambda bi, h, ti: (bi, ti, h))
    st = pl.BlockSpec((None, hp, HEAD_DIM, HEAD_DIM), lambda bi, h, ti: (bi, h, 0, 0))
    return pl.pallas_call(
        functools.partial(_gla_kernel, layer=layer, chunk=chunk, n_chunks=tt // chunk, hp=hp),
        grid=(b, nh // hp, t // tt),
        in_specs=[tok, tok, tok, pl.BlockSpec((depth, hp * HEAD_DIM), lambda bi, h, ti: (0, h)), st,
                  pl.BlockSpec((1, HEAD_DIM), lambda bi, h, ti: (0, 0))],
        out_specs=[tok, st],
        out_shape=[jax.ShapeDtypeStruct((b, t, w), F32), jax.ShapeDtypeStruct(s0.shape, F32)],
        scratch_shapes=[pltpu.VMEM((hp, HEAD_DIM, HEAD_DIM), F32)],
        compiler_params=_cparams(("arbitrary", "arbitrary", "arbitrary")),
        name="hgrn_gla",
    )(q, f, i_in, lbp, s0, norm_g.reshape(1, HEAD_DIM))


def _gla_dec_kernel(q_ref, f_ref, v_ref, lbp_ref, s0_ref, ng_ref, o_ref, s_ref, *, layer):
    lb = _lower_bound(lbp_ref[...], layer, 0)[0]
    qv = q_ref[...]
    q = qv * _sigmoid(qv)
    g = lb + (1.0 - lb) * _sigmoid(f_ref[...])
    s = g * s0_ref[...] + (1.0 - g) * v_ref[...]
    s_ref[...] = s
    o = jnp.sum(q * s, axis=1, keepdims=True)
    inv = lax.rsqrt(jnp.mean(o * o, axis=-1, keepdims=True) + RMS_EPS)
    o_ref[...] = o * inv * ng_ref[...]


def _gla_decode(q, f, i_in, lbp, s0, norm_g, layer):
    b, nh = s0.shape[:2]
    depth = lbp.shape[0]
    col = pl.BlockSpec((None, nh, HEAD_DIM, 1), lambda bi: (bi, 0, 0, 0))
    row = pl.BlockSpec((None, nh, 1, HEAD_DIM), lambda bi: (bi, 0, 0, 0))
    st = pl.BlockSpec((None, nh, HEAD_DIM, HEAD_DIM), lambda bi: (bi, 0, 0, 0))
    o, s = pl.pallas_call(
        functools.partial(_gla_dec_kernel, layer=layer),
        grid=(b,),
        in_specs=[col, col, row, pl.BlockSpec((depth, nh, HEAD_DIM, 1), lambda bi: (0, 0, 0, 0)), st,
                  pl.BlockSpec((1, HEAD_DIM), lambda bi: (0, 0))],
        out_specs=[row, st],
        out_shape=[jax.ShapeDtypeStruct((b, nh, 1, HEAD_DIM), F32), jax.ShapeDtypeStruct(s0.shape, F32)],
        compiler_params=_cparams(("parallel",)),
        name="hgrn_step",
    )(q.reshape(b, nh, HEAD_DIM, 1), f.reshape(b, nh, HEAD_DIM, 1), i_in.reshape(b, nh, 1, HEAD_DIM),
      lbp.reshape(depth, nh, HEAD_DIM, 1), s0, norm_g.reshape(1, HEAD_DIM))
    return o.reshape(b, 1, nh * HEAD_DIM), s


def _as_rows(w):
    if w.shape[1] % 128:
        return jnp.swapaxes(w, 0, 1), True
    return w, False


def _fox_layer(x, bt, norm_g, w_in, b_f, past):
    b, t = bt
    d = x.shape[1]
    nh = d // HEAD_DIM
    wt, tr = _as_rows(w_in)
    assert tr
    q, k, v, z = _norm_proj(x, norm_g, wt, [(n * d, d) for n in range(4)], 512, True)
    (fl,) = _norm_proj(x, norm_g, wt, [(4 * d, nh)], nh, True)
    if past is None:
        q3, k3, v3 = (a.reshape(b, t, d) for a in (q, k, v))
        logf, cum = _fox_prep(fl.reshape(b, t, nh), b_f)
        o = _fox_attn(q3, k3, v3, cum)
    else:
        k_pool, v_pool, lf_pool, page_table = past
        npool, page = k_pool.shape[:2]
        o, logf = _fox_decode(q.reshape(b, nh, HEAD_DIM), k.reshape(b, nh, HEAD_DIM), v.reshape(b, nh, HEAD_DIM),
                              fl.reshape(b, nh, 1), b_f, k_pool.reshape(npool, page * nh, HEAD_DIM),
                              v_pool.reshape(npool, page * nh, HEAD_DIM), jnp.swapaxes(lf_pool, 1, 2), page_table)
    state = (k.reshape(1, b, t, nh, HEAD_DIM), v.reshape(1, b, t, nh, HEAD_DIM), logf.reshape(1, b, t, nh))
    return o.reshape(b * t, d), z, state


def _pad_summaries(summaries, b, n_cmp, n_pad):
    s = summaries.reshape(2, NSA_GROUPS, b, n_cmp, HEAD_DIM)
    return jnp.pad(jnp.swapaxes(s, 1, 2), ((0, 0), (0, 0), (0, 0), (0, n_pad - n_cmp), (0, 0)))


def _nsa_layer(x, bt, norm_g, w_in, cmp_pos, cmp_w1, cmp_w2, t5_bias, past):
    b, t = bt
    d = x.shape[1]
    kvw = NSA_GROUPS * HEAD_DIM
    off_gate = d + 6 * kvw
    n_gate = 3 * (d // HEAD_DIM)
    wt, tr = _as_rows(w_in)
    assert tr
    q, kv4, win = _norm_proj(x, norm_g, wt, [(0, d), (d, 4 * kvw), (d + 4 * kvw, 2 * kvw)], 512, True)
    (gates,) = _norm_proj(x, norm_g, wt[off_gate:off_gate + n_gate], [(0, n_gate)], n_gate, True)
    (z,) = _norm_proj(x, norm_g, wt[off_gate + n_gate:], [(0, d)], 512, True)
    q3, gates3 = q.reshape(b, t, d), gates.reshape(b, t, n_gate)
    inner = NSA_KINDS * NSA_GROUPS
    if past is None:
        kv3, win3 = kv4.reshape(b, t, 4 * kvw), win.reshape(b, t, 2 * kvw)
        n_cmp = t // CMP_BLOCK
        n_pad = -(-n_cmp // 128) * 128
        summaries = _pad_summaries(_compress(kv4, cmp_pos, cmp_w1, cmp_w2), b, n_cmp, n_pad)
        o, mask = _nsa_cmp(q3, summaries[0], summaries[1], gates3, 0, t, False)
        o = _nsa_attn(q3, kv3, 4, 6, _t5_tiles(t5_bias, 128, math.gcd(t, 512)), gates3, o, mask, 1)
        o = _nsa_attn(q3, win3, 0, 2, _t5_tiles(t5_bias, 128, math.gcd(t, 256)), gates3, o, None, 2)
        keep = min(WINDOW, t)
        win_state = win3[:, t - keep:]
    else:
        kv_pool, win_past, page_table = past
        assert t == 1
        npool, page = kv_pool.shape[:2]
        n_pages = page_table.shape[1]
        plen = n_pages * page
        pool = kv_pool.reshape(npool, page * inner, HEAD_DIM)
        n_cmp = plen // CMP_BLOCK
        n_pad = -(-(n_cmp + 1) // 128) * 128
        summaries = _pad_summaries(_compress(_cmp_gather(pool, page_table, page), cmp_pos, cmp_w1, cmp_w2),
                                   b, n_cmp, n_pad)
        rows = SUBLANES
        q_pad = jnp.pad(q3, ((0, 0), (0, rows - t), (0, 0)))
        g_pad = jnp.pad(gates3, ((0, 0), (0, rows - t), (0, 0)))
        o, sel_idx = _nsa_cmp(q_pad, summaries[0], summaries[1], g_pad, plen, plen + t, True)
        o4 = o[:, :1].reshape(b, NSA_GROUPS, NSA_REP, HEAD_DIM)
        sel_idx = sel_idx[:, :, 0, :].reshape(-1)
        q4 = q3.reshape(b, NSA_GROUPS, NSA_REP, HEAD_DIM)
        gates4 = gates3.reshape(b, NSA_GROUPS, NSA_REP, 3)
        tab_t = t5_bias.T
        blocks = pool.reshape(npool * page // CMP_BLOCK, CMP_BLOCK * inner, HEAD_DIM)
        o4 = _nsa_sel_decode(q4, blocks, kv4.reshape(b, inner, HEAD_DIM), tab_t, gates4, o4, sel_idx,
                             page_table, plen, page)
        lw = win_past.shape[1]
        wp = win_past.reshape(b, lw * 2 * NSA_GROUPS, HEAD_DIM)
        wn = win.reshape(b, 2 * NSA_GROUPS, HEAD_DIM)
        o4 = _nsa_win_decode(q4, wp, wn, tab_t, gates4, o4)
        o = o4.reshape(b, t, d)
        keep = min(WINDOW, lw + t)
        win_state = jnp.concatenate([wp, wn], axis=1)[:, (lw + t - keep) * 2 * NSA_GROUPS:]
    state = (kv4.reshape(1, b, t, NSA_KINDS, NSA_GROUPS, HEAD_DIM),
             win_state.reshape(1, b, -1, 2, NSA_GROUPS, HEAD_DIM))
    return o.reshape(b * t, d), z, state


def _hgrn_layer(x, bt, norm_g, w_in, lbp, hgrn_norm_g, layer, s0):
    b, t = bt
    d = x.shape[1]
    q, f, i_in, z = _norm_proj(x, norm_g, w_in, [(n * d, d) for n in range(4)], 512)
    if t == 1:
        o, s_new = _gla_decode(q, f, i_in, lbp, s0, hgrn_norm_g, layer)
    else:
        o, s_new = _gla(q.reshape(b, t, d), f.reshape(b, t, d), i_in.reshape(b, t, d), lbp, s0, hgrn_norm_g, layer)
    return o.reshape(b * t, d), z, (s_new[None],)


def _sb_layer(x, bt, norm_g, w_in, past):
    b, t = bt
    d = x.shape[1]
    nh = d // HEAD_DIM
    q, k, v, z = _norm_proj(x, norm_g, w_in, [(n * d, d) for n in range(4)], 512)
    if past is None:
        o = _sb_attn(q.reshape(b, t, d), k.reshape(b, t, d), v.reshape(b, t, d))
    else:
        k_pool, v_pool, page_table = past
        npool, page = k_pool.shape[:2]
        o = _sb_decode(q.reshape(b, nh, HEAD_DIM), k_pool.reshape(npool, page * nh, HEAD_DIM),
                       v_pool.reshape(npool, page * nh, HEAD_DIM), page_table, page)
    state = (k.reshape(1, b, t, nh, HEAD_DIM), v.reshape(1, b, t, nh, HEAD_DIM))
    return o.reshape(b * t, d), z, state


def kernel(x_prompt, x_sample, cache_fox_k, cache_fox_v, cache_fox_logf, cache_nsa_kv, state_nsa_win, state_hgrn, cache_sb_k, cache_sb_v, page_table, p_prompt, p_sample, norm_g, final_norm_g, ple_gate_w, ple_proj_w, fox_w_in, fox_b_f, fox_w_out, nsa_w_in, nsa_cmp_pos, nsa_cmp_w1, nsa_cmp_w2, t5_bias, nsa_w_out, hgrn_w_in, hgrn_lower_bounds, hgrn_norm_g, hgrn_w_out, sb_w_in, sb_w_out):
    depth = norm_g.shape[0]
    assert depth == 4 and all(a.shape[0] == 1 for a in (fox_w_in, nsa_w_in, hgrn_w_in, sb_w_in))
    w_out = [w[0].astype(BF16) for w in (fox_w_out, nsa_w_out, hgrn_w_out, sb_w_out)]
    w_gate = [ple_gate_w[i].astype(BF16) for i in range(depth)]
    w_proj = [ple_proj_w[i].astype(BF16) for i in range(depth)]

    def trunk(x3, p4, decode):
        b, t, d = x3.shape
        bt = (b, t)
        x = x3.reshape(b * t, d)
        p = p4.reshape(depth, b * t, -1)
        if decode:
            past = ((cache_fox_k[0], cache_fox_v[0], cache_fox_logf[0], page_table),
                    (cache_nsa_kv[0], state_nsa_win[0], page_table),
                    state_hgrn[0],
                    (cache_sb_k[0], cache_sb_v[0], page_table))
        else:
            past = (None, None, jnp.zeros((b,) + state_hgrn.shape[2:], F32), None)
        o, z, st_a = _fox_layer(x, bt, norm_g[0], fox_w_in[0], fox_b_f[0], past[0])
        x = _layer_tail(o, z, x, p[0], w_out[0], w_gate[0], w_proj[0])
        o, z, st_b = _nsa_layer(x, bt, norm_g[1], nsa_w_in[0], nsa_cmp_pos[0], nsa_cmp_w1[0], nsa_cmp_w2[0],
                                t5_bias, past[1])
        x = _layer_tail(o, z, x, p[1], w_out[1], w_gate[1], w_proj[1])
        o, z, st_c = _hgrn_layer(x, bt, norm_g[2], hgrn_w_in[0], hgrn_lower_bounds, hgrn_norm_g[0], 2, past[2])
        x = _layer_tail(o, z, x, p[2], w_out[2], w_gate[2], w_proj[2])
        o, z, st_d = _sb_layer(x, bt, norm_g[3], sb_w_in[0], past[3])
        y = _layer_tail(o, z, x, p[3], w_out[3], w_gate[3], w_proj[3], final_norm_g).reshape(b, t, d)
        return y, st_a, st_b, st_c, st_d

    yp, pa, pb, pc, pd = trunk(x_prompt, p_prompt, False)
    ys, sa, sb, sc, sd = trunk(x_sample, p_sample, True)
    return (yp, ys, pa[0], sa[0], pa[1], sa[1], pa[2], sa[2], pb[0], sb[0], pb[1], sb[1],
            pc[0], sc[0], pd[0], sd[0], pd[1], sd[1])
```

```python
import functools
import math

import jax
import jax.numpy as jnp
from jax import lax
from jax.experimental import pallas as pl
from jax.experimental.pallas import tpu as pltpu

F32 = jnp.float32
BF16 = jnp.bfloat16
I32 = jnp.int32

HEAD_DIM = 128
RMS_EPS = 1e-6
NEG_INF = -1e30
FORCE_SCORE = 1e9
CMP_BLOCK = 64
N_SEL = 16
WINDOW = 512
N_BUCKETS = 32
MAX_DISTANCE = 128
GLA_CHUNK = 32
NSA_GROUPS = 2
NSA_REP = 8
NSA_KINDS = 4
SUBLANES = 8
SCALE = HEAD_DIM ** -0.5
LOG2E = 1.0 / math.log(2.0)
SB_DEAD_TAIL = -160.0
MIB = 1 << 20


def _cparams(sem, vmem_mib=None):
    kw = dict(dimension_semantics=sem)
    if vmem_mib is not None:
        kw["vmem_limit_bytes"] = vmem_mib * MIB
    return pltpu.CompilerParams(**kw)


def _iota(shape, axis):
    return lax.broadcasted_iota(I32, shape, axis)


def _dot(a, b):
    return jnp.dot(a, b, preferred_element_type=F32)


def _dot_nt(a, b):
    return lax.dot_general(a, b, (((1,), (1,)), ((), ())), preferred_element_type=F32)


def _dot_tn(a, b):
    return lax.dot_general(a, b, (((0,), (0,)), ((), ())), preferred_element_type=F32)


def _split3(x):
    hi = x.astype(BF16)
    r1 = x - hi.astype(F32)
    mid = r1.astype(BF16)
    lo = (r1 - mid.astype(F32)).astype(BF16)
    return hi, mid, lo


def _dot3(x, m, dot=_dot):
    hi, mid, lo = _split3(x)
    return dot(hi, m) + dot(mid, m) + dot(lo, m)


def _dot3_rhs(m, x, dot=_dot):
    hi, mid, lo = _split3(x)
    return dot(m, hi) + dot(m, mid) + dot(m, lo)


def _log_sigmoid_pair(x):
    sp = jnp.log1p(jnp.exp(-jnp.abs(x)))
    return jnp.minimum(x, 0.0) - sp, jnp.minimum(-x, 0.0) - sp


def _log2_sigmoid_pair(x2):
    sp = jnp.log(1.0 + jnp.exp2(-jnp.abs(x2))) * LOG2E
    pos = jnp.minimum(x2, 0.0) - sp
    return pos, pos - x2


def _sigmoid(x):
    return jax.nn.sigmoid(x)


def _t5_bucket(dist):
    exact = N_BUCKETS // 2
    d = jnp.maximum(dist, 0)
    logd = jnp.log(jnp.maximum(d, 1).astype(F32) * (1.0 / exact))
    large = exact + (logd * ((N_BUCKETS - exact) / math.log(MAX_DISTANCE / exact))).astype(I32)
    return jnp.where(d < exact, d, jnp.minimum(large, N_BUCKETS - 1))


def _div_pow2(x, n):
    assert n & (n - 1) == 0
    return lax.shift_right_arithmetic(x, n.bit_length() - 1)


def _row_tile(m, cap):
    t = min(m, cap)
    assert m % t == 0, (m, t)
    return t


def _norm_proj_kernel(x_ref, g_ref, w_ref, *rest, slab_blocks, transposed):
    outs, u_ref = rest[:-1], rest[-1]
    j = pl.program_id(1)

    @pl.when(j == 0)
    def _():
        x = x_ref[...]
        inv = lax.rsqrt(jnp.mean(x * x, axis=-1, keepdims=True) + RMS_EPS)
        u_ref[...] = (x * inv * g_ref[...]).astype(BF16)

    w = w_ref[...].astype(BF16)
    acc = _dot_nt(u_ref[...], w) if transposed else _dot(u_ref[...], w)
    for (lo, n), o_ref in zip(slab_blocks, outs):
        @pl.when((j >= lo) & (j < lo + n))
        def _(o_ref=o_ref):
            o_ref[...] = acc


def _norm_proj(x, g, w, slabs, tn, transposed=False):
    m, d = x.shape
    tm = _row_tile(m, 1024)
    base = slabs[0][0]
    slab_blocks, pos = [], base
    for off, width in slabs:
        assert off == pos and width % tn == 0 and off % tn == 0, (off, width, tn)
        slab_blocks.append(((off - base) // tn, width // tn))
        pos += width
    nj = (pos - base) // tn
    b0 = base // tn

    def out_map(lo, n):
        return lambda i, j: (i, jnp.minimum(jnp.maximum(j - lo, 0), n - 1))

    if transposed:
        w_spec = pl.BlockSpec((tn, d), lambda i, j: (b0 + j, 0))
    else:
        w_spec = pl.BlockSpec((d, tn), lambda i, j: (0, b0 + j))
    return pl.pallas_call(
        functools.partial(_norm_proj_kernel, slab_blocks=tuple(slab_blocks), transposed=transposed),
        grid=(m // tm, nj),
        in_specs=[pl.BlockSpec((tm, d), lambda i, j: (i, 0)), pl.BlockSpec((1, d), lambda i, j: (0, 0)), w_spec],
        out_specs=[pl.BlockSpec((tm, tn), out_map(lo, n)) for lo, n in slab_blocks],
        out_shape=[jax.ShapeDtypeStruct((m, width), F32) for _, width in slabs],
        scratch_shapes=[pltpu.VMEM((tm, d), BF16)],
        compiler_params=_cparams(("parallel", "arbitrary"), 56),
        name="norm_proj",
    )(x, g.reshape(1, d), w)


def _layer_tail_kernel(o_ref, z_ref, x_ref, p_ref, wo_ref, wg_ref, wp_ref, *rest):
    z = z_ref[...]
    a = (o_ref[...] * (z * _sigmoid(z))).astype(BF16)
    h = x_ref[...] + _dot(a, wo_ref[...])
    gate = _sigmoid(_dot(h.astype(BF16), wg_ref[...]))
    y = h + gate * _dot(p_ref[...].astype(BF16), wp_ref[...])
    if len(rest) == 2:
        g_ref, out_ref = rest
        y = y * lax.rsqrt(jnp.mean(y * y, axis=-1, keepdims=True) + RMS_EPS) * g_ref[...]
    else:
        (out_ref,) = rest
    out_ref[...] = y


def _layer_tail(o, z, x, p, wo, wg, wp, final_g=None):
    m, d = o.shape
    pd = p.shape[1]
    tm = _row_tile(m, 256)
    row = pl.BlockSpec((tm, d), lambda i: (i, 0))

    def resident(shape):
        return pl.BlockSpec(shape, lambda i: (0, 0), pipeline_mode=pl.Buffered(1))

    in_specs = [row, row, row, pl.BlockSpec((tm, pd), lambda i: (i, 0)),
                resident((d, d)), resident((d, d)), resident((pd, d))]
    args = [o, z, x, p, wo, wg, wp]
    if final_g is not None:
        in_specs.append(pl.BlockSpec((1, d), lambda i: (0, 0)))
        args.append(final_g.reshape(1, d))
    return pl.pallas_call(
        _layer_tail_kernel,
        grid=(m // tm,),
        in_specs=in_specs,
        out_specs=row,
        out_shape=jax.ShapeDtypeStruct((m, d), F32),
        compiler_params=_cparams(("parallel",), 48),
        name="layer_tail",
    )(*args)


def _fox_prep_kernel(x_ref, bf_ref, logf_ref, cum_ref, *, n_chunks):
    tril = (_iota((128, 128), 0) >= _iota((128, 128), 1)).astype(BF16)
    carry = jnp.zeros((1, x_ref.shape[1]), F32)
    for c in range(n_chunks):
        sl = slice(c * 128, (c + 1) * 128)
        lf, _ = _log_sigmoid_pair(x_ref[sl, :] + bf_ref[...])
        logf_ref[sl, :] = lf
        cum = _dot3_rhs(tril, lf) + carry
        cum_ref[sl, :] = cum
        carry = cum[127:128, :]


def _fox_prep(logits, b_f):
    b, t, h = logits.shape
    assert t % 128 == 0
    spec = pl.BlockSpec((None, t, h), lambda i: (i, 0, 0))
    return pl.pallas_call(
        functools.partial(_fox_prep_kernel, n_chunks=t // 128),
        grid=(b,),
        in_specs=[spec, pl.BlockSpec((1, h), lambda i: (0, 0))],
        out_specs=[spec, spec],
        out_shape=[jax.ShapeDtypeStruct((b, t, h), F32)] * 2,
        compiler_params=_cparams(("parallel",)),
        name="fox_prep",
    )(logits, b_f.reshape(1, h))


def _pick_col(x, c):
    return jnp.sum(jnp.where(_iota(x.shape, 1) == c, x, 0.0), axis=1, keepdims=True)


def _bias_lanes(col, ones_first):
    rows = col.shape[0]
    hi, mid, lo = (p.astype(F32) for p in _split3(col))
    lane = _iota((rows, HEAD_DIM), 1)
    one, val = (0, 3) if ones_first else (3, 0)
    x = jnp.where(lane == val, hi, jnp.where(lane == val + 1, mid, jnp.where(lane == val + 2, lo, 0.0)))
    x = jnp.where((lane >= one) & (lane < one + 3), 1.0, x)
    return x.astype(BF16)


def _fox_attn_kernel(q_ref, k_ref, v_ref, cq_ref, ck_ref, o_ref, ka_ref, vb_ref, *, tq):
    h = pl.program_id(1)
    qi = pl.program_id(2)
    t = k_ref.shape[0]

    @pl.when(qi == 0)
    def _():
        for c in range(t // tq):
            sl = slice(c * tq, (c + 1) * tq)
            ka_ref[sl, :HEAD_DIM] = k_ref[sl, :].astype(BF16)
            ka_ref[sl, HEAD_DIM:] = _bias_lanes(_pick_col(ck_ref[sl, :], h) * -LOG2E, True)
            vb_ref[sl, :] = v_ref[sl, :].astype(BF16)

    qa = jnp.concatenate([(q_ref[...] * (SCALE * LOG2E)).astype(BF16),
                          _bias_lanes(_pick_col(cq_ref[...], h) * LOG2E, False)], axis=1)
    tri = _iota((tq, tq), 1) <= _iota((tq, tq), 0)

    def step(kc, carry, diagonal):
        m, l, acc = carry
        ks = pl.multiple_of(kc * tq, tq)
        s = _dot_nt(qa, ka_ref[pl.ds(ks, tq), :])
        if diagonal:
            s = jnp.where(tri, s, NEG_INF)
        m_new = jnp.maximum(m, jnp.max(s, axis=-1, keepdims=True))
        p = jnp.exp2(s - m_new)
        alpha = jnp.exp2(m - m_new)
        l = alpha * l + jnp.sum(p, axis=-1, keepdims=True)
        acc = alpha * acc + _dot(p.astype(BF16), vb_ref[pl.ds(ks, tq), :])
        return m_new, l, acc

    init = (jnp.full((tq, 1), NEG_INF, F32), jnp.zeros((tq, 1), F32), jnp.zeros((tq, HEAD_DIM), F32))
    carry = lax.fori_loop(0, qi, lambda kc, c: step(kc, c, False), init)
    _, l, acc = step(qi, carry, True)
    o_ref[...] = acc / l


def _fox_attn(q, k, v, cum, tq=512):
    b, t, w = q.shape
    h = w // HEAD_DIM
    tq = min(tq, t)
    assert t % tq == 0
    return pl.pallas_call(
        functools.partial(_fox_attn_kernel, tq=tq),
        grid=(b, h, t // tq),
        in_specs=[
            pl.BlockSpec((None, tq, HEAD_DIM), lambda bi, hi, qi: (bi, qi, hi)),
            pl.BlockSpec((None, t, HEAD_DIM), lambda bi, hi, qi: (bi, 0, hi)),
            pl.BlockSpec((None, t, HEAD_DIM), lambda bi, hi, qi: (bi, 0, hi)),
            pl.BlockSpec((None, tq, h), lambda bi, hi, qi: (bi, qi, 0)),
            pl.BlockSpec((None, t, h), lambda bi, hi, qi: (bi, 0, 0)),
        ],
        out_specs=pl.BlockSpec((None, tq, HEAD_DIM), lambda bi, hi, qi: (bi, qi, hi)),
        out_shape=jax.ShapeDtypeStruct((b, t, w), F32),
        scratch_shapes=[pltpu.VMEM((t, 2 * HEAD_DIM), BF16), pltpu.VMEM((t, HEAD_DIM), BF16)],
        compiler_params=_cparams(("arbitrary", "arbitrary", "arbitrary")),
        name="fox_attn",
    )(q, k, v, cum, cum)


def _sb_attn_kernel(q_ref, k_ref, v_ref, o_ref, kb_ref, vb_ref, *, tq):
    qi = pl.program_id(2)
    t = k_ref.shape[0]
    sub = HEAD_DIM

    @pl.when(qi == 0)
    def _():
        for c in range(t // tq):
            sl = slice(c * tq, (c + 1) * tq)
            kb_ref[sl, :] = k_ref[sl, :].astype(BF16)
            vb_ref[sl, :] = v_ref[sl, :].astype(BF16)

    q = (q_ref[...] * (SCALE * LOG2E)).astype(BF16)
    later = (_iota((sub, sub), 0) > _iota((sub, sub), 1)).astype(BF16)
    scan = jnp.concatenate([later, jnp.ones((sub, sub), BF16)], axis=1)
    scan = jnp.concatenate([scan, scan], axis=0)
    before = _iota((tq, tq), 1) < _iota((tq, tq), 0)

    def step(kc, carry, diagonal):
        tail, acc = carry
        ks = pl.multiple_of(kc * tq, tq)
        x = _dot_nt(q, kb_ref[pl.ds(ks, tq), :])
        ls_pos, ls_neg = _log2_sigmoid_pair(x)
        if diagonal:
            ls_neg = jnp.where(before, ls_neg, 0.0)
        ws = [None] * (tq // sub)
        for j in reversed(range(tq // sub)):
            sl = slice(j * sub, (j + 1) * sub)
            stay = ls_neg[:, sl]
            hi = stay.astype(BF16)
            lo = (stay - hi.astype(F32)).astype(BF16)
            sums = _dot(jnp.concatenate([hi, lo], axis=1), scan)
            w = jnp.exp2(ls_pos[:, sl] + sums[:, :sub] + tail)
            if diagonal:
                w = jnp.where(before[:, sl], w, 0.0)
            ws[j] = w.astype(BF16)
            tail = tail + sums[:, sub:]
        acc = acc + _dot(jnp.concatenate(ws, axis=1), vb_ref[pl.ds(ks, tq), :])
        return tail, acc

    tail, acc = step(qi, (jnp.zeros((tq, sub), F32), jnp.zeros((tq, HEAD_DIM), F32)), True)

    def alive(c):
        return (c[0] < qi) & (jnp.max(c[1]) > SB_DEAD_TAIL)

    def older(c):
        i, tail, acc = c
        tail, acc = step(qi - 1 - i, (tail, acc), False)
        return i + 1, tail, acc

    _, _, acc = lax.while_loop(alive, older, (jnp.int32(0), tail, acc))
    o_ref[...] = acc


def _sb_attn(q, k, v, tq=512):
    b, t, w = q.shape
    h = w // HEAD_DIM
    tq = min(tq, t)
    assert t % tq == 0
    return pl.pallas_call(
        functools.partial(_sb_attn_kernel, tq=tq),
        grid=(b, h, t // tq),
        in_specs=[
            pl.BlockSpec((None, tq, HEAD_DIM), lambda bi, hi, qi: (bi, qi, hi)),
            pl.BlockSpec((None, t, HEAD_DIM), lambda bi, hi, qi: (bi, 0, hi)),
            pl.BlockSpec((None, t, HEAD_DIM), lambda bi, hi, qi: (bi, 0, hi)),
        ],
        out_specs=pl.BlockSpec((None, tq, HEAD_DIM), lambda bi, hi, qi: (bi, qi, hi)),
        out_shape=jax.ShapeDtypeStruct((b, t, w), F32),
        scratch_shapes=[pltpu.VMEM((t, HEAD_DIM), BF16), pltpu.VMEM((t, HEAD_DIM), BF16)],
        compiler_params=_cparams(("arbitrary", "arbitrary", "arbitrary")),
        name="sb_attn",
    )(q, k, v)


FOX_DEC_PAGES = 2
SB_DEC_PAGES = 4


def _page_scores(k_ref, q, nh, page):
    y = (k_ref[...].reshape(page, nh, HEAD_DIM) * q[None]).astype(BF16)
    s = _dot(y.reshape(page * nh, HEAD_DIM), jnp.ones((HEAD_DIM, HEAD_DIM), BF16))
    return s.reshape(page, nh, HEAD_DIM)


def _fox_dec_kernel(pt_ref, q_ref, kn_ref, vn_ref, fl_ref, bf_ref, *refs, pp):
    k_refs, v_refs, lft_refs = refs[:pp], refs[pp:2 * pp], refs[2 * pp:3 * pp]
    o_ref, lfo_ref, m_ref, l_ref, c_ref, acc_ref = refs[3 * pp:]
    p = pl.program_id(1)
    nh, page = lft_refs[0].shape
    q = q_ref[...] * (SCALE * LOG2E)

    @pl.when(p == 0)
    def _():
        s_new = jnp.sum(q * kn_ref[...], axis=-1, keepdims=True)
        m_ref[...] = jnp.broadcast_to(s_new, m_ref.shape)
        l_ref[...] = jnp.ones_like(l_ref)
        acc_ref[...] = vn_ref[...]
        lf_new, _ = _log_sigmoid_pair(fl_ref[...] + bf_ref[...])
        lfo_ref[...] = lf_new
        c_ref[...] = lf_new

    later = (_iota((page, page), 0) > _iota((page, page), 1)).astype(BF16)
    for i in range(pp):
        lft = lft_refs[i][...]
        decay = (c_ref[...] + _dot3(lft, later)) * LOG2E
        decay = jnp.stack([jnp.broadcast_to(decay[:, r:r + 1], (nh, HEAD_DIM)) for r in range(page)], axis=0)
        s = _page_scores(k_refs[i], q, nh, page) + decay
        m_old = m_ref[...]
        m_new = jnp.maximum(m_old, jnp.max(s, axis=0))
        pr = jnp.exp2(s - m_new[None])
        alpha = jnp.exp2(m_old - m_new)
        l_ref[...] = alpha * l_ref[...] + jnp.sum(pr, axis=0)
        acc_ref[...] = alpha * acc_ref[...] + jnp.sum(pr * v_refs[i][...].reshape(page, nh, HEAD_DIM), axis=0)
        m_ref[...] = m_new
        c_ref[...] = c_ref[...] + jnp.sum(lft, axis=1, keepdims=True)

    @pl.when(p == pl.num_programs(1) - 1)
    def _():
        o_ref[...] = acc_ref[...] / l_ref[...]


def _fox_decode(q, k_new, v_new, fl, b_f, k_pool, v_pool, lf_pool_t, page_table):
    b, nh, _ = q.shape
    n_pages = page_table.shape[1]
    page = lf_pool_t.shape[2]
    pp = math.gcd(FOX_DEC_PAGES, n_pages)

    def row(bi, p, pt):
        return (bi, 0, 0)

    def pg(i):
        return lambda bi, p, pt: (pt[bi, n_pages - 1 - (p * pp + i)], 0, 0)

    head = pl.BlockSpec((None, nh, HEAD_DIM), row)
    stat = pltpu.VMEM((nh, HEAD_DIM), F32)
    grid_spec = pltpu.PrefetchScalarGridSpec(
        num_scalar_prefetch=1,
        grid=(b, n_pages // pp),
        in_specs=[head, head, head, pl.BlockSpec((None, nh, 1), row), pl.BlockSpec((nh, 1), lambda bi, p, pt: (0, 0))]
        + [pl.BlockSpec((None, page * nh, HEAD_DIM), pg(i)) for i in range(pp)] * 2
        + [pl.BlockSpec((None, nh, page), pg(i)) for i in range(pp)],
        out_specs=[head, pl.BlockSpec((None, nh, 1), row)],
        scratch_shapes=[stat, stat, pltpu.VMEM((nh, 1), F32), stat],
    )
    return pl.pallas_call(
        functools.partial(_fox_dec_kernel, pp=pp),
        grid_spec=grid_spec,
        out_shape=[jax.ShapeDtypeStruct((b, nh, HEAD_DIM), F32), jax.ShapeDtypeStruct((b, nh, 1), F32)],
        compiler_params=_cparams(("arbitrary", "arbitrary")),
        name="fox_decode",
    )(page_table, q, k_new, v_new, fl, b_f.reshape(nh, 1), *([k_pool] * pp), *([v_pool] * pp), *([lf_pool_t] * pp))


def _sb_dec_kernel(pt_ref, live_ref, q_ref, c0_ref, acc0_ref, *refs, pp, page):
    k_refs, v_refs = refs[:pp], refs[pp:2 * pp]
    o_ref, cout_ref, c_ref, acc_ref = refs[2 * pp:]
    p = pl.program_id(1)
    nh = q_ref.shape[0]
    q = q_ref[...] * (SCALE * LOG2E)

    @pl.when(p == 0)
    def _():
        c_ref[...] = c0_ref[...]
        acc_ref[...] = acc0_ref[...]

    @pl.when(jnp.max(c_ref[...]) > SB_DEAD_TAIL)
    def _():
        for i in range(pp):
            ls_pos, ls_neg = _log2_sigmoid_pair(_page_scores(k_refs[i], q, nh, page))
            tail = c_ref[...]
            ws = [None] * page
            for r in reversed(range(page)):
                ws[r] = jnp.exp2(ls_pos[r] + tail)
                tail = tail + ls_neg[r]
            acc_ref[...] = acc_ref[...] + jnp.sum(
                jnp.stack(ws, axis=0) * v_refs[i][...].reshape(page, nh, HEAD_DIM), axis=0)
            c_ref[...] = tail

    @pl.when(p == pl.num_programs(1) - 1)
    def _():
        o_ref[...] = acc_ref[...]
        cout_ref[...] = c_ref[...]


def _sb_decode_steps(q, c0, acc0, live, k_pool, v_pool, page_table, page, pp, first, n_steps):
    b, nh, _ = q.shape
    n_pages = page_table.shape[1]

    def row(bi, p, pt, lv):
        return (bi, 0, 0)

    def pg(i):
        def index(bi, p, pt, lv):
            step = first + jnp.where(lv[bi] > 0, p, 0)
            return (pt[bi, n_pages - 1 - (step * pp + i)], 0, 0)
        return index

    head = pl.BlockSpec((None, nh, HEAD_DIM), row)
    stat = pltpu.VMEM((nh, HEAD_DIM), F32)
    grid_spec = pltpu.PrefetchScalarGridSpec(
        num_scalar_prefetch=2,
        grid=(b, n_steps),
        in_specs=[head, head, head] + [pl.BlockSpec((None, page * nh, HEAD_DIM), pg(i)) for i in range(pp)] * 2,
        out_specs=[head, head],
        scratch_shapes=[stat, stat],
    )
    return pl.pallas_call(
        functools.partial(_sb_dec_kernel, pp=pp, page=page),
        grid_spec=grid_spec,
        out_shape=[jax.ShapeDtypeStruct((b, nh, HEAD_DIM), F32)] * 2,
        compiler_params=_cparams(("arbitrary", "arbitrary")),
        name="sb_decode",
    )(page_table, live, q, c0, acc0, *([k_pool] * pp), *([v_pool] * pp))


def _sb_decode(q, k_pool, v_pool, page_table, page):
    b = q.shape[0]
    n_pages = page_table.shape[1]
    pp = math.gcd(SB_DEC_PAGES, n_pages)
    steps = n_pages // pp
    zero = jnp.zeros(q.shape, F32)
    acc, tail = _sb_decode_steps(q, zero, zero, jnp.ones((b,), I32), k_pool, v_pool, page_table, page, pp, 0, 1)
    if steps > 1:
        live = (jnp.max(tail, axis=(1, 2)) > SB_DEAD_TAIL).astype(I32)
        acc, _ = _sb_decode_steps(q, tail, acc, live, k_pool, v_pool, page_table, page, pp, 1, steps - 1)
    return acc


def _cmp_gather_kernel(pt_ref, *refs, inner, page):
    x_refs, o_ref = refs[:-1], refs[-1]
    for i, x_ref in enumerate(x_refs):
        for c in range(2 * NSA_GROUPS):
            o_ref[c, i * page:(i + 1) * page, :] = x_ref[:, c, :]


def _cmp_gather(pool, page_table, page, pp=8):
    b, n_pages = page_table.shape
    inner = pool.shape[1] // page
    half = 2 * NSA_GROUPS
    pool = pool.reshape(pool.shape[0], page, inner // half, half, HEAD_DIM)
    pp = math.gcd(pp, n_pages)
    steps = n_pages // pp

    def page_spec(i):
        return pl.BlockSpec((None, page, None, half, HEAD_DIM), lambda bi, p, pt: (pt[bi, p * pp + i], 0, 0, 0, 0))

    grid_spec = pltpu.PrefetchScalarGridSpec(
        num_scalar_prefetch=1,
        grid=(b, steps),
        in_specs=[page_spec(i) for i in range(pp)],
        out_specs=pl.BlockSpec((2 * NSA_GROUPS, pp * page, HEAD_DIM), lambda bi, p, pt: (0, bi * steps + p, 0)),
    )
    return pl.pallas_call(
        functools.partial(_cmp_gather_kernel, inner=inner, page=page),
        grid_spec=grid_spec,
        out_shape=jax.ShapeDtypeStruct((2 * NSA_GROUPS, b * n_pages * page, HEAD_DIM), F32),
        compiler_params=_cparams(("parallel", "arbitrary")),
        name="nsa_cmp_gather",
    )(page_table, *([pool] * pp))


def _compress_kernel(x_ref, pos_ref, w1_ref, w2_ref, o_ref, *, rb):
    hidden = w1_ref.shape[2]
    hid = jnp.zeros((rb, hidden), F32)
    for c in range(0, CMP_BLOCK, 2):
        xc = [(x_ref[pl.ds(c + i, rb, stride=CMP_BLOCK), :] + pos_ref[c + i:c + i + 1, :]).astype(BF16) for i in (0, 1)]
        w = w1_ref[c:c + 2].reshape(2 * HEAD_DIM, hidden).astype(BF16)
        hid = hid + _dot(jnp.concatenate(xc, axis=1), w)
    hid = hid * _sigmoid(hid)
    o_ref[...] = _dot(hid.astype(BF16), w2_ref[...].astype(BF16))


def _compress(x, pos, w1, w2):
    r = x.shape[-2]
    nb = r // CMP_BLOCK
    rb = _row_tile(nb, 256)
    hidden = w1.shape[-1]
    w1v = w1.reshape(2, CMP_BLOCK, HEAD_DIM, hidden)
    if x.ndim == 2:
        x_spec = pl.BlockSpec((rb * CMP_BLOCK, HEAD_DIM), lambda c, i: (i, c))
    else:
        x_spec = pl.BlockSpec((None, rb * CMP_BLOCK, HEAD_DIM), lambda c, i: (c, i, 0))
    return pl.pallas_call(
        functools.partial(_compress_kernel, rb=rb),
        grid=(2 * NSA_GROUPS, nb // rb),
        in_specs=[
            x_spec,
            pl.BlockSpec((None, CMP_BLOCK, HEAD_DIM), lambda c, i: (c // NSA_GROUPS, 0, 0)),
            pl.BlockSpec((None, CMP_BLOCK, HEAD_DIM, hidden), lambda c, i: (c // NSA_GROUPS, 0, 0, 0)),
            pl.BlockSpec((None, hidden, HEAD_DIM), lambda c, i: (c // NSA_GROUPS, 0, 0)),
        ],
        out_specs=pl.BlockSpec((None, rb, HEAD_DIM), lambda c, i: (c, i, 0)),
        out_shape=jax.ShapeDtypeStruct((2 * NSA_GROUPS, nb, HEAD_DIM), F32),
        compiler_params=_cparams(("parallel", "arbitrary"), 48),
        name="nsa_compress",
    )(x, pos, w1v, w2)


def _gate_col(gl, col):
    lane = _iota(gl.shape, 1)
    return _sigmoid(jnp.sum(jnp.where(lane == col, gl, 0.0), axis=-1, keepdims=True))


def _nsa_cmp_kernel(q_ref, kc_ref, vc_ref, gate_ref, o_ref, sel_ref, *, past, tq, n_cmp, n_blk, t_keys, decode):
    g = pl.program_id(1)
    qi = pl.program_id(2)
    n_pad = kc_ref.shape[0]
    kc = kc_ref[...].astype(BF16)
    vc = vc_ref[...].astype(BF16)
    qpos = past + qi * tq + _iota((tq, 1), 0)
    blk = _iota((tq, n_pad), 1)
    visible = ((blk + 1) * CMP_BLOCK - 1 <= qpos) & (blk < n_cmp)
    gl = gate_ref[...]
    q_all = jnp.concatenate([q_ref[:, r * HEAD_DIM:(r + 1) * HEAD_DIM].astype(BF16) for r in range(NSA_REP)], axis=0)
    s = (_dot_nt(q_all, kc) * SCALE).reshape(NSA_REP, tq, n_pad)
    s = jnp.where(visible[None], s, NEG_INF)
    e = jnp.exp(s - jnp.max(s, axis=-1, keepdims=True))
    pr = jnp.where(visible[None], e / jnp.sum(e, axis=-1, keepdims=True), 0.0)
    imp = jnp.sum(pr, axis=0)
    gate = jnp.concatenate([_gate_col(gl, (g * NSA_REP + r) * 3) for r in range(NSA_REP)], axis=0)
    o = gate * _dot(pr.reshape(NSA_REP * tq, n_pad).astype(BF16), vc)
    o_ref[...] = jnp.concatenate([o[r * tq:(r + 1) * tq] for r in range(NSA_REP)], axis=1)

    cur = _div_pow2(qpos, CMP_BLOCK)
    forced = (blk == 0) | (blk == cur) | (blk == cur - 1)
    score = jnp.where(blk > cur, NEG_INF, jnp.where(forced, FORCE_SCORE, imp))
    if decode:
        row = score[0:1, :]
        col = jnp.sum(jnp.where(_iota((n_pad, n_pad), 0) == _iota((n_pad, n_pad), 1),
                                jnp.broadcast_to(row, (n_pad, n_pad)), 0.0), axis=1, keepdims=True)
        i_blk, j_blk = _iota((n_pad, n_pad), 0), _iota((n_pad, n_pad), 1)
        first = ((col > row) | ((col == row) & (i_blk < j_blk))) & (i_blk < n_blk)
        rank = jnp.sum(jnp.where(first, 1.0, 0.0), axis=0, keepdims=True)
        blk_f = blk[0:1, :].astype(F32)
        lane = _iota((1, N_SEL), 1)
        idx = jnp.zeros((1, N_SEL), F32)
        for k in range(N_SEL):
            ik = jnp.sum(jnp.where(rank == k, blk_f, 0.0), axis=1, keepdims=True)
            idx = jnp.where(lane == k, ik, idx)
        sel_ref[...] = jnp.broadcast_to(idx, sel_ref.shape).astype(I32)
    else:
        rank = jnp.zeros((tq, n_pad), F32)
        for i in range(n_blk):
            col = jnp.sum(jnp.where(blk == i, score, 0.0), axis=1, keepdims=True)
            first = (col > score) | ((col == score) & (i < blk))
            rank = rank + jnp.where(first, 1.0, 0.0)
        sel = jnp.where(rank < N_SEL, 1.0, 0.0).astype(BF16)
        expand = (_div_pow2(_iota((n_pad, t_keys), 1), CMP_BLOCK) == _iota((n_pad, t_keys), 0)).astype(BF16)
        sel_ref[...] = _dot(sel, expand).astype(BF16)


def _nsa_cmp(q, kc, vc, gates, past, n_keys, decode):
    b, t, w = q.shape
    n_pad = kc.shape[2]
    n_cmp = n_keys // CMP_BLOCK
    n_blk = -(-n_keys // CMP_BLOCK)
    assert N_SEL <= n_blk <= n_pad
    tq = min(t, 128)
    gw = w // NSA_GROUPS
    if decode:
        sel_shape, sel_block, sel_dtype = (b, NSA_GROUPS, t, N_SEL), (None, None, tq, N_SEL), I32
    else:
        sel_shape, sel_block, sel_dtype = (b, NSA_GROUPS, t, t), (None, None, tq, t), BF16
    return pl.pallas_call(
        functools.partial(_nsa_cmp_kernel, past=past, tq=tq, n_cmp=n_cmp, n_blk=n_blk, t_keys=t, decode=decode),
        grid=(b, NSA_GROUPS, t // tq),
        in_specs=[
            pl.BlockSpec((None, tq, gw), lambda bi, g, qi: (bi, qi, g)),
            pl.BlockSpec((None, None, n_pad, HEAD_DIM), lambda bi, g, qi: (bi, g, 0, 0)),
            pl.BlockSpec((None, None, n_pad, HEAD_DIM), lambda bi, g, qi: (bi, g, 0, 0)),
            pl.BlockSpec((None, tq, gates.shape[-1]), lambda bi, g, qi: (bi, qi, 0)),
        ],
        out_specs=[
            pl.BlockSpec((None, tq, gw), lambda bi, g, qi: (bi, qi, g)),
            pl.BlockSpec(sel_block, lambda bi, g, qi: (bi, g, qi, 0)),
        ],
        out_shape=[jax.ShapeDtypeStruct((b, t, w), F32), jax.ShapeDtypeStruct(sel_shape, sel_dtype)],
        compiler_params=_cparams(("parallel", "parallel", "arbitrary")),
        name="nsa_cmp_select",
    )(q, kc, vc, gates)


def _t5_tiles_kernel(tab_ref, o_ref, *, tq, tk, n_tiles):
    h = pl.program_id(0)
    d0 = _iota((tq, tk), 0) - _iota((tq, tk), 1)
    for t in range(n_tiles):
        bucket = _t5_bucket(d0 + t * tq)
        acc = jnp.zeros((tq, tk), F32)
        for n in range(N_BUCKETS):
            acc = jnp.where(bucket == n, tab_ref[n, h], acc)
        o_ref[t] = acc * LOG2E


def _t5_tiles(t5_bias, tq, tk):
    nh = t5_bias.shape[1]
    n_tiles = -(-(MAX_DISTANCE + tk - 1) // tq) + 1
    return pl.pallas_call(
        functools.partial(_t5_tiles_kernel, tq=tq, tk=tk, n_tiles=n_tiles),
        grid=(nh,),
        in_specs=[pl.BlockSpec(memory_space=pltpu.SMEM)],
        out_specs=pl.BlockSpec((None, n_tiles, tq, tk), lambda h: (h, 0, 0, 0)),
        out_shape=jax.ShapeDtypeStruct((nh, n_tiles, tq, tk), F32),
        compiler_params=_cparams(("arbitrary",)),
        name="t5_tiles",
    )(t5_bias)


def _nsa_attn_kernel(*refs, tq, windowed, branch):
    if windowed:
        q_ref, k_ref, v_ref, tiles_ref, gate_ref, prev_ref, o_ref, kb_ref, vb_ref = refs
        mask_ref = None
    else:
        q_ref, k_ref, v_ref, tiles_ref, gate_ref, prev_ref, mask_ref, o_ref, kb_ref, vb_ref = refs
    g = pl.program_id(1)
    qi = pl.program_id(2)
    t = k_ref.shape[0]

    @pl.when(qi == 0)
    def _():
        for c in range(t // tq):
            sl = slice(c * tq, (c + 1) * tq)
            kb_ref[sl, :] = k_ref[sl, :].astype(BF16)
            vb_ref[sl, :] = v_ref[sl, :].astype(BF16)

    rows = NSA_REP * tq
    q_all = jnp.concatenate([(q_ref[:, r * HEAD_DIM:(r + 1) * HEAD_DIM] * (SCALE * LOG2E)).astype(BF16)
                             for r in range(NSA_REP)], axis=0)
    tk = tiles_ref.shape[-1]
    per = tk // tq
    far = tiles_ref.shape[1] - 1
    dist0 = _iota((tq, tk), 0) - _iota((tq, tk), 1)

    def body(kc, carry):
        m, l, acc = carry
        ks = pl.multiple_of(kc * tk, tk)
        off = qi - kc * per
        dist = dist0 + off * tq
        ok = dist >= 0
        if windowed:
            ok = ok & (dist < WINDOW)
        else:
            ok = ok & (mask_ref[:, pl.ds(ks, tk)].astype(F32) > 0.5)
        s = _dot_nt(q_all, kb_ref[pl.ds(ks, tk), :]).reshape(NSA_REP, tq, tk)
        s = jnp.where(ok[None], s + tiles_ref[:, jnp.minimum(off, far)], NEG_INF).reshape(rows, tk)
        m_new = jnp.maximum(m, jnp.max(s, axis=-1, keepdims=True))
        p = jnp.exp2(s - m_new)
        if windowed:
            p = jnp.where(s > 0.5 * NEG_INF, p, 0.0)
        alpha = jnp.exp2(m - m_new)
        l = alpha * l + jnp.sum(p, axis=-1, keepdims=True)
        acc = alpha * acc + _dot(p.astype(BF16), vb_ref[pl.ds(ks, tk), :])
        return m_new, l, acc

    lo = jnp.maximum(lax.shift_right_arithmetic(qi * tq - (WINDOW - 1), tk.bit_length() - 1), 0) if windowed else 0
    init = (jnp.full((rows, 1), NEG_INF, F32), jnp.zeros((rows, 1), F32), jnp.zeros((rows, HEAD_DIM), F32))
    _, l, acc = lax.fori_loop(lo, qi // per + 1, body, init)
    gl = gate_ref[...]
    gate = jnp.concatenate([_gate_col(gl, (g * NSA_REP + r) * 3 + branch) for r in range(NSA_REP)], axis=0)
    o = gate * (acc / l)
    o_ref[...] = prev_ref[...] + jnp.concatenate([o[r * tq:(r + 1) * tq] for r in range(NSA_REP)], axis=1)


def _nsa_attn(q, kv, k_blk, v_blk, tiles, gates, prev, mask, branch):
    b, t, w = q.shape
    gw = w // NSA_GROUPS
    n_tiles, tq, tk = tiles.shape[1:]
    assert t % tk == 0 and tk % tq == 0 and tk & (tk - 1) == 0
    windowed = mask is None
    in_specs = [
        pl.BlockSpec((None, tq, gw), lambda bi, g, qi: (bi, qi, g)),
        pl.BlockSpec((None, t, HEAD_DIM), lambda bi, g, qi: (bi, 0, k_blk + g)),
        pl.BlockSpec((None, t, HEAD_DIM), lambda bi, g, qi: (bi, 0, v_blk + g)),
        pl.BlockSpec((NSA_REP, n_tiles, tq, tk), lambda bi, g, qi: (g, 0, 0, 0)),
        pl.BlockSpec((None, tq, gates.shape[-1]), lambda bi, g, qi: (bi, qi, 0)),
        pl.BlockSpec((None, tq, gw), lambda bi, g, qi: (bi, qi, g)),
    ]
    args = [q, kv, kv, tiles, gates, prev]
    if not windowed:
        in_specs.append(pl.BlockSpec((None, None, tq, t), lambda bi, g, qi: (bi, g, qi, 0)))
        args.append(mask)
    return pl.pallas_call(
        functools.partial(_nsa_attn_kernel, tq=tq, windowed=windowed, branch=branch),
        grid=(b, NSA_GROUPS, t // tq),
        in_specs=in_specs,
        out_specs=pl.BlockSpec((None, tq, gw), lambda bi, g, qi: (bi, qi, g)),
        out_shape=jax.ShapeDtypeStruct((b, t, w), F32),
        scratch_shapes=[pltpu.VMEM((t, HEAD_DIM), BF16), pltpu.VMEM((t, HEAD_DIM), BF16)],
        compiler_params=_cparams(("arbitrary", "arbitrary", "arbitrary")),
        name="nsa_window" if windowed else "nsa_select",
    )(*args)


def _t5_rows(tab_t, dist):
    n = dist.shape[1]
    onehot = (_iota((N_BUCKETS, n), 0) == _t5_bucket(dist)).astype(BF16)
    return _dot3(tab_t, onehot)


def _pick_row(x, r):
    return jnp.sum(jnp.where(_iota(x.shape, 0) == r, x, 0.0), axis=0, keepdims=True)


def _nsa_sel_dec_kernel(pt_ref, idx_ref, q_ref, *refs, past, n_past_blk, inner):
    kv_refs = refs[:N_SEL]
    new_ref, tab_ref, gate_ref, prev_ref, o_ref = refs[N_SEL:]
    bi, g = pl.program_id(0), pl.program_id(1)
    k_row = 2 * NSA_GROUPS + g
    v_row = 3 * NSA_GROUPS + g
    new = new_ref[...]
    k_new = jnp.broadcast_to(_pick_row(new, k_row), (CMP_BLOCK, HEAD_DIM))
    v_new = jnp.broadcast_to(_pick_row(new, v_row), (CMP_BLOCK, HEAD_DIM))
    ks, vs, dists = [], [], []
    for j in range(N_SEL):
        blk = idx_ref[(bi * NSA_GROUPS + g) * N_SEL + j]
        is_new = blk >= n_past_blk
        ks.append(jnp.where(is_new, k_new, kv_refs[j][pl.ds(k_row, CMP_BLOCK, stride=inner), :]).astype(BF16))
        vs.append(jnp.where(is_new, v_new, kv_refs[j][pl.ds(v_row, CMP_BLOCK, stride=inner), :]).astype(BF16))
        dists.append(past - (blk * CMP_BLOCK + _iota((1, CMP_BLOCK), 1)))
    dist = jnp.concatenate(dists, axis=1)
    ok = dist >= 0
    s = _dot_nt(q_ref[...].astype(BF16), jnp.concatenate(ks, axis=0)) * SCALE + _t5_rows(tab_ref[...], dist)
    s = jnp.where(ok, s, NEG_INF)
    p = jnp.where(ok, jnp.exp(s - jnp.max(s, axis=-1, keepdims=True)), 0.0)
    o = _dot(p.astype(BF16), jnp.concatenate(vs, axis=0)) / jnp.sum(p, axis=-1, keepdims=True)
    o_ref[...] = prev_ref[...] + _sigmoid(gate_ref[:, 1:2]) * o


def _nsa_sel_decode(q4, pool_blocks, kv_new, tab_t, gates4, prev4, sel_idx, page_table, past, page):
    b = q4.shape[0]
    assert past % page == 0 and page % CMP_BLOCK == 0
    n_past_blk = past // CMP_BLOCK
    per_page = page // CMP_BLOCK
    inner = pool_blocks.shape[1] // CMP_BLOCK

    def blk_spec(j):
        def index(bi, g, pt, idx):
            n = jnp.minimum(idx[(bi * NSA_GROUPS + g) * N_SEL + j], n_past_blk - 1)
            return (pt[bi, n // per_page] * per_page + n % per_page, 0, 0)
        return pl.BlockSpec((None, CMP_BLOCK * inner, HEAD_DIM), index)

    def grp(bi, g, pt, idx):
        return (bi, g, 0, 0)

    grid_spec = pltpu.PrefetchScalarGridSpec(
        num_scalar_prefetch=2,
        grid=(b, NSA_GROUPS),
        in_specs=[pl.BlockSpec((None, None, NSA_REP, HEAD_DIM), grp)] + [blk_spec(j) for j in range(N_SEL)] + [
            pl.BlockSpec((None, inner, HEAD_DIM), lambda bi, g, pt, idx: (bi, 0, 0)),
            pl.BlockSpec((NSA_REP, N_BUCKETS), lambda bi, g, pt, idx: (g, 0)),
            pl.BlockSpec((None, None, NSA_REP, 3), grp),
            pl.BlockSpec((None, None, NSA_REP, HEAD_DIM), grp),
        ],
        out_specs=pl.BlockSpec((None, None, NSA_REP, HEAD_DIM), grp),
    )
    return pl.pallas_call(
        functools.partial(_nsa_sel_dec_kernel, past=past, n_past_blk=n_past_blk, inner=inner),
        grid_spec=grid_spec,
        out_shape=jax.ShapeDtypeStruct(q4.shape, F32),
        compiler_params=_cparams(("parallel", "parallel")),
        name="nsa_select_decode",
    )(page_table, sel_idx, q4, *([pool_blocks] * N_SEL), kv_new, tab_t, gates4, prev4)


def _nsa_win_dec_kernel(q_ref, win_ref, new_ref, tab_ref, gate_ref, prev_ref, o_ref, *, lw):
    g = pl.program_id(1)
    inner = 2 * NSA_GROUPS
    q = q_ref[...].astype(BF16)
    k = win_ref[pl.ds(g, lw, stride=inner), :].astype(BF16)
    v = win_ref[pl.ds(NSA_GROUPS + g, lw, stride=inner), :].astype(BF16)
    new = new_ref[...]
    dist = lw - _iota((1, lw), 1)
    ok = dist < WINDOW
    tab = tab_ref[...]
    s = jnp.where(ok, _dot_nt(q, k) * SCALE + _t5_rows(tab, dist), NEG_INF)
    kn = _pick_row(new, g).astype(BF16).astype(F32)
    s_new = jnp.sum(q.astype(F32) * kn, axis=-1, keepdims=True) * SCALE + tab[:, 0:1]
    m = jnp.maximum(jnp.max(s, axis=-1, keepdims=True), s_new)
    p = jnp.where(ok, jnp.exp(s - m), 0.0)
    p_new = jnp.exp(s_new - m)
    l = jnp.sum(p, axis=-1, keepdims=True) + p_new
    o = (_dot(p.astype(BF16), v) + p_new * _pick_row(new, NSA_GROUPS + g)) / l
    o_ref[...] = prev_ref[...] + _sigmoid(gate_ref[:, 2:3]) * o


def _nsa_win_decode(q4, win_past, win_new, tab_t, gates4, prev4):
    b = q4.shape[0]
    inner = 2 * NSA_GROUPS
    lw = win_past.shape[1] // inner
    assert lw == WINDOW

    def grp(bi, g):
        return (bi, g, 0, 0)

    return pl.pallas_call(
        functools.partial(_nsa_win_dec_kernel, lw=lw),
        grid=(b, NSA_GROUPS),
        in_specs=[
            pl.BlockSpec((None, None, NSA_REP, HEAD_DIM), grp),
            pl.BlockSpec((None, lw * inner, HEAD_DIM), lambda bi, g: (bi, 0, 0)),
            pl.BlockSpec((None, inner, HEAD_DIM), lambda bi, g: (bi, 0, 0)),
            pl.BlockSpec((NSA_REP, N_BUCKETS), lambda bi, g: (g, 0)),
            pl.BlockSpec((None, None, NSA_REP, 3), grp),
            pl.BlockSpec((None, None, NSA_REP, HEAD_DIM), grp),
        ],
        out_specs=pl.BlockSpec((None, None, NSA_REP, HEAD_DIM), grp),
        out_shape=jax.ShapeDtypeStruct(q4.shape, F32),
        compiler_params=_cparams(("parallel", "parallel")),
        name="nsa_window_decode",
    )(q4, win_past, win_new, tab_t, gates4, prev4)


def _lower_bound(lbp, layer, axis):
    e = jnp.exp(lbp - jnp.max(lbp, axis=axis, keepdims=True))
    sm = e / jnp.sum(e, axis=axis, keepdims=True)
    idx = _iota(lbp.shape, axis)
    return jnp.sum(jnp.where((idx >= 1) & (idx <= layer), sm, 0.0), axis=axis, keepdims=True)


def _gla_kernel(q_ref, f_ref, i_ref, lbp_ref, s0_ref, ng_ref, o_ref, s_ref, st_ref, *, layer, chunk, n_chunks, hp):
    ti = pl.program_id(2)
    lb = _lower_bound(lbp_ref[...], layer, 0)
    tril = _iota((chunk, chunk), 0) >= _iota((chunk, chunk), 1)
    tril_b = tril.astype(BF16)
    ng = ng_ref[...]

    @pl.when(ti == 0)
    def _():
        for r in range(hp):
            st_ref[r] = s0_ref[r].T

    def heads(x):
        return jnp.stack([x[:, r * HEAD_DIM:(r + 1) * HEAD_DIM] for r in range(hp)], axis=0)

    def bdot(a, b, ca, cb):
        return lax.dot_general(a, b, (((ca,), (cb,)), ((0,), (0,))), preferred_element_type=F32)

    def body(c, _):
        r0 = pl.multiple_of(c * chunk, chunk)
        qv = q_ref[pl.ds(r0, chunk), :]
        q = qv * _sigmoid(qv)
        g = lb + (1.0 - lb) * _sigmoid(f_ref[pl.ds(r0, chunk), :])
        k = 1.0 - g
        b = _dot3_rhs(tril_b, jnp.log(g))
        b_last = b[chunk - 1:chunk, :]
        qe = heads((q * jnp.exp(b)).astype(BF16))
        ke = heads((k * jnp.exp(-b)).astype(BF16))
        kt = heads((k * jnp.exp(b_last - b)).astype(BF16))
        v = heads(i_ref[pl.ds(r0, chunk), :].astype(BF16))
        st = st_ref[...]
        att = jnp.where(tril[None], bdot(qe, ke, 2, 2), 0.0)
        o = bdot(att.astype(BF16), v, 2, 1) + bdot(qe, st.astype(BF16), 2, 2)
        st_ref[...] = heads(jnp.exp(b_last)) * st + bdot(v, kt, 1, 1)
        inv = lax.rsqrt(jnp.mean(o * o, axis=-1, keepdims=True) + RMS_EPS)
        o = o * inv * ng
        o_ref[pl.ds(r0, chunk), :] = jnp.concatenate([o[r] for r in range(hp)], axis=1)
        return 0

    lax.fori_loop(0, n_chunks, body, 0)

    @pl.when(ti == pl.num_programs(2) - 1)
    def _():
        for r in range(hp):
            s_ref[r] = st_ref[r].T


def _gla(q, f, i_in, lbp, s0, norm_g, layer, hp=16, tt=512):
    b, t, w = q.shape
    nh = w // HEAD_DIM
    chunk = math.gcd(t, GLA_CHUNK)
    depth = lbp.shape[0]
    tt = min(tt, t)
    assert nh % hp == 0 and t % tt == 0 and tt % chunk == 0
    tok = pl.BlockSpec((None, tt, hp * HEAD_DIM), lambda bi, h, ti: (bi, ti, h))
    st = pl.BlockSpec((None, hp, HEAD_DIM, HEAD_DIM), lambda bi, h, ti: (bi, h, 0, 0))
    return pl.pallas_call(
        functools.partial(_gla_kernel, layer=layer, chunk=chunk, n_chunks=tt // chunk, hp=hp),
        grid=(b, nh // hp, t // tt),
        in_specs=[tok, tok, tok, pl.BlockSpec((depth, hp * HEAD_DIM), lambda bi, h, ti: (0, h)), st,
                  pl.BlockSpec((1, HEAD_DIM), lambda bi, h, ti: (0, 0))],
        out_specs=[tok, st],
        out_shape=[jax.ShapeDtypeStruct((b, t, w), F32), jax.ShapeDtypeStruct(s0.shape, F32)],
        scratch_shapes=[pltpu.VMEM((hp, HEAD_DIM, HEAD_DIM), F32)],
        compiler_params=_cparams(("arbitrary", "arbitrary", "arbitrary")),
        name="hgrn_gla",
    )(q, f, i_in, lbp, s0, norm_g.reshape(1, HEAD_DIM))


def _gla_dec_kernel(q_ref, f_ref, v_ref, lbp_ref, s0_ref, ng_ref, o_ref, s_ref, *, layer):
    lb = _lower_bound(lbp_ref[...], layer, 0)[0]
    qv = q_ref[...]
    q = qv * _sigmoid(qv)
    g = lb + (1.0 - lb) * _sigmoid(f_ref[...])
    s = g * s0_ref[...] + (1.0 - g) * v_ref[...]
    s_ref[...] = s
    o = jnp.sum(q * s, axis=1, keepdims=True)
    inv = lax.rsqrt(jnp.mean(o * o, axis=-1, keepdims=True) + RMS_EPS)
    o_ref[...] = o * inv * ng_ref[...]


def _gla_decode(q, f, i_in, lbp, s0, norm_g, layer):
    b, nh = s0.shape[:2]
    depth = lbp.shape[0]
    col = pl.BlockSpec((None, nh, HEAD_DIM, 1), lambda bi: (bi, 0, 0, 0))
    row = pl.BlockSpec((None, nh, 1, HEAD_DIM), lambda bi: (bi, 0, 0, 0))
    st = pl.BlockSpec((None, nh, HEAD_DIM, HEAD_DIM), lambda bi: (bi, 0, 0, 0))
    o, s = pl.pallas_call(
        functools.partial(_gla_dec_kernel, layer=layer),
        grid=(b,),
        in_specs=[col, col, row, pl.BlockSpec((depth, nh, HEAD_DIM, 1), lambda bi: (0, 0, 0, 0)), st,
                  pl.BlockSpec((1, HEAD_DIM), lambda bi: (0, 0))],
        out_specs=[row, st],
        out_shape=[jax.ShapeDtypeStruct((b, nh, 1, HEAD_DIM), F32), jax.ShapeDtypeStruct(s0.shape, F32)],
        compiler_params=_cparams(("parallel",)),
        name="hgrn_step",
    )(q.reshape(b, nh, HEAD_DIM, 1), f.reshape(b, nh, HEAD_DIM, 1), i_in.reshape(b, nh, 1, HEAD_DIM),
      lbp.reshape(depth, nh, HEAD_DIM, 1), s0, norm_g.reshape(1, HEAD_DIM))
    return o.reshape(b, 1, nh * HEAD_DIM), s


def _as_rows(w):
    if w.shape[1] % 128:
        return jnp.swapaxes(w, 0, 1), True
    return w, False


def _fox_layer(x, bt, norm_g, w_in, b_f, past):
    b, t = bt
    d = x.shape[1]
    nh = d // HEAD_DIM
    wt, tr = _as_rows(w_in)
    assert tr
    q, k, v, z = _norm_proj(x, norm_g, wt, [(n * d, d) for n in range(4)], 512, True)
    (fl,) = _norm_proj(x, norm_g, wt, [(4 * d, nh)], nh, True)
    if past is None:
        q3, k3, v3 = (a.reshape(b, t, d) for a in (q, k, v))
        logf, cum = _fox_prep(fl.reshape(b, t, nh), b_f)
        o = _fox_attn(q3, k3, v3, cum)
    else:
        k_pool, v_pool, lf_pool, page_table = past
        npool, page = k_pool.shape[:2]
        o, logf = _fox_decode(q.reshape(b, nh, HEAD_DIM), k.reshape(b, nh, HEAD_DIM), v.reshape(b, nh, HEAD_DIM),
                              fl.reshape(b, nh, 1), b_f, k_pool.reshape(npool, page * nh, HEAD_DIM),
                              v_pool.reshape(npool, page * nh, HEAD_DIM), jnp.swapaxes(lf_pool, 1, 2), page_table)
    state = (k.reshape(1, b, t, nh, HEAD_DIM), v.reshape(1, b, t, nh, HEAD_DIM), logf.reshape(1, b, t, nh))
    return o.reshape(b * t, d), z, state


def _pad_summaries(summaries, b, n_cmp, n_pad):
    s = summaries.reshape(2, NSA_GROUPS, b, n_cmp, HEAD_DIM)
    return jnp.pad(jnp.swapaxes(s, 1, 2), ((0, 0), (0, 0), (0, 0), (0, n_pad - n_cmp), (0, 0)))


def _nsa_layer(x, bt, norm_g, w_in, cmp_pos, cmp_w1, cmp_w2, t5_bias, past):
    b, t = bt
    d = x.shape[1]
    kvw = NSA_GROUPS * HEAD_DIM
    off_gate = d + 6 * kvw
    n_gate = 3 * (d // HEAD_DIM)
    wt, tr = _as_rows(w_in)
    assert tr
    q, kv4, win = _norm_proj(x, norm_g, wt, [(0, d), (d, 4 * kvw), (d + 4 * kvw, 2 * kvw)], 512, True)
    (gates,) = _norm_proj(x, norm_g, wt[off_gate:off_gate + n_gate], [(0, n_gate)], n_gate, True)
    (z,) = _norm_proj(x, norm_g, wt[off_gate + n_gate:], [(0, d)], 512, True)
    q3, gates3 = q.reshape(b, t, d), gates.reshape(b, t, n_gate)
    inner = NSA_KINDS * NSA_GROUPS
    if past is None:
        kv3, win3 = kv4.reshape(b, t, 4 * kvw), win.reshape(b, t, 2 * kvw)
        n_cmp = t // CMP_BLOCK
        n_pad = -(-n_cmp // 128) * 128
        summaries = _pad_summaries(_compress(kv4, cmp_pos, cmp_w1, cmp_w2), b, n_cmp, n_pad)
        o, mask = _nsa_cmp(q3, summaries[0], summaries[1], gates3, 0, t, False)
        o = _nsa_attn(q3, kv3, 4, 6, _t5_tiles(t5_bias, 128, math.gcd(t, 512)), gates3, o, mask, 1)
        o = _nsa_attn(q3, win3, 0, 2, _t5_tiles(t5_bias, 128, math.gcd(t, 256)), gates3, o, None, 2)
        keep = min(WINDOW, t)
        win_state = win3[:, t - keep:]
    else:
        kv_pool, win_past, page_table = past
        assert t == 1
        npool, page = kv_pool.shape[:2]
        n_pages = page_table.shape[1]
        plen = n_pages * page
        pool = kv_pool.reshape(npool, page * inner, HEAD_DIM)
        n_cmp = plen // CMP_BLOCK
        n_pad = -(-(n_cmp + 1) // 128) * 128
        summaries = _pad_summaries(_compress(_cmp_gather(pool, page_table, page), cmp_pos, cmp_w1, cmp_w2),
                                   b, n_cmp, n_pad)
        rows = SUBLANES
        q_pad = jnp.pad(q3, ((0, 0), (0, rows - t), (0, 0)))
        g_pad = jnp.pad(gates3, ((0, 0), (0, rows - t), (0, 0)))
        o, sel_idx = _nsa_cmp(q_pad, summaries[0], summaries[1], g_pad, plen, plen + t, True)
        o4 = o[:, :1].reshape(b, NSA_GROUPS, NSA_REP, HEAD_DIM)
        sel_idx = sel_idx[:, :, 0, :].reshape(-1)
        q4 = q3.reshape(b, NSA_GROUPS, NSA_REP, HEAD_DIM)
        gates4 = gates3.reshape(b, NSA_GROUPS, NSA_REP, 3)
        tab_t = t5_bias.T
        blocks = pool.reshape(npool * page // CMP_BLOCK, CMP_BLOCK * inner, HEAD_DIM)
        o4 = _nsa_sel_decode(q4, blocks, kv4.reshape(b, inner, HEAD_DIM), tab_t, gates4, o4, sel_idx,
                             page_table, plen, page)
        lw = win_past.shape[1]
        wp = win_past.reshape(b, lw * 2 * NSA_GROUPS, HEAD_DIM)
        wn = win.reshape(b, 2 * NSA_GROUPS, HEAD_DIM)
        o4 = _nsa_win_decode(q4, wp, wn, tab_t, gates4, o4)
        o = o4.reshape(b, t, d)
        keep = min(WINDOW, lw + t)
        win_state = jnp.concatenate([wp, wn], axis=1)[:, (lw + t - keep) * 2 * NSA_GROUPS:]
    state = (kv4.reshape(1, b, t, NSA_KINDS, NSA_GROUPS, HEAD_DIM),
             win_state.reshape(1, b, -1, 2, NSA_GROUPS, HEAD_DIM))
    return o.reshape(b * t, d), z, state


def _hgrn_layer(x, bt, norm_g, w_in, lbp, hgrn_norm_g, layer, s0):
    b, t = bt
    d = x.shape[1]
    q, f, i_in, z = _norm_proj(x, norm_g, w_in, [(n * d, d) for n in range(4)], 512)
    if t == 1:
        o, s_new = _gla_decode(q, f, i_in, lbp, s0, hgrn_norm_g, layer)
    else:
        o, s_new = _gla(q.reshape(b, t, d), f.reshape(b, t, d), i_in.reshape(b, t, d), lbp, s0, hgrn_norm_g, layer)
    return o.reshape(b * t, d), z, (s_new[None],)


def _sb_layer(x, bt, norm_g, w_in, past):
    b, t = bt
    d = x.shape[1]
    nh = d // HEAD_DIM
    q, k, v, z = _norm_proj(x, norm_g, w_in, [(n * d, d) for n in range(4)], 512)
    if past is None:
        o = _sb_attn(q.reshape(b, t, d), k.reshape(b, t, d), v.reshape(b, t, d))
    else:
        k_pool, v_pool, page_table = past
        npool, page = k_pool.shape[:2]
        o = _sb_decode(q.reshape(b, nh, HEAD_DIM), k_pool.reshape(npool, page * nh, HEAD_DIM),
                       v_pool.reshape(npool, page * nh, HEAD_DIM), page_table, page)
    state = (k.reshape(1, b, t, nh, HEAD_DIM), v.reshape(1, b, t, nh, HEAD_DIM))
    return o.reshape(b * t, d), z, state


def kernel(x_prompt, x_sample, cache_fox_k, cache_fox_v, cache_fox_logf, cache_nsa_kv, state_nsa_win, state_hgrn, cache_sb_k, cache_sb_v, page_table, p_prompt, p_sample, norm_g, final_norm_g, ple_gate_w, ple_proj_w, fox_w_in, fox_b_f, fox_w_out, nsa_w_in, nsa_cmp_pos, nsa_cmp_w1, nsa_cmp_w2, t5_bias, nsa_w_out, hgrn_w_in, hgrn_lower_bounds, hgrn_norm_g, hgrn_w_out, sb_w_in, sb_w_out):
    depth = norm_g.shape[0]
    assert depth == 4 and all(a.shape[0] == 1 for a in (fox_w_in, nsa_w_in, hgrn_w_in, sb_w_in))
    w_out = [w[0].astype(BF16) for w in (fox_w_out, nsa_w_out, hgrn_w_out, sb_w_out)]
    w_gate = [ple_gate_w[i].astype(BF16) for i in range(depth)]
    w_proj = [ple_proj_w[i].astype(BF16) for i in range(depth)]

    def trunk(x3, p4, decode):
        b, t, d = x3.shape
        bt = (b, t)
        x = x3.reshape(b * t, d)
        p = p4.reshape(depth, b * t, -1)
        if decode:
            past = ((cache_fox_k[0], cache_fox_v[0], cache_fox_logf[0], page_table),
                    (cache_nsa_kv[0], state_nsa_win[0], page_table),
                    state_hgrn[0],
                    (cache_sb_k[0], cache_sb_v[0], page_table))
        else:
            past = (None, None, jnp.zeros((b,) + state_hgrn.shape[2:], F32), None)
        o, z, st_a = _fox_layer(x, bt, norm_g[0], fox_w_in[0], fox_b_f[0], past[0])
        x = _layer_tail(o, z, x, p[0], w_out[0], w_gate[0], w_proj[0])
        o, z, st_b = _nsa_layer(x, bt, norm_g[1], nsa_w_in[0], nsa_cmp_pos[0], nsa_cmp_w1[0], nsa_cmp_w2[0],
                                t5_bias, past[1])
        x = _layer_tail(o, z, x, p[1], w_out[1], w_gate[1], w_proj[1])
        o, z, st_c = _hgrn_layer(x, bt, norm_g[2], hgrn_w_in[0], hgrn_lower_bounds, hgrn_norm_g[0], 2, past[2])
        x = _layer_tail(o, z, x, p[2], w_out[2], w_gate[2], w_proj[2])
        o, z, st_d = _sb_layer(x, bt, norm_g[3], sb_w_in[0], past[3])
        y = _layer_tail(o, z, x, p[3], w_out[3], w_gate[3], w_proj[3], final_norm_g).reshape(b, t, d)
        return y, st_a, st_b, st_c, st_d

    yp, pa, pb, pc, pd = trunk(x_prompt, p_prompt, False)
    ys, sa, sb, sc, sd = trunk(x_sample, p_sample, True)
    return (yp, ys, pa[0], sa[0], pa[1], sa[1], pa[2], sa[2], pb[0], sb[0], pb[1], sb[1],
            pc[0], sc[0], pd[0], sd[0], pd[1], sd[1])
```
